```python
import jax, jax.numpy as jnp
from jax import lax
import numpy as np

D_MODEL = 1024
BATCH = 8
SEQ = 2048
DEPTH = 4

CHUNK = 64
Q_BLOCK = 128
N_MIXERS = 3
N_FOX = (DEPTH + 2) // 3
N_GLA = (DEPTH + 1) // 3
N_HGRN = DEPTH // 3
EPS = 1e-6

FOX_HEADS = 16
FOX_HEAD_DIM = D_MODEL // FOX_HEADS
FOX_IN = 4 * D_MODEL + FOX_HEADS

GLA_HEADS = 4
GLA_DK = D_MODEL // 2 // GLA_HEADS
GLA_DV = D_MODEL // GLA_HEADS
GLA_RANK = 16
GLA_TAU = 16.0
GLA_IN = 2 * GLA_HEADS * GLA_DK + 2 * D_MODEL + GLA_RANK

HGRN_EXPAND = 128
HGRN_HEADS = D_MODEL // HGRN_EXPAND
HGRN_DV = D_MODEL // HGRN_HEADS
HGRN_IN = 4 * D_MODEL

PEER_HEADS = 8
PEER_NKEYS = 128
PEER_EXPERTS = PEER_NKEYS * PEER_NKEYS
PEER_DQ = 256
PEER_TOPK = 16
PEER_TOKEN_BLOCK = 128

kernel_name = "hybrid_fox_gla_hgrn2_peer_trunk"


def rms_norm(x, g):
    xf = x.astype(jnp.float32)
    y = xf * lax.rsqrt(jnp.mean(xf * xf, axis=-1, keepdims=True) + EPS)
    return (y * g.astype(jnp.float32)).astype(x.dtype)


def to_heads(a, n_heads):
    B, T, W = a.shape
    return a.reshape(B, T, n_heads, W // n_heads).transpose(0, 2, 1, 3)


def from_heads(a):
    B, H, T, d = a.shape
    return a.transpose(0, 2, 1, 3).reshape(B, T, H * d)


def forgetting_attention(q, k, v, log_f):
    B, H, T, d = q.shape
    c = jnp.cumsum(log_f, axis=-1)
    scale = d ** -0.5
    outs = []
    for i in range(T // Q_BLOCK):
        lo, hi = i * Q_BLOCK, (i + 1) * Q_BLOCK
        s = jnp.einsum('bhqd,bhkd->bhqk', q[:, :, lo:hi], k[:, :, :hi]).astype(jnp.float32) * scale
        s = s + (c[:, :, lo:hi, None] - c[:, :, None, :hi])
        mask = (lo + jnp.arange(Q_BLOCK))[:, None] >= jnp.arange(hi)[None, :]
        p = jax.nn.softmax(jnp.where(mask, s, -jnp.inf), axis=-1)
        outs.append(jnp.einsum('bhqk,bhkd->bhqd', p.astype(v.dtype), v[:, :, :hi]))
    return jnp.concatenate(outs, axis=2)


def chunked_gated_linear_attention(q, k, v, log_g):
    B, H, T, dk = q.shape
    dv = v.shape[-1]
    n = T // CHUNK

    def to_chunks(a):
        return a.astype(jnp.float32).reshape(B, H, n, CHUNK, a.shape[-1]).transpose(2, 0, 1, 3, 4)

    causal = jnp.tril(jnp.ones((CHUNK, CHUNK), dtype=bool))[:, :, None]

    def step(S, inp):
        qc, kc, vc, gc = inp
        b = jnp.cumsum(gc, axis=2)
        diff = b[:, :, :, None, :] - b[:, :, None, :, :]
        decay = jnp.where(causal, jnp.exp(jnp.where(causal, diff, 0.0)), 0.0)
        scores = jnp.einsum('bhtd,bhsd,bhtsd->bhts', qc, kc, decay)
        o = jnp.einsum('bhts,bhsv->bhtv', scores, vc) + jnp.einsum('bhtd,bhdv->bhtv', qc * jnp.exp(b), S)
        b_end = b[:, :, -1:, :]
        S = S * jnp.exp(b_end[:, :, 0, :, None]) + jnp.einsum('bhsd,bhsv->bhdv', kc * jnp.exp(b_end - b), vc)
        return S, o

    S0 = jnp.zeros((B, H, dk, dv), jnp.float32)
    _, o = lax.scan(step, S0, (to_chunks(q), to_chunks(k), to_chunks(v), to_chunks(log_g)))
    return o.transpose(1, 2, 0, 3, 4).reshape(B, H, T, dv).astype(v.dtype)


def fox_mixer(x, w_in, b_f, q_gain, k_gain, w_out):
    proj = x @ w_in
    q, k, v, g, f = jnp.split(proj, [D_MODEL, 2 * D_MODEL, 3 * D_MODEL, 4 * D_MODEL], axis=-1)
    q = rms_norm(to_heads(q, FOX_HEADS), q_gain)
    k = rms_norm(to_heads(k, FOX_HEADS), k_gain)
    log_f = jax.nn.log_sigmoid((f + b_f).astype(jnp.float32)).transpose(0, 2, 1)
    o = forgetting_attention(q, k, to_heads(v, FOX_HEADS), log_f)
    o = o * jax.nn.sigmoid(to_heads(g, FOX_HEADS))
    return from_heads(o) @ w_out


def gla_mixer(x, w_in, w_up, b_alpha, out_gain, w_out):
    gk = GLA_HEADS * GLA_DK
    proj = x @ w_in
    q, k, v, r, low = jnp.split(proj, [gk, 2 * gk, 2 * gk + D_MODEL, 2 * gk + 2 * D_MODEL], axis=-1)
    log_a = jax.nn.log_sigmoid((low @ w_up + b_alpha).astype(jnp.float32)) / GLA_TAU
    o = chunked_gated_linear_attention(to_heads(q, GLA_HEADS) * (GLA_DK ** -0.5), to_heads(k, GLA_HEADS),
                                       to_heads(v, GLA_HEADS), to_heads(log_a, GLA_HEADS))
    o = rms_norm(o, out_gain[:, None, :]) * jax.nn.silu(to_heads(r, GLA_HEADS))
    return from_heads(o) @ w_out


def hgrn2_mixer(x, w_in, lb, out_gain, w_out):
    proj = x @ w_in
    q, f, i, g = jnp.split(proj, 4, axis=-1)
    gate = lb + (1.0 - lb) * jax.nn.sigmoid(f.astype(jnp.float32))
    k = 1.0 - gate
    o = chunked_gated_linear_attention(to_heads(q, HGRN_HEADS) * (HGRN_EXPAND ** -0.5), to_heads(k, HGRN_HEADS),
                                       to_heads(i, HGRN_HEADS), to_heads(jnp.log(gate), HGRN_HEADS))
    o = rms_norm(o, out_gain[:, None, :]) * jax.nn.silu(to_heads(g, HGRN_HEADS))
    return from_heads(o) @ w_out


def peer_mixer(x, w_q, sub_keys, u, v):
    B, T, D = x.shape
    N = B * T
    xt = x.reshape(N, D)
    q = (xt @ w_q).reshape(N, PEER_HEADS, 2, PEER_DQ // 2)
    s = jnp.einsum('nhpd,hpkd->nhpk', q, sub_keys).astype(jnp.float32)
    s_top, i_top = lax.top_k(s, PEER_TOPK)
    cand = (s_top[:, :, 0, :, None] + s_top[:, :, 1, None, :]).reshape(N, PEER_HEADS, PEER_TOPK * PEER_TOPK)
    cand_idx = (i_top[:, :, 0, :, None] * PEER_NKEYS + i_top[:, :, 1, None, :]).reshape(N, PEER_HEADS, PEER_TOPK * PEER_TOPK)
    best, pos = lax.top_k(cand, PEER_TOPK)
    experts = jnp.take_along_axis(cand_idx, pos, axis=-1)
    gates = jax.nn.softmax(best, axis=-1)
    n_blk = N // PEER_TOKEN_BLOCK

    def apply(args):
        xb, eb, gb = args
        h = jax.nn.gelu(jnp.einsum('tpkd,td->tpk', u[eb], xb).astype(jnp.float32), approximate=False)
        return jnp.einsum('tpk,tpkd->td', (gb * h).astype(v.dtype), v[eb])

    y = lax.map(apply, (xt.reshape(n_blk, PEER_TOKEN_BLOCK, D),
                        experts.reshape(n_blk, PEER_TOKEN_BLOCK, PEER_HEADS, PEER_TOPK),
                        gates.reshape(n_blk, PEER_TOKEN_BLOCK, PEER_HEADS, PEER_TOPK)))
    return y.reshape(B, T, D)


def setup_inputs(seed: int = 0) -> dict:
    key = jax.random.key(seed)
    ks = jax.random.split(key, 24)
    nrm = jax.random.normal
    D = D_MODEL
    f32 = jnp.float32
    return {
        "x": nrm(ks[0], (BATCH, SEQ, D), f32),
        "norm_mix": 1.0 + 0.05 * nrm(ks[1], (DEPTH, D), f32),
        "norm_ffn": 1.0 + 0.05 * nrm(ks[2], (DEPTH, D), f32),
        "fox_w_in": nrm(ks[3], (N_FOX, D, FOX_IN), f32) * D ** -0.5,
        "fox_b_f": 2.0 + 0.5 * nrm(ks[4], (N_FOX, FOX_HEADS), f32),
        "fox_q_gain": 1.0 + 0.05 * nrm(ks[5], (N_FOX, FOX_HEAD_DIM), f32),
        "fox_k_gain": 1.0 + 0.05 * nrm(ks[6], (N_FOX, FOX_HEAD_DIM), f32),
        "fox_w_out": nrm(ks[7], (N_FOX, D, D), f32) * D ** -0.5,
        "gla_w_in": nrm(ks[8], (N_GLA, D, GLA_IN), f32) * D ** -0.5,
        "gla_w_up": nrm(ks[9], (N_GLA, GLA_RANK, GLA_HEADS * GLA_DK), f32) * GLA_RANK ** -0.5,
        "gla_b_alpha": 0.1 * nrm(ks[10], (N_GLA, GLA_HEADS * GLA_DK), f32),
        "gla_out_gain": 1.0 + 0.05 * nrm(ks[11], (N_GLA, GLA_HEADS, GLA_DV), f32),
        "gla_w_out": nrm(ks[12], (N_GLA, D, D), f32) * D ** -0.5,
        "hgrn_w_in": nrm(ks[13], (N_HGRN, D, HGRN_IN), f32) * D ** -0.5,
        "hgrn_lb_logits": 0.5 * nrm(ks[14], (DEPTH, D), f32),
        "hgrn_out_gain": 1.0 + 0.05 * nrm(ks[15], (N_HGRN, HGRN_HEADS, HGRN_DV), f32),
        "hgrn_w_out": nrm(ks[16], (N_HGRN, D, D), f32) * D ** -0.5,
        "peer_w_q": nrm(ks[17], (DEPTH, D, PEER_HEADS * PEER_DQ), f32) * D ** -0.5,
        "peer_sub_keys": nrm(ks[18], (DEPTH, PEER_HEADS, 2, PEER_NKEYS, PEER_DQ // 2), f32) * (PEER_DQ // 2) ** -0.5,
        "peer_u": nrm(ks[19], (DEPTH, PEER_EXPERTS, D), f32) * D ** -0.5,
        "peer_v": nrm(ks[20], (DEPTH, PEER_EXPERTS, D), f32) * (PEER_HEADS * PEER_TOPK) ** -0.5,
    }


def reference(x, norm_mix, norm_ffn, fox_w_in, fox_b_f, fox_q_gain, fox_k_gain, fox_w_out,
              gla_w_in, gla_w_up, gla_b_alpha, gla_out_gain, gla_w_out,
              hgrn_w_in, hgrn_lb_logits, hgrn_out_gain, hgrn_w_out,
              peer_w_q, peer_sub_keys, peer_u, peer_v):
    p = jax.nn.softmax(hgrn_lb_logits.astype(jnp.float32), axis=0)
    lower_bounds = jnp.cumsum(p, axis=0) - p[0:1]
    h = x
    for i in range(DEPTH):
        m, j = i % N_MIXERS, i // N_MIXERS
        hn = rms_norm(h, norm_mix[i])
        if m == 0:
            y = fox_mixer(hn, fox_w_in[j], fox_b_f[j], fox_q_gain[j], fox_k_gain[j], fox_w_out[j])
        elif m == 1:
            y = gla_mixer(hn, gla_w_in[j], gla_w_up[j], gla_b_alpha[j], gla_out_gain[j], gla_w_out[j])
        else:
            y = hgrn2_mixer(hn, hgrn_w_in[j], lower_bounds[i], hgrn_out_gain[j], hgrn_w_out[j])
        h = h + y.astype(h.dtype)
        h = h + peer_mixer(rms_norm(h, norm_ffn[i]), peer_w_q[i], peer_sub_keys[i], peer_u[i], peer_v[i]).astype(h.dtype)
    return h
```

```python
import functools
import math

import numpy as np
import jax
import jax.numpy as jnp
from jax import lax
from jax.experimental import pallas as pl
from jax.experimental.pallas import tpu as pltpu

F32 = jnp.float32
BF16 = jnp.bfloat16
EPS = 1e-6

DEPTH = 4
N_MIXERS = 3
CHUNK = 64
FOX_HEADS = 16
FOX_HEAD_DIM = 64
GLA_HEADS = 4
GLA_DK = 128
GLA_DV = 256
GLA_TAU = 16.0
HGRN_HEADS = 8
HGRN_DK = 128
HGRN_DV = 128
PEER_HEADS = 8
PEER_NKEYS = 128
PEER_TOPK = 16
LANES = 128
VMEM_LIMIT = 48 * 1024 * 1024


def _cparams(sem):
    return pltpu.CompilerParams(dimension_semantics=sem, vmem_limit_bytes=VMEM_LIMIT)


def _dot(a, b):
    return jnp.dot(a, b, preferred_element_type=F32)


def _dot_nt(a, b):
    return lax.dot_general(a, b, (((1,), (1,)), ((), ())), preferred_element_type=F32)


def _dot_tn(a, b):
    return lax.dot_general(a, b, (((0,), (0,)), ((), ())), preferred_element_type=F32)


def _split2(x):
    hi = x.astype(BF16)
    lo = (x - hi.astype(F32)).astype(BF16)
    return hi, lo


def _split3(x):
    hi = x.astype(BF16)
    r = x - hi.astype(F32)
    mid = r.astype(BF16)
    lo = (r - mid.astype(F32)).astype(BF16)
    return hi, mid, lo


def _log_sigmoid(x):
    return jnp.minimum(x, 0.0) - jnp.log1p(jnp.exp(-jnp.abs(x)))


def _sigmoid(x):
    return 1.0 / (1.0 + jnp.exp(-x))


def _rms(x, gain):
    ms = jnp.mean(x * x, axis=-1, keepdims=True)
    return x * lax.rsqrt(ms + EPS) * gain


def _rmsnorm_body(x_ref, g_ref, o_ref):
    o_ref[...] = _rms(x_ref[...], g_ref[...]).astype(o_ref.dtype)


def rmsnorm(x, gain, tm=512):
    n, d = x.shape
    tm = min(tm, n)
    return pl.pallas_call(
        _rmsnorm_body,
        out_shape=jax.ShapeDtypeStruct((n, d), BF16),
        grid=(n // tm,),
        in_specs=[pl.BlockSpec((tm, d), lambda i: (i, 0)),
                  pl.BlockSpec((1, d), lambda i: (0, 0))],
        out_specs=pl.BlockSpec((tm, d), lambda i: (i, 0)),
        compiler_params=_cparams(("parallel",)),
        name="rmsnorm",
    )(x, gain.reshape(1, d))


def _mm_call(body, x, w, extras, extra_specs, out_dtypes, tm, tn, name):
    n, k = x.shape
    m = w.shape[1]
    tm = min(tm, n)
    tn = min(tn, m)
    outs = [jax.ShapeDtypeStruct((n, m), dt) for dt in out_dtypes]
    return pl.pallas_call(
        body,
        out_shape=outs,
        grid=(n // tm, m // tn),
        in_specs=[pl.BlockSpec((tm, k), lambda i, j: (i, 0)),
                  pl.BlockSpec((k, tn), lambda i, j: (0, j))] + extra_specs,
        out_specs=[pl.BlockSpec((tm, tn), lambda i, j: (i, j)) for _ in outs],
        compiler_params=_cparams(("parallel", "arbitrary")),
        name=name,
    )(x, w, *extras)


def _mm_plain_body(x_ref, w_ref, o_ref, *, scale):
    acc = _dot(x_ref[...], w_ref[...])
    if scale != 1.0:
        acc = acc * scale
    o_ref[...] = acc.astype(o_ref.dtype)


def mm_plain(x, w, out_dtype, scale=1.0, tm=512, tn=512):
    return _mm_call(functools.partial(_mm_plain_body, scale=scale), x, w, [], [],
                    [out_dtype], tm, tn, "mm_plain")[0]


def _mm_headnorm_body(x_ref, w_ref, bd_ref, g_ref, o_ref, *, scale, hd):
    acc = _dot(x_ref[...], w_ref[...])
    sq_hi, sq_lo = _split2(acc * acc)
    ms = (_dot(sq_hi, bd_ref[...]) + _dot(sq_lo, bd_ref[...])) * (1.0 / hd)
    o_ref[...] = (acc * lax.rsqrt(ms + EPS) * (g_ref[...] * scale)).astype(o_ref.dtype)


def mm_headnorm(x, w, gain, scale, hd, tm=512, tn=512):
    m = w.shape[1]
    tn = min(tn, m)
    blk = np.arange(tn) // hd
    bd = jnp.asarray(blk[:, None] == blk[None, :], BF16)
    g = jnp.tile(gain.astype(F32), tn // hd).reshape(1, tn)
    return _mm_call(functools.partial(_mm_headnorm_body, scale=scale, hd=hd), x, w,
                    [bd, g],
                    [pl.BlockSpec((tn, tn), lambda i, j: (0, 0)),
                     pl.BlockSpec((1, tn), lambda i, j: (0, 0))],
                    [BF16], tm, tn, "mm_headnorm")[0]


def _mm_logsig_body(x_ref, w_ref, b_ref, o_ref, *, mult):
    acc = _dot(x_ref[...], w_ref[...]) + b_ref[...]
    o_ref[...] = (_log_sigmoid(acc) * mult).astype(o_ref.dtype)


def mm_logsig(x, w, bias, mult, tm=512, tn=512):
    m = w.shape[1]
    tn = min(tn, m)
    return _mm_call(functools.partial(_mm_logsig_body, mult=mult), x, w,
                    [bias.astype(F32).reshape(1, m)],
                    [pl.BlockSpec((1, tn), lambda i, j: (0, j))],
                    [F32], tm, tn, "mm_logsig")[0]


def _mm_hgrn_gate_body(x_ref, w_ref, lbl_ref, k_ref, lg_ref, *, layer):
    acc = _dot(x_ref[...], w_ref[...])
    logits = lbl_ref[...]
    e = jnp.exp(logits - jnp.max(logits, axis=0, keepdims=True))
    p = e / jnp.sum(e, axis=0, keepdims=True)
    lb = jnp.zeros_like(p[0:1])
    for u in range(1, layer + 1):
        lb = lb + p[u:u + 1]
    gate = lb + (1.0 - lb) * _sigmoid(acc)
    k_ref[...] = (1.0 - gate).astype(k_ref.dtype)
    lg_ref[...] = jnp.log(gate).astype(lg_ref.dtype)


def mm_hgrn_gate(x, w, lb_logits, layer, tm=512, tn=512):
    m = w.shape[1]
    tn = min(tn, m)
    depth = lb_logits.shape[0]
    return _mm_call(functools.partial(_mm_hgrn_gate_body, layer=layer), x, w,
                    [lb_logits.astype(F32)],
                    [pl.BlockSpec((depth, tn), lambda i, j: (0, j))],
                    [F32, F32], tm, tn, "mm_hgrn_gate")


def _mm_res_norm_body(x_ref, w_ref, h_ref, g_ref, ho_ref, hn_ref):
    hnew = h_ref[...] + _dot(x_ref[...], w_ref[...])
    ho_ref[...] = hnew
    hn_ref[...] = _rms(hnew, g_ref[...]).astype(hn_ref.dtype)


def mm_res_norm(x, w, h, gain, tm=512):
    n, d = h.shape
    return _mm_call(_mm_res_norm_body, x, w, [h, gain.astype(F32).reshape(1, d)],
                    [pl.BlockSpec((min(tm, n), d), lambda i, j: (i, 0)),
                     pl.BlockSpec((1, d), lambda i, j: (0, 0))],
                    [F32, BF16], tm, d, "mm_res_norm")


CUM_BLOCK = 256


def _cumsum_body(x_ref, o_ref, *, t):
    nb = t // CUM_BLOCK
    r = lax.broadcasted_iota(jnp.int32, (CUM_BLOCK, CUM_BLOCK), 0)
    c = lax.broadcasted_iota(jnp.int32, (CUM_BLOCK, CUM_BLOCK), 1)
    tril = jnp.where(r >= c, 1.0, 0.0).astype(BF16)
    carry = jnp.zeros((1, x_ref.shape[-1]), F32)
    for b in range(nb):
        x = x_ref[0, b * CUM_BLOCK:(b + 1) * CUM_BLOCK, :]
        hi, mid, lo = _split3(x)
        cs = _dot(tril, hi) + _dot(tril, mid) + _dot(tril, lo) + carry
        o_ref[0, b * CUM_BLOCK:(b + 1) * CUM_BLOCK, :] = cs
        carry = cs[CUM_BLOCK - 1:CUM_BLOCK, :]


def time_cumsum(x):
    b, t, h = x.shape
    return pl.pallas_call(
        functools.partial(_cumsum_body, t=t),
        out_shape=jax.ShapeDtypeStruct((b, t, h), F32),
        grid=(b,),
        in_specs=[pl.BlockSpec((1, t, h), lambda i: (i, 0, 0))],
        out_specs=pl.BlockSpec((1, t, h), lambda i: (i, 0, 0)),
        compiler_params=_cparams(("parallel",)),
        name="time_cumsum",
    )(x)


def _fox_body(q_ref, k_ref, v_ref, g_ref, c_ref, o_ref, *, tq):
    qi = pl.program_id(2)
    q = q_ref[0]
    lane = lax.broadcasted_iota(jnp.int32, (tq, LANES), 1)
    row = lax.broadcasted_iota(jnp.int32, (tq, tq), 0)
    col = lax.broadcasted_iota(jnp.int32, (tq, tq), 1)
    outs = []
    for j in range(2):
        sel = (lane < FOX_HEAD_DIM) if j == 0 else (lane >= FOX_HEAD_DIM)
        qj = jnp.where(sel, q, jnp.zeros_like(q))

        def kv_step(kb, carry, qj=qj, j=j):
            m, l, acc = carry
            off = pl.multiple_of(kb * tq, tq)
            ks = k_ref[0, pl.ds(off, tq), :]
            vs = v_ref[0, pl.ds(off, tq), :]
            s = _dot_nt(qj, ks) - c_ref[0, 0, pl.ds(j, 1), pl.ds(off, tq)]
            s = jnp.where(row + (qi - kb) * tq >= col, s, -jnp.inf)
            m_new = jnp.maximum(m, jnp.max(s, axis=-1, keepdims=True))
            alpha = jnp.exp(m - m_new)
            p = jnp.exp(s - m_new)
            l = alpha * l + jnp.sum(p, axis=-1, keepdims=True)
            acc = alpha * acc + _dot(p.astype(BF16), vs)
            return m_new, l, acc

        init = (jnp.full((tq, 1), -jnp.inf, F32), jnp.zeros((tq, 1), F32),
                jnp.zeros((tq, LANES), F32))
        m, l, acc = lax.fori_loop(0, qi + 1, kv_step, init)
        outs.append(acc / l)
    o = jnp.where(lane < FOX_HEAD_DIM, outs[0], outs[1])
    o = o * _sigmoid(g_ref[0].astype(F32))
    o_ref[0] = o.astype(o_ref.dtype)


def fox_attention(q, k, v, g, c_rows, tq=256):
    b, t, d = q.shape
    tq = min(tq, t)
    npair = d // LANES
    return pl.pallas_call(
        functools.partial(_fox_body, tq=tq),
        out_shape=jax.ShapeDtypeStruct((b, t, d), BF16),
        grid=(b, npair, t // tq),
        in_specs=[pl.BlockSpec((1, tq, LANES), lambda bi, p, i: (bi, i, p)),
                  pl.BlockSpec((1, t, LANES), lambda bi, p, i: (bi, 0, p)),
                  pl.BlockSpec((1, t, LANES), lambda bi, p, i: (bi, 0, p)),
                  pl.BlockSpec((1, tq, LANES), lambda bi, p, i: (bi, i, p)),
                  pl.BlockSpec((1, 1, 2, t), lambda bi, p, i: (bi, p, 0, 0))],
        out_specs=pl.BlockSpec((1, tq, LANES), lambda bi, p, i: (bi, i, p)),
        compiler_params=_cparams(("parallel", "parallel", "arbitrary")),
        name="fox_attention",
    )(q, k, v, g, c_rows)


_LEVELS = (64, 32, 16, 8, 4, 2)


def _gla_consts():
    c = CHUNK
    t = np.arange(c)
    u = t[None, :]
    blocks = [u <= t[:, None]]
    masks = [np.eye(c, dtype=bool)]
    qb, kb = [], []
    for lv in _LEVELS:
        half = lv // 2
        blk, pos = t // lv, t % lv
        r = blk * lv + half - 1
        upper = pos >= half
        qb.append(upper[:, None] & (u > r[:, None]) & (u <= t[:, None]))
        kb.append((~upper)[:, None] & (u > t[:, None]) & (u <= r[:, None]))
        masks.append((blk[:, None] == blk[None, :]) & upper[:, None] & (~upper)[None, :])
    blocks += qb + kb
    blocks.append(u > t[:, None])
    mall = np.concatenate(blocks, axis=0).astype(np.float32)
    mask = np.stack(masks).astype(np.float32)
    return jnp.asarray(mall, BF16), jnp.asarray(mask, F32)


def _gla_body(q_ref, k_ref, g_ref, v_ref, r_ref, gain_ref, mall_ref, mask_ref,
              o_ref, st_ref, *, nchunk):
    c = CHUNK
    nl = len(_LEVELS)

    @pl.when(pl.program_id(2) == 0)
    def _():
        st_ref[...] = jnp.zeros_like(st_ref)

    def chunk(ci, carry):
        sl = pl.ds(pl.multiple_of(ci * c, c), c)
        q = q_ref[0, sl, :]
        k = k_ref[0, sl, :]
        g = g_ref[0, sl, :]
        v = v_ref[0, sl, :]
        g_hi, g_lo = _split2(g)
        mall = mall_ref[...]
        x = jnp.exp(_dot(mall, g_hi) + _dot(mall, g_lo))
        scores = mask_ref[0] * _dot_nt(q.astype(BF16), k.astype(BF16))
        for lv in range(nl):
            ql = (q * x[(1 + lv) * c:(2 + lv) * c]).astype(BF16)
            kl = (k * x[(1 + nl + lv) * c:(2 + nl + lv) * c]).astype(BF16)
            scores = scores + mask_ref[1 + lv] * _dot_nt(ql, kl)
        st = st_ref[...]
        o = _dot(scores.astype(BF16), v) + _dot_nt((q * x[0:c]).astype(BF16), st.astype(BF16))
        kt = (k * x[(1 + 2 * nl) * c:(2 + 2 * nl) * c]).astype(BF16)
        st_ref[...] = st * x[c - 1:c, :] + _dot_tn(v, kt)
        y = _rms(o, gain_ref[0]) * (r_ref[0, sl, :].astype(F32) * _sigmoid(r_ref[0, sl, :].astype(F32)))
        o_ref[0, sl, :] = y.astype(o_ref.dtype)
        return carry

    lax.fori_loop(0, nchunk, chunk, 0)


def gated_linear_attention(q, k, g, v, r, gain, heads, dk, dv, ts=256):
    b, t, _ = q.shape
    ts = min(ts, t)
    mall, mask = _gla_consts()
    qspec = pl.BlockSpec((1, ts, dk), lambda bi, h, i: (bi, i, h))
    vspec = pl.BlockSpec((1, ts, dv), lambda bi, h, i: (bi, i, h))
    return pl.pallas_call(
        functools.partial(_gla_body, nchunk=ts // CHUNK),
        out_shape=jax.ShapeDtypeStruct((b, t, heads * dv), BF16),
        grid=(b, heads, t // ts),
        in_specs=[qspec, qspec, qspec, vspec, vspec,
                  pl.BlockSpec((1, 1, dv), lambda bi, h, i: (h, 0, 0)),
                  pl.BlockSpec(mall.shape, lambda bi, h, i: (0, 0)),
                  pl.BlockSpec(mask.shape, lambda bi, h, i: (0, 0, 0))],
        out_specs=vspec,
        scratch_shapes=[pltpu.VMEM((dv, dk), F32)],
        compiler_params=_cparams(("parallel", "parallel", "arbitrary")),
        name="gated_linear_attention",
    )(q, k, g, v, r, gain.astype(F32).reshape(heads, 1, dv), mall, mask)


def _peer_cand_layout():
    k = PEER_TOPK
    rows_a = [0] * k
    rows_b = list(range(k))
    for a in range(1, k):
        rows_a += [a] * 8
        rows_b += list(range(8))
    rows_a, rows_b = np.array(rows_a), np.array(rows_b)
    valid = (rows_a + 1) * (rows_b + 1) <= k
    pos = np.where(valid, rows_a * k + rows_b, 1 << 20)
    return np.broadcast_to(pos[:, None], (pos.size, LANES)).astype(np.int32)


def _topk_rows(s, ids, big, k):
    vals, sel = [], []
    for _ in range(k):
        m = jnp.max(s, axis=0, keepdims=True)
        ix = jnp.min(jnp.where(s == m, ids, big), axis=0, keepdims=True)
        vals.append(m)
        sel.append(ix)
        s = jnp.where(ids == ix, -jnp.inf, s)
    return jnp.concatenate(vals, axis=0), jnp.concatenate(sel, axis=0)


def _peer_route_body(q_ref, sk_ref, pos_ref, i_ref, j_ref, g_ref, *, tm):
    k = PEER_TOPK
    nk = PEER_NKEYS
    key_ids = lax.broadcasted_iota(jnp.int32, (nk, tm), 0)
    pos = jnp.concatenate([pos_ref[...]] * (tm // LANES), axis=1)
    for h in range(PEER_HEADS):
        tops = []
        for p in range(2):
            hp = 2 * h + p
            st = _dot_nt(sk_ref[hp], q_ref[:, hp * nk:(hp + 1) * nk])
            tops.append(_topk_rows(st, key_ids, nk, k))
        (s0, i0), (s1, i1) = tops
        cand = [s0[0:1] + s1]
        for a in range(1, k):
            cand.append(s0[a:a + 1] + s1[0:8])
        cand = jnp.concatenate(cand, axis=0)
        cand = jnp.where(pos < k * k, cand, -jnp.inf)
        best, bpos = _topk_rows(cand, pos, 1 << 21, k)
        ra = bpos >> 4
        rb = bpos & (k - 1)
        isel = jnp.zeros((k, tm), jnp.int32)
        jsel = jnp.zeros((k, tm), jnp.int32)
        for a in range(k):
            isel = jnp.where(ra == a, i0[a:a + 1], isel)
            jsel = jnp.where(rb == a, i1[a:a + 1], jsel)
        e = jnp.exp(best - best[0:1])
        gates = e / jnp.sum(e, axis=0, keepdims=True)
        i_ref[h * k:(h + 1) * k, :] = isel
        j_ref[h * k:(h + 1) * k, :] = jsel
        g_ref[h * k:(h + 1) * k, :] = gates


def peer_route(q, sub_keys, tm=256):
    n = q.shape[0]
    tm = min(tm, n)
    pos = jnp.asarray(_peer_cand_layout())
    npair = PEER_HEADS * PEER_TOPK
    out = [jax.ShapeDtypeStruct((npair, n), jnp.int32),
           jax.ShapeDtypeStruct((npair, n), jnp.int32),
           jax.ShapeDtypeStruct((npair, n), F32)]
    ospec = pl.BlockSpec((npair, tm), lambda i: (0, i))
    return pl.pallas_call(
        functools.partial(_peer_route_body, tm=tm),
        out_shape=out,
        grid=(n // tm,),
        in_specs=[pl.BlockSpec((tm, q.shape[1]), lambda i: (i, 0)),
                  pl.BlockSpec(sub_keys.shape, lambda i: (0, 0, 0)),
                  pl.BlockSpec(pos.shape, lambda i: (0, 0))],
        out_specs=[ospec, ospec, ospec],
        compiler_params=_cparams(("parallel",)),
        name="peer_route",
    )(q, sub_keys, pos)


def _peer_weights_body(i_ref, j_ref, g_ref, o_ref, w3_ref, *, tm):
    nk = PEER_NKEYS
    sub = lax.broadcasted_iota(jnp.int32, (nk, LANES), 0)

    def tok(n, carry):
        irow = i_ref[pl.ds(n, 1), :]
        jrow = j_ref[pl.ds(n, 1), :]
        grow = g_ref[pl.ds(n, 1), :]
        at = jnp.where(sub == irow, grow, 0.0).astype(BF16)
        bt = jnp.where(sub == jrow, 1.0, 0.0).astype(BF16)
        w3_ref[pl.ds(pl.multiple_of(n * nk, nk), nk), :] = _dot_nt(at, bt)
        return carry

    lax.fori_loop(0, tm, tok, 0)
    for i in range(nk):
        o_ref[:, i * nk:(i + 1) * nk] = w3_ref[pl.ds(i, tm, stride=nk), :].astype(o_ref.dtype)


def peer_weights(isel, jsel, gates, tm=128):
    n, npair = isel.shape
    tm = min(tm, n)
    ne = PEER_NKEYS * PEER_NKEYS
    ispec = pl.BlockSpec((tm, npair), lambda i: (i, 0))
    return pl.pallas_call(
        functools.partial(_peer_weights_body, tm=tm),
        out_shape=jax.ShapeDtypeStruct((n, ne), BF16),
        grid=(n // tm,),
        in_specs=[ispec, ispec, ispec],
        out_specs=pl.BlockSpec((tm, ne), lambda i: (i, 0)),
        scratch_shapes=[pltpu.VMEM((tm * PEER_NKEYS, PEER_NKEYS), F32)],
        compiler_params=_cparams(("parallel",)),
        name="peer_weights",
    )(isel, jsel, gates)


def _gelu(x):
    return 0.5 * x * (1.0 + lax.erf(x * (1.0 / math.sqrt(2.0))))


def _peer_dense_body(x_ref, u_ref, v_ref, w_ref, h_ref, gn_ref, ho_ref, hn_ref, acc_ref, *, nj):
    j = pl.program_id(1)

    @pl.when(j == 0)
    def _():
        acc_ref[...] = jnp.zeros_like(acc_ref)

    hid = _dot_nt(x_ref[...], u_ref[...])
    a = (_gelu(hid) * w_ref[...].astype(F32)).astype(BF16)
    acc_ref[...] += _dot(a, v_ref[...])

    @pl.when(j == nj - 1)
    def _():
        hnew = h_ref[...] + acc_ref[...]
        ho_ref[...] = hnew
        hn_ref[...] = _rms(hnew, gn_ref[...]).astype(hn_ref.dtype)


def peer_dense(x, u, v, w, h, gain_next, tm=1024, te=512):
    n, d = x.shape
    ne = u.shape[0]
    tm = min(tm, n)
    te = min(te, ne)
    nj = ne // te
    rowspec = pl.BlockSpec((tm, d), lambda i, j: (i, 0))
    return pl.pallas_call(
        functools.partial(_peer_dense_body, nj=nj),
        out_shape=[jax.ShapeDtypeStruct((n, d), F32), jax.ShapeDtypeStruct((n, d), BF16)],
        grid=(n // tm, nj),
        in_specs=[rowspec,
                  pl.BlockSpec((te, d), lambda i, j: (j, 0)),
                  pl.BlockSpec((te, d), lambda i, j: (j, 0)),
                  pl.BlockSpec((tm, te), lambda i, j: (i, j)),
                  rowspec,
                  pl.BlockSpec((1, d), lambda i, j: (0, 0))],
        out_specs=[rowspec, rowspec],
        scratch_shapes=[pltpu.VMEM((tm, d), F32)],
        compiler_params=_cparams(("parallel", "arbitrary")),
        name="peer_dense",
    )(x, u, v, w, h, gain_next.astype(F32).reshape(1, d))


def _fox_layer(hn, b, t, w_in, b_f, q_gain, k_gain):
    d = hn.shape[1]
    w = w_in.astype(BF16)
    q = mm_headnorm(hn, w[:, 0:d], q_gain, FOX_HEAD_DIM ** -0.5, FOX_HEAD_DIM)
    k = mm_headnorm(hn, w[:, d:2 * d], k_gain, 1.0, FOX_HEAD_DIM)
    v = mm_plain(hn, w[:, 2 * d:3 * d], BF16)
    g = mm_plain(hn, w[:, 3 * d:4 * d], BF16)
    lf = mm_logsig(hn, w[:, 4 * d:], b_f, 1.0)
    c = time_cumsum(lf.reshape(b, t, FOX_HEADS))
    c_rows = c.transpose(0, 2, 1).reshape(b, FOX_HEADS // 2, 2, t)
    shp = (b, t, d)
    o = fox_attention(q.reshape(shp), k.reshape(shp), v.reshape(shp), g.reshape(shp), c_rows)
    return o.reshape(b * t, d)


def _gla_layer(hn, b, t, w_in, w_up, b_alpha, out_gain):
    d = hn.shape[1]
    gk = GLA_HEADS * GLA_DK
    w = w_in.astype(BF16)
    q = mm_plain(hn, w[:, 0:gk], F32, scale=GLA_DK ** -0.5)
    k = mm_plain(hn, w[:, gk:2 * gk], F32)
    v = mm_plain(hn, w[:, 2 * gk:2 * gk + d], BF16)
    r = mm_plain(hn, w[:, 2 * gk + d:2 * gk + 2 * d], BF16)
    low = mm_plain(hn, w[:, 2 * gk + 2 * d:], BF16)
    log_a = mm_logsig(low, w_up.astype(BF16), b_alpha, 1.0 / GLA_TAU)
    o = gated_linear_attention(q.reshape(b, t, gk), k.reshape(b, t, gk), log_a.reshape(b, t, gk),
                               v.reshape(b, t, d), r.reshape(b, t, d), out_gain,
                               GLA_HEADS, GLA_DK, GLA_DV)
    return o.reshape(b * t, d)


def _hgrn_layer(hn, b, t, layer, w_in, lb_logits, out_gain):
    d = hn.shape[1]
    w = w_in.astype(BF16)
    q = mm_plain(hn, w[:, 0:d], F32, scale=HGRN_DK ** -0.5)
    k, log_g = mm_hgrn_gate(hn, w[:, d:2 * d], lb_logits, layer)
    v = mm_plain(hn, w[:, 2 * d:3 * d], BF16)
    r = mm_plain(hn, w[:, 3 * d:4 * d], BF16)
    shp = (b, t, d)
    o = gated_linear_attention(q.reshape(shp), k.reshape(shp), log_g.reshape(shp),
                               v.reshape(shp), r.reshape(shp), out_gain,
                               HGRN_HEADS, HGRN_DK, HGRN_DV)
    return o.reshape(b * t, d)


def _peer_layer(h, hn, w_q, sub_keys, u, v, gain_next):
    q = mm_plain(hn, w_q.astype(BF16), BF16)
    sk = sub_keys.astype(BF16).reshape(2 * PEER_HEADS, PEER_NKEYS, -1)
    isel, jsel, gates = peer_route(q, sk)
    w = peer_weights(isel.T, jsel.T, gates.T)
    return peer_dense(hn, u.astype(BF16), v.astype(BF16), w, h, gain_next)


def kernel(x, norm_mix, norm_ffn, fox_w_in, fox_b_f, fox_q_gain, fox_k_gain, fox_w_out,
           gla_w_in, gla_w_up, gla_b_alpha, gla_out_gain, gla_w_out,
           hgrn_w_in, hgrn_lb_logits, hgrn_out_gain, hgrn_w_out,
           peer_w_q, peer_sub_keys, peer_u, peer_v):
    b, t, d = x.shape
    depth = norm_mix.shape[0]
    h = x.reshape(b * t, d)
    hn = rmsnorm(h, norm_mix[0])
    for i in range(depth):
        m, j = i % N_MIXERS, i // N_MIXERS
        if m == 0:
            o = _fox_layer(hn, b, t, fox_w_in[j], fox_b_f[j], fox_q_gain[j], fox_k_gain[j])
            w_out = fox_w_out[j]
        elif m == 1:
            o = _gla_layer(hn, b, t, gla_w_in[j], gla_w_up[j], gla_b_alpha[j], gla_out_gain[j])
            w_out = gla_w_out[j]
        else:
            o = _hgrn_layer(hn, b, t, i, hgrn_w_in[j], hgrn_lb_logits, hgrn_out_gain[j])
            w_out = hgrn_w_out[j]
        h, hn = mm_res_norm(o, w_out.astype(BF16), h, norm_ffn[i])
        gain_next = norm_mix[(i + 1) % depth]
        h, hn = _peer_layer(h, hn, peer_w_q[i], peer_sub_keys[i], peer_u[i], peer_v[i], gain_next)
    return h.reshape(b, t, d)
```

```python
import functools
import math

import numpy as np
import jax
import jax.numpy as jnp
from jax import lax
from jax.experimental import pallas as pl
from jax.experimental.pallas import tpu as pltpu

F32 = jnp.float32
BF16 = jnp.bfloat16
EPS = 1e-6

DEPTH = 4
N_MIXERS = 3
CHUNK = 64
FOX_HEADS = 16
FOX_HEAD_DIM = 64
GLA_HEADS = 4
GLA_DK = 128
GLA_DV = 256
GLA_TAU = 16.0
HGRN_HEADS = 8
HGRN_DK = 128
HGRN_DV = 128
PEER_HEADS = 8
PEER_NKEYS = 128
PEER_TOPK = 16
LANES = 128
VMEM_LIMIT = 48 * 1024 * 1024


def _cparams(sem):
    return pltpu.CompilerParams(dimension_semantics=sem, vmem_limit_bytes=VMEM_LIMIT)


def _dot(a, b):
    return jnp.dot(a, b, preferred_element_type=F32)


def _dot_nt(a, b):
    return lax.dot_general(a, b, (((1,), (1,)), ((), ())), preferred_element_type=F32)


def _dot_tn(a, b):
    return lax.dot_general(a, b, (((0,), (0,)), ((), ())), preferred_element_type=F32)


def _split2(x):
    hi = x.astype(BF16)
    lo = (x - hi.astype(F32)).astype(BF16)
    return hi, lo


def _split3(x):
    hi = x.astype(BF16)
    r = x - hi.astype(F32)
    mid = r.astype(BF16)
    lo = (r - mid.astype(F32)).astype(BF16)
    return hi, mid, lo


def _log_sigmoid(x):
    return jnp.minimum(x, 0.0) - jnp.log1p(jnp.exp(-jnp.abs(x)))


def _sigmoid(x):
    return 1.0 / (1.0 + jnp.exp(-x))


def _rms(x, gain):
    ms = jnp.mean(x * x, axis=-1, keepdims=True)
    return x * lax.rsqrt(ms + EPS) * gain


def _rmsnorm_body(x_ref, g_ref, o_ref):
    o_ref[...] = _rms(x_ref[...], g_ref[...]).astype(o_ref.dtype)


def rmsnorm(x, gain, tm=512):
    n, d = x.shape
    tm = min(tm, n)
    return pl.pallas_call(
        _rmsnorm_body,
        out_shape=jax.ShapeDtypeStruct((n, d), BF16),
        grid=(n // tm,),
        in_specs=[pl.BlockSpec((tm, d), lambda i: (i, 0)),
                  pl.BlockSpec((1, d), lambda i: (0, 0))],
        out_specs=pl.BlockSpec((tm, d), lambda i: (i, 0)),
        compiler_params=_cparams(("parallel",)),
        name="rmsnorm",
    )(x, gain.reshape(1, d))


def _mm_call(body, x, w, extras, extra_specs, out_dtypes, tm, tn, name):
    n, k = x.shape
    m = w.shape[1]
    tm = min(tm, n)
    tn = min(tn, m)
    outs = [jax.ShapeDtypeStruct((n, m), dt) for dt in out_dtypes]
    return pl.pallas_call(
        body,
        out_shape=outs,
        grid=(n // tm, m // tn),
        in_specs=[pl.BlockSpec((tm, k), lambda i, j: (i, 0)),
                  pl.BlockSpec((k, tn), lambda i, j: (0, j))] + extra_specs,
        out_specs=[pl.BlockSpec((tm, tn), lambda i, j: (i, j)) for _ in outs],
        compiler_params=_cparams(("parallel", "arbitrary")),
        name=name,
    )(x, w, *extras)


def _mm_plain_body(x_ref, w_ref, o_ref, *, scale):
    acc = _dot(x_ref[...], w_ref[...])
    if scale != 1.0:
        acc = acc * scale
    o_ref[...] = acc.astype(o_ref.dtype)


def mm_plain(x, w, out_dtype, scale=1.0, tm=512, tn=512):
    return _mm_call(functools.partial(_mm_plain_body, scale=scale), x, w, [], [],
                    [out_dtype], tm, tn, "mm_plain")[0]


def _mm_headnorm_body(x_ref, w_ref, bd_ref, g_ref, o_ref, *, scale, hd):
    acc = _dot(x_ref[...], w_ref[...])
    sq_hi, sq_lo = _split2(acc * acc)
    ms = (_dot(sq_hi, bd_ref[...]) + _dot(sq_lo, bd_ref[...])) * (1.0 / hd)
    o_ref[...] = (acc * lax.rsqrt(ms + EPS) * (g_ref[...] * scale)).astype(o_ref.dtype)


def mm_headnorm(x, w, gain, scale, hd, tm=512, tn=512):
    m = w.shape[1]
    tn = min(tn, m)
    blk = np.arange(tn) // hd
    bd = jnp.asarray(blk[:, None] == blk[None, :], BF16)
    g = jnp.tile(gain.astype(F32), tn // hd).reshape(1, tn)
    return _mm_call(functools.partial(_mm_headnorm_body, scale=scale, hd=hd), x, w,
                    [bd, g],
                    [pl.BlockSpec((tn, tn), lambda i, j: (0, 0)),
                     pl.BlockSpec((1, tn), lambda i, j: (0, 0))],
                    [BF16], tm, tn, "mm_headnorm")[0]


def _mm_logsig_body(x_ref, w_ref, b_ref, o_ref, *, mult):
    acc = _dot(x_ref[...], w_ref[...]) + b_ref[...]
    o_ref[...] = (_log_sigmoid(acc) * mult).astype(o_ref.dtype)


def mm_logsig(x, w, bias, mult, tm=512, tn=512):
    m = w.shape[1]
    tn = min(tn, m)
    return _mm_call(functools.partial(_mm_logsig_body, mult=mult), x, w,
                    [bias.astype(F32).reshape(1, m)],
                    [pl.BlockSpec((1, tn), lambda i, j: (0, j))],
                    [F32], tm, tn, "mm_logsig")[0]


def _mm_hgrn_gate_body(x_ref, w_ref, lbl_ref, k_ref, lg_ref, *, layer):
    acc = _dot(x_ref[...], w_ref[...])
    logits = lbl_ref[...]
    e = jnp.exp(logits - jnp.max(logits, axis=0, keepdims=True))
    p = e / jnp.sum(e, axis=0, keepdims=True)
    lb = jnp.zeros_like(p[0:1])
    for u in range(1, layer + 1):
        lb = lb + p[u:u + 1]
    gate = lb + (1.0 - lb) * _sigmoid(acc)
    k_ref[...] = (1.0 - gate).astype(k_ref.dtype)
    lg_ref[...] = jnp.log(gate).astype(lg_ref.dtype)


def mm_hgrn_gate(x, w, lb_logits, layer, tm=512, tn=512):
    m = w.shape[1]
    tn = min(tn, m)
    depth = lb_logits.shape[0]
    return _mm_call(functools.partial(_mm_hgrn_gate_body, layer=layer), x, w,
                    [lb_logits.astype(F32)],
                    [pl.BlockSpec((depth, tn), lambda i, j: (0, j))],
                    [F32, F32], tm, tn, "mm_hgrn_gate")


def _mm_res_norm_body(x_ref, w_ref, h_ref, g_ref, ho_ref, hn_ref):
    hnew = h_ref[...] + _dot(x_ref[...], w_ref[...])
    ho_ref[...] = hnew
    hn_ref[...] = _rms(hnew, g_ref[...]).astype(hn_ref.dtype)


def mm_res_norm(x, w, h, gain, tm=512):
    n, d = h.shape
    return _mm_call(_mm_res_norm_body, x, w, [h, gain.astype(F32).reshape(1, d)],
                    [pl.BlockSpec((min(tm, n), d), lambda i, j: (i, 0)),
                     pl.BlockSpec((1, d), lambda i, j: (0, 0))],
                    [F32, BF16], tm, d, "mm_res_norm")


CUM_BLOCK = 256


def _cumsum_body(x_ref, o_ref, *, t):
    nb = t // CUM_BLOCK
    r = lax.broadcasted_iota(jnp.int32, (CUM_BLOCK, CUM_BLOCK), 0)
    c = lax.broadcasted_iota(jnp.int32, (CUM_BLOCK, CUM_BLOCK), 1)
    tril = jnp.where(r >= c, 1.0, 0.0).astype(BF16)
    carry = jnp.zeros((1, x_ref.shape[-1]), F32)
    for b in range(nb):
        x = x_ref[0, b * CUM_BLOCK:(b + 1) * CUM_BLOCK, :]
        hi, mid, lo = _split3(x)
        cs = _dot(tril, hi) + _dot(tril, mid) + _dot(tril, lo) + carry
        o_ref[0, b * CUM_BLOCK:(b + 1) * CUM_BLOCK, :] = cs
        carry = cs[CUM_BLOCK - 1:CUM_BLOCK, :]


def time_cumsum(x):
    b, t, h = x.shape
    return pl.pallas_call(
        functools.partial(_cumsum_body, t=t),
        out_shape=jax.ShapeDtypeStruct((b, t, h), F32),
        grid=(b,),
        in_specs=[pl.BlockSpec((1, t, h), lambda i: (i, 0, 0))],
        out_specs=pl.BlockSpec((1, t, h), lambda i: (i, 0, 0)),
        compiler_params=_cparams(("parallel",)),
        name="time_cumsum",
    )(x)


def _fox_body(q_ref, k_ref, v_ref, g_ref, c_ref, o_ref, s_ref, m_ref, l_ref, acc_ref, *, tq):
    qi = pl.program_id(2)
    q = q_ref[0]
    lane = lax.broadcasted_iota(jnp.int32, (tq, LANES), 1)
    row = lax.broadcasted_iota(jnp.int32, (tq, tq), 0)
    col = lax.broadcasted_iota(jnp.int32, (tq, tq), 1)
    zero = jnp.zeros_like(q)
    qs = [jnp.where(lane < FOX_HEAD_DIM, q, zero), jnp.where(lane >= FOX_HEAD_DIM, q, zero)]
    nfold = tq // LANES

    def fold(x, op):
        r = x[:, 0:LANES]
        for f in range(1, nfold):
            r = op(r, x[:, f * LANES:(f + 1) * LANES])
        return r

    def scores(j, off):
        ks = k_ref[0, pl.ds(off, tq), :]
        return _dot_nt(qs[j], ks) - c_ref[0, 0, pl.ds(j, 1), pl.ds(off, tq)]

    m_ref[...] = jnp.full(m_ref.shape, -jnp.inf, F32)
    l_ref[...] = jnp.zeros(l_ref.shape, F32)
    acc_ref[...] = jnp.zeros(acc_ref.shape, F32)

    def pass1(kb, carry):
        off = pl.multiple_of(kb * tq, tq)
        for j in range(2):
            s = scores(j, off)
            s_ref[j, :, pl.ds(off, tq)] = s
            m_ref[j] = jnp.maximum(m_ref[j], fold(s, jnp.maximum))
        return carry

    lax.fori_loop(0, qi, pass1, 0)
    offd = pl.multiple_of(qi * tq, tq)
    sd, ms = [], []
    for j in range(2):
        s = jnp.where(row >= col, scores(j, offd), -jnp.inf)
        sd.append(s)
        ms.append(jnp.max(jnp.maximum(m_ref[j], fold(s, jnp.maximum)), axis=-1, keepdims=True))

    def pass2(kb, carry):
        off = pl.multiple_of(kb * tq, tq)
        vs = v_ref[0, pl.ds(off, tq), :]
        for j in range(2):
            p = jnp.exp(s_ref[j, :, pl.ds(off, tq)] - ms[j])
            l_ref[j] += fold(p, jnp.add)
            acc_ref[j] += _dot(p.astype(BF16), vs)
        return carry

    lax.fori_loop(0, qi, pass2, 0)
    vsd = v_ref[0, pl.ds(offd, tq), :]
    outs = []
    for j in range(2):
        p = jnp.exp(sd[j] - ms[j])
        l = jnp.sum(l_ref[j] + fold(p, jnp.add), axis=-1, keepdims=True)
        outs.append((acc_ref[j] + _dot(p.astype(BF16), vsd)) / l)
    o = jnp.where(lane < FOX_HEAD_DIM, outs[0], outs[1])
    o = o * _sigmoid(g_ref[0].astype(F32))
    o_ref[0] = o.astype(o_ref.dtype)


def fox_attention(q, k, v, g, c_rows, tq=256):
    b, t, d = q.shape
    tq = min(tq, t)
    npair = d // LANES
    return pl.pallas_call(
        functools.partial(_fox_body, tq=tq),
        out_shape=jax.ShapeDtypeStruct((b, t, d), BF16),
        grid=(b, npair, t // tq),
        in_specs=[pl.BlockSpec((1, tq, LANES), lambda bi, p, i: (bi, i, p)),
                  pl.BlockSpec((1, t, LANES), lambda bi, p, i: (bi, 0, p)),
                  pl.BlockSpec((1, t, LANES), lambda bi, p, i: (bi, 0, p)),
                  pl.BlockSpec((1, tq, LANES), lambda bi, p, i: (bi, i, p)),
                  pl.BlockSpec((1, 1, 2, t), lambda bi, p, i: (bi, p, 0, 0))],
        out_specs=pl.BlockSpec((1, tq, LANES), lambda bi, p, i: (bi, i, p)),
        scratch_shapes=[pltpu.VMEM((2, tq, t), F32), pltpu.VMEM((2, tq, LANES), F32),
                        pltpu.VMEM((2, tq, LANES), F32), pltpu.VMEM((2, tq, LANES), F32)],
        compiler_params=_cparams(("parallel", "parallel", "arbitrary")),
        name="fox_attention",
    )(q, k, v, g, c_rows)


_LEVELS = (64, 32, 16, 8, 4, 2)


def _gla_consts():
    c = CHUNK
    t = np.arange(c)
    u = t[None, :]
    blocks = [u <= t[:, None]]
    masks = [np.eye(c, dtype=bool)]
    qb, kb = [], []
    for lv in _LEVELS:
        half = lv // 2
        blk, pos = t // lv, t % lv
        r = blk * lv + half - 1
        upper = pos >= half
        qb.append(upper[:, None] & (u > r[:, None]) & (u <= t[:, None]))
        kb.append((~upper)[:, None] & (u > t[:, None]) & (u <= r[:, None]))
        masks.append((blk[:, None] == blk[None, :]) & upper[:, None] & (~upper)[None, :])
    blocks += qb + kb
    blocks.append(u > t[:, None])
    mall = np.concatenate(blocks, axis=0).astype(np.float32)
    mask = np.stack(masks).astype(np.float32)
    return jnp.asarray(mall, BF16), jnp.asarray(mask, F32)


def _gla_body(q_ref, k_ref, g_ref, v_ref, r_ref, gain_ref, mall_ref, mask_ref,
              o_ref, st_ref, *, nchunk, hg, dk, dv):
    c = CHUNK
    nl = len(_LEVELS)

    @pl.when(pl.program_id(2) == 0)
    def _():
        st_ref[...] = jnp.zeros_like(st_ref)

    def chunk(ci, carry):
        sl = pl.ds(pl.multiple_of(ci * c, c), c)
        g_hi, g_lo = _split2(g_ref[0, sl, :])
        mall = mall_ref[...]
        xall = jnp.exp(_dot(mall, g_hi) + _dot(mall, g_lo))
        for hh in range(hg):
            ks, vs = slice(hh * dk, (hh + 1) * dk), slice(hh * dv, (hh + 1) * dv)
            q = q_ref[0, sl, ks]
            k = k_ref[0, sl, ks]
            v = v_ref[0, sl, vs]
            x = xall[:, ks]
            scores = mask_ref[0] * _dot_nt(q.astype(BF16), k.astype(BF16))
            for lv in range(nl):
                ql = (q * x[(1 + lv) * c:(2 + lv) * c]).astype(BF16)
                kl = (k * x[(1 + nl + lv) * c:(2 + nl + lv) * c]).astype(BF16)
                scores = scores + mask_ref[1 + lv] * _dot_nt(ql, kl)
            st = st_ref[hh]
            o = _dot(scores.astype(BF16), v) + _dot_nt((q * x[0:c]).astype(BF16), st.astype(BF16))
            kt = (k * x[(1 + 2 * nl) * c:(2 + 2 * nl) * c]).astype(BF16)
            st_ref[hh] = st * x[c - 1:c, :] + _dot_tn(v, kt)
            r = r_ref[0, sl, vs].astype(F32)
            y = _rms(o, gain_ref[hh]) * (r * _sigmoid(r))
            o_ref[0, sl, vs] = y.astype(o_ref.dtype)
        return carry

    lax.fori_loop(0, nchunk, chunk, 0)


def gated_linear_attention(q, k, g, v, r, gain, heads, dk, dv, hg=2, ts=256):
    b, t, _ = q.shape
    ts = min(ts, t)
    mall, mask = _gla_consts()
    qspec = pl.BlockSpec((1, ts, hg * dk), lambda bi, h, i: (bi, i, h))
    vspec = pl.BlockSpec((1, ts, hg * dv), lambda bi, h, i: (bi, i, h))
    return pl.pallas_call(
        functools.partial(_gla_body, nchunk=ts // CHUNK, hg=hg, dk=dk, dv=dv),
        out_shape=jax.ShapeDtypeStruct((b, t, heads * dv), BF16),
        grid=(b, heads // hg, t // ts),
        in_specs=[qspec, qspec, qspec, vspec, vspec,
                  pl.BlockSpec((hg, 1, dv), lambda bi, h, i: (h, 0, 0)),
                  pl.BlockSpec(mall.shape, lambda bi, h, i: (0, 0)),
                  pl.BlockSpec(mask.shape, lambda bi, h, i: (0, 0, 0))],
        out_specs=vspec,
        scratch_shapes=[pltpu.VMEM((hg, dv, dk), F32)],
        compiler_params=_cparams(("parallel", "parallel", "arbitrary")),
        name="gated_linear_attention",
    )(q, k, g, v, r, gain.astype(F32).reshape(heads, 1, dv), mall, mask)


def _peer_cand_layout():
    k = PEER_TOPK
    rows_a = [0] * k
    rows_b = list(range(k))
    for a in range(1, k):
        rows_a += [a] * 8
        rows_b += list(range(8))
    rows_a, rows_b = np.array(rows_a), np.array(rows_b)
    valid = (rows_a + 1) * (rows_b + 1) <= k
    pos = np.where(valid, rows_a * k + rows_b, 1 << 20)
    return np.broadcast_to(pos[:, None], (pos.size, LANES)).astype(np.int32)


def _topk_rows(s, ids, big, k):
    vals, sel = [], []
    for _ in range(k):
        m = jnp.max(s, axis=0, keepdims=True)
        ix = jnp.min(jnp.where(s == m, ids, big), axis=0, keepdims=True)
        vals.append(m)
        sel.append(ix)
        s = jnp.where(ids == ix, -jnp.inf, s)
    return jnp.concatenate(vals, axis=0), jnp.concatenate(sel, axis=0)


def _peer_route_body(q_ref, sk_ref, pos_ref, i_ref, j_ref, g_ref, *, tm):
    k = PEER_TOPK
    nk = PEER_NKEYS
    key_ids = lax.broadcasted_iota(jnp.int32, (nk, tm), 0)
    pos = jnp.concatenate([pos_ref[...]] * (tm // LANES), axis=1)
    for h in range(PEER_HEADS):
        tops = []
        for p in range(2):
            hp = 2 * h + p
            st = _dot_nt(sk_ref[hp], q_ref[:, hp * nk:(hp + 1) * nk])
            tops.append(_topk_rows(st, key_ids, nk, k))
        (s0, i0), (s1, i1) = tops
        cand = [s0[0:1] + s1]
        for a in range(1, k):
            cand.append(s0[a:a + 1] + s1[0:8])
        cand = jnp.concatenate(cand, axis=0)
        cand = jnp.where(pos < k * k, cand, -jnp.inf)
        best, bpos = _topk_rows(cand, pos, 1 << 21, k)
        ra = bpos >> 4
        rb = bpos & (k - 1)
        isel = jnp.zeros((k, tm), jnp.int32)
        jsel = jnp.zeros((k, tm), jnp.int32)
        for a in range(k):
            isel = jnp.where(ra == a, i0[a:a + 1], isel)
            jsel = jnp.where(rb == a, i1[a:a + 1], jsel)
        e = jnp.exp(best - best[0:1])
        gates = e / jnp.sum(e, axis=0, keepdims=True)
        i_ref[h * k:(h + 1) * k, :] = isel
        j_ref[h * k:(h + 1) * k, :] = jsel
        g_ref[h * k:(h + 1) * k, :] = gates


def peer_route(q, sub_keys, tm=256):
    n = q.shape[0]
    tm = min(tm, n)
    pos = jnp.asarray(_peer_cand_layout())
    npair = PEER_HEADS * PEER_TOPK
    out = [jax.ShapeDtypeStruct((npair, n), jnp.int32),
           jax.ShapeDtypeStruct((npair, n), jnp.int32),
           jax.ShapeDtypeStruct((npair, n), F32)]
    ospec = pl.BlockSpec((npair, tm), lambda i: (0, i))
    return pl.pallas_call(
        functools.partial(_peer_route_body, tm=tm),
        out_shape=out,
        grid=(n // tm,),
        in_specs=[pl.BlockSpec((tm, q.shape[1]), lambda i: (i, 0)),
                  pl.BlockSpec(sub_keys.shape, lambda i: (0, 0, 0)),
                  pl.BlockSpec(pos.shape, lambda i: (0, 0))],
        out_specs=[ospec, ospec, ospec],
        compiler_params=_cparams(("parallel",)),
        name="peer_route",
    )(q, sub_keys, pos)


W3_GROUP = 8
W3_PITCH = PEER_NKEYS + 8


def _peer_weights_body(i_ref, j_ref, g_ref, o_ref, w3_ref, *, tm):
    nk = PEER_NKEYS
    sub = lax.broadcasted_iota(jnp.int32, (nk, LANES), 0)

    def tok_group(gi, carry):
        base = pl.multiple_of(gi * W3_GROUP, W3_GROUP)
        it = i_ref[pl.ds(base, W3_GROUP), :]
        jt = j_ref[pl.ds(base, W3_GROUP), :]
        gt = g_ref[pl.ds(base, W3_GROUP), :]
        for s in range(W3_GROUP):
            at = jnp.where(sub == it[s:s + 1], gt[s:s + 1], 0.0).astype(BF16)
            bt = jnp.where(sub == jt[s:s + 1], 1.0, 0.0).astype(BF16)
            row0 = pl.multiple_of((base + s) * W3_PITCH, 8)
            w3_ref[pl.ds(row0, nk), :] = _dot_nt(at, bt)
        return carry

    lax.fori_loop(0, tm // W3_GROUP, tok_group, 0)
    for i in range(nk):
        o_ref[:, i * nk:(i + 1) * nk] = w3_ref[pl.ds(i, tm, stride=W3_PITCH), :].astype(o_ref.dtype)


def peer_weights(isel, jsel, gates, tm=128):
    n, npair = isel.shape
    tm = min(tm, n)
    ne = PEER_NKEYS * PEER_NKEYS
    ispec = pl.BlockSpec((tm, npair), lambda i: (i, 0))
    return pl.pallas_call(
        functools.partial(_peer_weights_body, tm=tm),
        out_shape=jax.ShapeDtypeStruct((n, ne), BF16),
        grid=(n // tm,),
        in_specs=[ispec, ispec, ispec],
        out_specs=pl.BlockSpec((tm, ne), lambda i: (i, 0)),
        scratch_shapes=[pltpu.VMEM((tm * W3_PITCH, PEER_NKEYS), F32)],
        compiler_params=_cparams(("parallel",)),
        name="peer_weights",
    )(isel, jsel, gates)


def _gelu(x):
    return 0.5 * x * (1.0 + lax.erf(x * (1.0 / math.sqrt(2.0))))


def _peer_dense_body(x_ref, u_ref, v_ref, w_ref, h_ref, gn_ref, ho_ref, hn_ref, acc_ref, *, nj):
    j = pl.program_id(1)

    @pl.when(j == 0)
    def _():
        acc_ref[...] = jnp.zeros_like(acc_ref)

    hid = _dot_nt(x_ref[...], u_ref[...])
    a = (_gelu(hid) * w_ref[...].astype(F32)).astype(BF16)
    acc_ref[...] += _dot(a, v_ref[...])

    @pl.when(j == nj - 1)
    def _():
        hnew = h_ref[...] + acc_ref[...]
        ho_ref[...] = hnew
        hn_ref[...] = _rms(hnew, gn_ref[...]).astype(hn_ref.dtype)


def peer_dense(x, u, v, w, h, gain_next, tm=1024, te=512):
    n, d = x.shape
    ne = u.shape[0]
    tm = min(tm, n)
    te = min(te, ne)
    nj = ne // te
    rowspec = pl.BlockSpec((tm, d), lambda i, j: (i, 0))
    return pl.pallas_call(
        functools.partial(_peer_dense_body, nj=nj),
        out_shape=[jax.ShapeDtypeStruct((n, d), F32), jax.ShapeDtypeStruct((n, d), BF16)],
        grid=(n // tm, nj),
        in_specs=[rowspec,
                  pl.BlockSpec((te, d), lambda i, j: (j, 0)),
                  pl.BlockSpec((te, d), lambda i, j: (j, 0)),
                  pl.BlockSpec((tm, te), lambda i, j: (i, j)),
                  rowspec,
                  pl.BlockSpec((1, d), lambda i, j: (0, 0))],
        out_specs=[rowspec, rowspec],
        scratch_shapes=[pltpu.VMEM((tm, d), F32)],
        compiler_params=_cparams(("parallel", "arbitrary")),
        name="peer_dense",
    )(x, u, v, w, h, gain_next.astype(F32).reshape(1, d))


def _fox_layer(hn, b, t, w_in, b_f, q_gain, k_gain):
    d = hn.shape[1]
    w = w_in.astype(BF16)
    q = mm_headnorm(hn, w[:, 0:d], q_gain, FOX_HEAD_DIM ** -0.5, FOX_HEAD_DIM)
    k = mm_headnorm(hn, w[:, d:2 * d], k_gain, 1.0, FOX_HEAD_DIM)
    v = mm_plain(hn, w[:, 2 * d:3 * d], BF16)
    g = mm_plain(hn, w[:, 3 * d:4 * d], BF16)
    lf = mm_logsig(hn, w[:, 4 * d:], b_f, 1.0)
    c = time_cumsum(lf.reshape(b, t, FOX_HEADS))
    c_rows = c.transpose(0, 2, 1).reshape(b, FOX_HEADS // 2, 2, t)
    shp = (b, t, d)
    o = fox_attention(q.reshape(shp), k.reshape(shp), v.reshape(shp), g.reshape(shp), c_rows)
    return o.reshape(b * t, d)


def _gla_layer(hn, b, t, w_in, w_up, b_alpha, out_gain):
    d = hn.shape[1]
    gk = GLA_HEADS * GLA_DK
    w = w_in.astype(BF16)
    q = mm_plain(hn, w[:, 0:gk], F32, scale=GLA_DK ** -0.5)
    k = mm_plain(hn, w[:, gk:2 * gk], F32)
    v = mm_plain(hn, w[:, 2 * gk:2 * gk + d], BF16)
    r = mm_plain(hn, w[:, 2 * gk + d:2 * gk + 2 * d], BF16)
    low = mm_plain(hn, w[:, 2 * gk + 2 * d:], BF16)
    log_a = mm_logsig(low, w_up.astype(BF16), b_alpha, 1.0 / GLA_TAU)
    o = gated_linear_attention(q.reshape(b, t, gk), k.reshape(b, t, gk), log_a.reshape(b, t, gk),
                               v.reshape(b, t, d), r.reshape(b, t, d), out_gain,
                               GLA_HEADS, GLA_DK, GLA_DV)
    return o.reshape(b * t, d)


def _hgrn_layer(hn, b, t, layer, w_in, lb_logits, out_gain):
    d = hn.shape[1]
    w = w_in.astype(BF16)
    q = mm_plain(hn, w[:, 0:d], F32, scale=HGRN_DK ** -0.5)
    k, log_g = mm_hgrn_gate(hn, w[:, d:2 * d], lb_logits, layer)
    v = mm_plain(hn, w[:, 2 * d:3 * d], BF16)
    r = mm_plain(hn, w[:, 3 * d:4 * d], BF16)
    shp = (b, t, d)
    o = gated_linear_attention(q.reshape(shp), k.reshape(shp), log_g.reshape(shp),
                               v.reshape(shp), r.reshape(shp), out_gain,
                               HGRN_HEADS, HGRN_DK, HGRN_DV)
    return o.reshape(b * t, d)


def _peer_layer(h, hn, w_q, sub_keys, u, v, gain_next):
    q = mm_plain(hn, w_q.astype(BF16), BF16)
    sk = sub_keys.astype(BF16).reshape(2 * PEER_HEADS, PEER_NKEYS, -1)
    isel, jsel, gates = peer_route(q, sk)
    w = peer_weights(isel.T, jsel.T, gates.T)
    return peer_dense(hn, u.astype(BF16), v.astype(BF16), w, h, gain_next)


def kernel(x, norm_mix, norm_ffn, fox_w_in, fox_b_f, fox_q_gain, fox_k_gain, fox_w_out,
           gla_w_in, gla_w_up, gla_b_alpha, gla_out_gain, gla_w_out,
           hgrn_w_in, hgrn_lb_logits, hgrn_out_gain, hgrn_w_out,
           peer_w_q, peer_sub_keys, peer_u, peer_v):
    b, t, d = x.shape
    depth = norm_mix.shape[0]
    h = x.reshape(b * t, d)
    hn = rmsnorm(h, norm_mix[0])
    for i in range(depth):
        m, j = i % N_MIXERS, i // N_MIXERS
        if m == 0:
            o = _fox_layer(hn, b, t, fox_w_in[j], fox_b_f[j], fox_q_gain[j], fox_k_gain[j])
            w_out = fox_w_out[j]
        elif m == 1:
            o = _gla_layer(hn, b, t, gla_w_in[j], gla_w_up[j], gla_b_alpha[j], gla_out_gain[j])
            w_out = gla_w_out[j]
        else:
            o = _hgrn_layer(hn, b, t, i, hgrn_w_in[j], hgrn_lb_logits, hgrn_out_gain[j])
            w_out = hgrn_w_out[j]
        h, hn = mm_res_norm(o, w_out.astype(BF16), h, norm_ffn[i])
        gain_next = norm_mix[(i + 1) % depth]
        h, hn = _peer_layer(h, hn, peer_w_q[i], peer_sub_keys[i], peer_u[i], peer_v[i], gain_next)
    return h.reshape(b, t, d)
```

```python
import functools
import math

import numpy as np
import jax
import jax.numpy as jnp
from jax import lax
from jax.experimental import pallas as pl
from jax.experimental.pallas import tpu as pltpu

F32 = jnp.float32
BF16 = jnp.bfloat16
EPS = 1e-6

DEPTH = 4
N_MIXERS = 3
CHUNK = 64
FOX_HEADS = 16
FOX_HEAD_DIM = 64
GLA_HEADS = 4
GLA_DK = 128
GLA_DV = 256
GLA_TAU = 16.0
HGRN_HEADS = 8
HGRN_DK = 128
HGRN_DV = 128
PEER_HEADS = 8
PEER_NKEYS = 128
PEER_TOPK = 16
LANES = 128
VMEM_LIMIT = 48 * 1024 * 1024


def _cparams(sem):
    return pltpu.CompilerParams(dimension_semantics=sem, vmem_limit_bytes=VMEM_LIMIT)


def _dot(a, b):
    return jnp.dot(a, b, preferred_element_type=F32)


def _dot_nt(a, b):
    return lax.dot_general(a, b, (((1,), (1,)), ((), ())), preferred_element_type=F32)


def _dot_tn(a, b):
    return lax.dot_general(a, b, (((0,), (0,)), ((), ())), preferred_element_type=F32)


def _split2(x):
    hi = x.astype(BF16)
    lo = (x - hi.astype(F32)).astype(BF16)
    return hi, lo


def _split3(x):
    hi = x.astype(BF16)
    r = x - hi.astype(F32)
    mid = r.astype(BF16)
    lo = (r - mid.astype(F32)).astype(BF16)
    return hi, mid, lo


def _log_sigmoid(x):
    return jnp.minimum(x, 0.0) - jnp.log1p(jnp.exp(-jnp.abs(x)))


def _sigmoid(x):
    return 1.0 / (1.0 + jnp.exp(-x))


def _rms(x, gain):
    ms = jnp.mean(x * x, axis=-1, keepdims=True)
    return x * lax.rsqrt(ms + EPS) * gain


def _rmsnorm_body(x_ref, g_ref, o_ref):
    o_ref[...] = _rms(x_ref[...], g_ref[...]).astype(o_ref.dtype)


def rmsnorm(x, gain, tm=512):
    n, d = x.shape
    tm = min(tm, n)
    return pl.pallas_call(
        _rmsnorm_body,
        out_shape=jax.ShapeDtypeStruct((n, d), BF16),
        grid=(n // tm,),
        in_specs=[pl.BlockSpec((tm, d), lambda i: (i, 0)),
                  pl.BlockSpec((1, d), lambda i: (0, 0))],
        out_specs=pl.BlockSpec((tm, d), lambda i: (i, 0)),
        compiler_params=_cparams(("parallel",)),
        name="rmsnorm",
    )(x, gain.reshape(1, d))


def _mm_call(body, x, w, extras, extra_specs, out_dtypes, tm, tn, name):
    n, k = x.shape
    m = w.shape[1]
    tm = min(tm, n)
    tn = min(tn, m)
    outs = [jax.ShapeDtypeStruct((n, m), dt) for dt in out_dtypes]
    return pl.pallas_call(
        body,
        out_shape=outs,
        grid=(n // tm, m // tn),
        in_specs=[pl.BlockSpec((tm, k), lambda i, j: (i, 0)),
                  pl.BlockSpec((k, tn), lambda i, j: (0, j))] + extra_specs,
        out_specs=[pl.BlockSpec((tm, tn), lambda i, j: (i, j)) for _ in outs],
        compiler_params=_cparams(("parallel", "arbitrary")),
        name=name,
    )(x, w, *extras)


def _mm_plain_body(x_ref, w_ref, o_ref, *, scale):
    acc = _dot(x_ref[...], w_ref[...])
    if scale != 1.0:
        acc = acc * scale
    o_ref[...] = acc.astype(o_ref.dtype)


def mm_plain(x, w, out_dtype, scale=1.0, tm=1024, tn=1024):
    return _mm_call(functools.partial(_mm_plain_body, scale=scale), x, w, [], [],
                    [out_dtype], tm, tn, "mm_plain")[0]


MXU_DIM = 256


def _fox_proj_body(x_ref, w_ref, bd_ref, g_ref, o_ref, *, hd):
    j = pl.program_id(1)
    acc = _dot(x_ref[...], w_ref[...])

    @pl.when(j < 2)
    def _():
        for c in range(acc.shape[1] // MXU_DIM):
            cols = slice(c * MXU_DIM, (c + 1) * MXU_DIM)
            a = acc[:, cols]
            ms = _dot((a * a).astype(BF16), bd_ref[...]) * (1.0 / hd)
            o_ref[:, cols] = (a * lax.rsqrt(ms + EPS) * g_ref[0, :, cols]).astype(o_ref.dtype)

    @pl.when(j >= 2)
    def _():
        o_ref[...] = acc.astype(o_ref.dtype)


def fox_proj(x, w, q_gain, k_gain, scale, hd, tm=512):
    n, d = x.shape
    tm = min(tm, n)
    blk = np.arange(MXU_DIM) // hd
    bd = jnp.asarray(blk[:, None] == blk[None, :], BF16)
    g = jnp.stack([jnp.tile(q_gain.astype(F32) * scale, d // hd),
                   jnp.tile(k_gain.astype(F32), d // hd)]).reshape(2, 1, d)
    return pl.pallas_call(
        functools.partial(_fox_proj_body, hd=hd),
        out_shape=jax.ShapeDtypeStruct((n, 4 * d), BF16),
        grid=(n // tm, 4),
        in_specs=[pl.BlockSpec((tm, d), lambda i, j: (i, 0)),
                  pl.BlockSpec((d, d), lambda i, j: (0, j)),
                  pl.BlockSpec((MXU_DIM, MXU_DIM), lambda i, j: (0, 0)),
                  pl.BlockSpec((1, 1, d), lambda i, j: (jnp.minimum(j, 1), 0, 0))],
        out_specs=pl.BlockSpec((tm, d), lambda i, j: (i, j)),
        compiler_params=_cparams(("parallel", "arbitrary")),
        name="fox_proj",
    )(x, w, bd, g)


def _mm_logsig_body(x_ref, w_ref, b_ref, o_ref, *, mult):
    acc = _dot(x_ref[...], w_ref[...]) + b_ref[...]
    o_ref[...] = (_log_sigmoid(acc) * mult).astype(o_ref.dtype)


def mm_logsig(x, w, bias, mult, tm=512, tn=512):
    m = w.shape[1]
    tn = min(tn, m)
    return _mm_call(functools.partial(_mm_logsig_body, mult=mult), x, w,
                    [bias.astype(F32).reshape(1, m)],
                    [pl.BlockSpec((1, tn), lambda i, j: (0, j))],
                    [F32], tm, tn, "mm_logsig")[0]


def _mm_hgrn_gate_body(x_ref, w_ref, lbl_ref, k_ref, lg_ref, *, layer):
    acc = _dot(x_ref[...], w_ref[...])
    logits = lbl_ref[...]
    e = jnp.exp(logits - jnp.max(logits, axis=0, keepdims=True))
    p = e / jnp.sum(e, axis=0, keepdims=True)
    lb = jnp.zeros_like(p[0:1])
    for u in range(1, layer + 1):
        lb = lb + p[u:u + 1]
    gate = lb + (1.0 - lb) * _sigmoid(acc)
    k_ref[...] = (1.0 - gate).astype(k_ref.dtype)
    lg_ref[...] = jnp.log(gate).astype(lg_ref.dtype)


def mm_hgrn_gate(x, w, lb_logits, layer, tm=512, tn=512):
    m = w.shape[1]
    tn = min(tn, m)
    depth = lb_logits.shape[0]
    return _mm_call(functools.partial(_mm_hgrn_gate_body, layer=layer), x, w,
                    [lb_logits.astype(F32)],
                    [pl.BlockSpec((depth, tn), lambda i, j: (0, j))],
                    [F32, F32], tm, tn, "mm_hgrn_gate")


def _mm_res_norm_body(x_ref, w_ref, h_ref, g_ref, ho_ref, hn_ref):
    hnew = h_ref[...] + _dot(x_ref[...], w_ref[...])
    ho_ref[...] = hnew
    hn_ref[...] = _rms(hnew, g_ref[...]).astype(hn_ref.dtype)


def mm_res_norm(x, w, h, gain, tm=512):
    n, d = h.shape
    return _mm_call(_mm_res_norm_body, x, w, [h, gain.astype(F32).reshape(1, d)],
                    [pl.BlockSpec((min(tm, n), d), lambda i, j: (i, 0)),
                     pl.BlockSpec((1, d), lambda i, j: (0, 0))],
                    [F32, BF16], tm, d, "mm_res_norm")


CUM_BLOCK = 256


def _cumsum_body(x_ref, o_ref, *, t):
    nb = t // CUM_BLOCK
    r = lax.broadcasted_iota(jnp.int32, (CUM_BLOCK, CUM_BLOCK), 0)
    c = lax.broadcasted_iota(jnp.int32, (CUM_BLOCK, CUM_BLOCK), 1)
    tril = jnp.where(r >= c, 1.0, 0.0).astype(BF16)
    carry = jnp.zeros((1, x_ref.shape[-1]), F32)
    for b in range(nb):
        x = x_ref[0, b * CUM_BLOCK:(b + 1) * CUM_BLOCK, :]
        hi, mid, lo = _split3(x)
        cs = _dot(tril, hi) + _dot(tril, mid) + _dot(tril, lo) + carry
        o_ref[0, b * CUM_BLOCK:(b + 1) * CUM_BLOCK, :] = cs
        carry = cs[CUM_BLOCK - 1:CUM_BLOCK, :]


def time_cumsum(x):
    b, t, h = x.shape
    return pl.pallas_call(
        functools.partial(_cumsum_body, t=t),
        out_shape=jax.ShapeDtypeStruct((b, t, h), F32),
        grid=(b,),
        in_specs=[pl.BlockSpec((1, t, h), lambda i: (i, 0, 0))],
        out_specs=pl.BlockSpec((1, t, h), lambda i: (i, 0, 0)),
        compiler_params=_cparams(("parallel",)),
        name="time_cumsum",
    )(x)


def _fox_body(q_ref, k_ref, v_ref, g_ref, c_ref, o_ref, s_ref, m_ref, l_ref, acc_ref, *, tq):
    qi = pl.program_id(2)
    q = q_ref[0]
    lane = lax.broadcasted_iota(jnp.int32, (tq, LANES), 1)
    row = lax.broadcasted_iota(jnp.int32, (tq, tq), 0)
    col = lax.broadcasted_iota(jnp.int32, (tq, tq), 1)
    zero = jnp.zeros_like(q)
    qs = [jnp.where(lane < FOX_HEAD_DIM, q, zero), jnp.where(lane >= FOX_HEAD_DIM, q, zero)]
    nfold = tq // LANES

    def fold(x, op):
        r = x[:, 0:LANES]
        for f in range(1, nfold):
            r = op(r, x[:, f * LANES:(f + 1) * LANES])
        return r

    def scores(j, off):
        ks = k_ref[0, pl.ds(off, tq), :]
        return _dot_nt(qs[j], ks) - c_ref[0, 0, pl.ds(j, 1), pl.ds(off, tq)]

    m_ref[...] = jnp.full(m_ref.shape, -jnp.inf, F32)
    l_ref[...] = jnp.zeros(l_ref.shape, F32)
    acc_ref[...] = jnp.zeros(acc_ref.shape, F32)

    def pass1(kb, carry):
        off = pl.multiple_of(kb * tq, tq)
        for j in range(2):
            s = scores(j, off)
            s_ref[j, :, pl.ds(off, tq)] = s
            m_ref[j] = jnp.maximum(m_ref[j], fold(s, jnp.maximum))
        return carry

    lax.fori_loop(0, qi, pass1, 0)
    offd = pl.multiple_of(qi * tq, tq)
    sd, ms = [], []
    for j in range(2):
        s = jnp.where(row >= col, scores(j, offd), -jnp.inf)
        sd.append(s)
        ms.append(jnp.max(jnp.maximum(m_ref[j], fold(s, jnp.maximum)), axis=-1, keepdims=True))

    def pass2(kb, carry):
        off = pl.multiple_of(kb * tq, tq)
        vs = v_ref[0, pl.ds(off, tq), :]
        for j in range(2):
            p = jnp.exp(s_ref[j, :, pl.ds(off, tq)] - ms[j])
            l_ref[j] += fold(p, jnp.add)
            acc_ref[j] += _dot(p.astype(BF16), vs)
        return carry

    lax.fori_loop(0, qi, pass2, 0)
    vsd = v_ref[0, pl.ds(offd, tq), :]
    outs = []
    for j in range(2):
        p = jnp.exp(sd[j] - ms[j])
        l = jnp.sum(l_ref[j] + fold(p, jnp.add), axis=-1, keepdims=True)
        outs.append((acc_ref[j] + _dot(p.astype(BF16), vsd)) / l)
    o = jnp.where(lane < FOX_HEAD_DIM, outs[0], outs[1])
    o = o * _sigmoid(g_ref[0].astype(F32))
    o_ref[0] = o.astype(o_ref.dtype)


def fox_attention(qkvg, c_rows, tq=256):
    b, t, d4 = qkvg.shape
    d = d4 // 4
    tq = min(tq, t)
    npair = d // LANES
    qspec = pl.BlockSpec((1, tq, LANES), lambda bi, p, i: (bi, i, p))
    return pl.pallas_call(
        functools.partial(_fox_body, tq=tq),
        out_shape=jax.ShapeDtypeStruct((b, t, d), BF16),
        grid=(b, npair, t // tq),
        in_specs=[qspec,
                  pl.BlockSpec((1, t, LANES), lambda bi, p, i: (bi, 0, npair + p)),
                  pl.BlockSpec((1, t, LANES), lambda bi, p, i: (bi, 0, 2 * npair + p)),
                  pl.BlockSpec((1, tq, LANES), lambda bi, p, i: (bi, i, 3 * npair + p)),
                  pl.BlockSpec((1, 1, 2, t), lambda bi, p, i: (bi, p, 0, 0))],
        out_specs=qspec,
        scratch_shapes=[pltpu.VMEM((2, tq, t), F32), pltpu.VMEM((2, tq, LANES), F32),
                        pltpu.VMEM((2, tq, LANES), F32), pltpu.VMEM((2, tq, LANES), F32)],
        compiler_params=_cparams(("parallel", "parallel", "arbitrary")),
        name="fox_attention",
    )(qkvg, qkvg, qkvg, qkvg, c_rows)


_LEVELS = (64, 32, 16, 8, 4, 2)


def _gla_consts():
    c = CHUNK
    t = np.arange(c)
    u = t[None, :]
    blocks = [u <= t[:, None]]
    masks = [np.eye(c, dtype=bool)]
    for lv in _LEVELS:
        half = lv // 2
        blk, pos = t // lv, t % lv
        r = blk * lv + half - 1
        upper = pos >= half
        qrow = upper[:, None] & (u > r[:, None]) & (u <= t[:, None])
        krow = (~upper)[:, None] & (u > t[:, None]) & (u <= r[:, None])
        blocks.append(qrow | krow)
        masks.append((blk[:, None] == blk[None, :]) & upper[:, None] & (~upper)[None, :])
    blocks.append(u > t[:, None])
    mall = np.concatenate(blocks, axis=0).astype(np.float32)
    mask = np.stack(masks).astype(np.float32)
    return jnp.asarray(mall, BF16), jnp.asarray(mask, F32)


def _gla_body(q_ref, k_ref, g_ref, v_ref, r_ref, gain_ref, mall_ref, mask_ref,
              o_ref, st_ref, *, nchunk, hg, dk, dv):
    c = CHUNK
    nl = len(_LEVELS)

    @pl.when(pl.program_id(2) == 0)
    def _():
        st_ref[...] = jnp.zeros_like(st_ref)

    def chunk(ci, carry):
        sl = pl.ds(pl.multiple_of(ci * c, c), c)
        g_hi, g_lo = _split2(g_ref[0, sl, :])
        mall = mall_ref[...]
        xall = jnp.exp(_dot(mall, g_hi) + _dot(mall, g_lo))
        for hh in range(hg):
            ks, vs = slice(hh * dk, (hh + 1) * dk), slice(hh * dv, (hh + 1) * dv)
            q = q_ref[0, sl, ks]
            k = k_ref[0, sl, ks]
            v = v_ref[0, sl, vs]
            x = xall[:, ks]
            scores = mask_ref[0] * _dot_nt(q.astype(BF16), k.astype(BF16))
            for lv in range(nl):
                z = x[(1 + lv) * c:(2 + lv) * c]
                scores = scores + mask_ref[1 + lv] * _dot_nt((q * z).astype(BF16), (k * z).astype(BF16))
            st = st_ref[hh]
            o = _dot(scores.astype(BF16), v) + _dot_nt((q * x[0:c]).astype(BF16), st.astype(BF16))
            kt = (k * x[(1 + nl) * c:(2 + nl) * c]).astype(BF16)
            st_ref[hh] = st * x[c - 1:c, :] + _dot_tn(v, kt)
            r = r_ref[0, sl, vs].astype(F32)
            y = _rms(o, gain_ref[hh]) * (r * _sigmoid(r))
            o_ref[0, sl, vs] = y.astype(o_ref.dtype)
        return carry

    lax.fori_loop(0, nchunk, chunk, 0)


def gated_linear_attention(q, k, g, v, r, gain, heads, dk, dv, hg=2, ts=256):
    b, t, _ = q.shape
    ts = min(ts, t)
    mall, mask = _gla_consts()
    qspec = pl.BlockSpec((1, ts, hg * dk), lambda bi, h, i: (bi, i, h))
    vspec = pl.BlockSpec((1, ts, hg * dv), lambda bi, h, i: (bi, i, h))
    return pl.pallas_call(
        functools.partial(_gla_body, nchunk=ts // CHUNK, hg=hg, dk=dk, dv=dv),
        out_shape=jax.ShapeDtypeStruct((b, t, heads * dv), BF16),
        grid=(b, heads // hg, t // ts),
        in_specs=[qspec, qspec, qspec, vspec, vspec,
                  pl.BlockSpec((hg, 1, dv), lambda bi, h, i: (h, 0, 0)),
                  pl.BlockSpec(mall.shape, lambda bi, h, i: (0, 0)),
                  pl.BlockSpec(mask.shape, lambda bi, h, i: (0, 0, 0))],
        out_specs=vspec,
        scratch_shapes=[pltpu.VMEM((hg, dv, dk), F32)],
        compiler_params=_cparams(("parallel", "parallel", "arbitrary")),
        name="gated_linear_attention",
    )(q, k, g, v, r, gain.astype(F32).reshape(heads, 1, dv), mall, mask)


def _peer_cand_layout():
    k = PEER_TOPK
    ab = [(0, b) for b in range(16)] + [(1, b) for b in range(8)]
    ab += [(2, b) if b < 5 else None for b in range(8)]
    ab += [(3, 0), (3, 1), (3, 2), (3, 3), (4, 0), (4, 1), (4, 2), None]
    ab += [(5, 0), (5, 1), (6, 0), (6, 1), (7, 0), (7, 1), None, None]
    ab += [(a, 0) for a in range(8, 16)]
    assert all(p is None or (p[0] + 1) * (p[1] + 1) <= k for p in ab)
    assert sum(p is not None for p in ab) == sum(k // (a + 1) for a in range(k))
    pos = np.array([PEER_POS_INVALID if p is None else p[0] * k + p[1] for p in ab], np.float32)
    return np.broadcast_to(pos[:, None], (pos.size, LANES)).copy()


PEER_POS_INVALID = 1024.0


def _extract_max(s, ids, big):
    m = jnp.max(s, axis=0, keepdims=True)
    ix = jnp.min(jnp.where(s == m, ids, big), axis=0, keepdims=True)
    return m, ix


def _topk_rows(s, ids, big, k):
    vals, sel = [], []
    for _ in range(k):
        m, ix = _extract_max(s, ids, big)
        vals.append(m)
        sel.append(ix)
        s = jnp.where(ids == ix, -jnp.inf, s)
    return jnp.concatenate(vals, axis=0), jnp.concatenate(sel, axis=0)


def _topk_keys(st, k):
    n = st.shape[0]
    a = jnp.concatenate([st[r:r + 8] for r in range(0, n, 16)], axis=0)
    b = jnp.concatenate([st[r + 8:r + 16] for r in range(0, n, 16)], axis=0)
    r = lax.broadcasted_iota(jnp.int32, a.shape, 0)
    ida = (((r >> 3) << 4) + (r & 7)).astype(F32)
    idb = ida + 8.0
    a_wins = a >= b
    win, idw = jnp.maximum(a, b), jnp.where(a_wins, ida, idb)
    los, idl = jnp.minimum(a, b), jnp.where(a_wins, idb, ida)
    vals, sel = [], []
    for _ in range(k):
        m, ix = _extract_max(win, idw, float(n))
        vals.append(m)
        sel.append(ix)
        hit = idw == ix
        win = jnp.where(hit, los, win)
        idw = jnp.where(hit, idl, idw)
        los = jnp.where(hit, -jnp.inf, los)
    return jnp.concatenate(vals, axis=0), jnp.concatenate(sel, axis=0)


def _peer_route_body(q_ref, sk_ref, pos_ref, i_ref, j_ref, g_ref, *, tm):
    k = PEER_TOPK
    nk = PEER_NKEYS
    pos = jnp.concatenate([pos_ref[...]] * (tm // LANES), axis=1)
    sub = lax.broadcasted_iota(jnp.int32, (8, tm), 0)

    def bc(x, r):
        return jnp.broadcast_to(x[r:r + 1], (8, tm))

    for h in range(PEER_HEADS):
        tops = []
        for p in range(2):
            hp = 2 * h + p
            st = _dot_nt(sk_ref[hp], q_ref[:, hp * nk:(hp + 1) * nk])
            tops.append(_topk_keys(st, k))
        (s0, i0), (s1, i1) = tops
        lo1 = s1[0:8]
        cand = jnp.concatenate([
            bc(s0, 0) + lo1,
            bc(s0, 0) + s1[8:16],
            bc(s0, 1) + lo1,
            bc(s0, 2) + lo1,
            jnp.where(sub < 4, bc(s0, 3), bc(s0, 4)) + jnp.where(sub < 4, lo1, pltpu.roll(lo1, 4, 0)),
            jnp.where(sub < 2, bc(s0, 5), jnp.where(sub < 4, bc(s0, 6), bc(s0, 7)))
            + jnp.where((sub & 1) == 0, bc(s1, 0), bc(s1, 1)),
            s0[8:16] + bc(s1, 0)], axis=0)
        cand = jnp.where(pos < PEER_POS_INVALID, cand, -jnp.inf)
        best, bpos = _topk_rows(cand, pos, 2.0 * PEER_POS_INVALID, k)
        bpos = bpos.astype(jnp.int32)
        ra = bpos >> 4
        rb = bpos & (k - 1)
        isel = jnp.zeros((k, tm), F32)
        jsel = jnp.zeros((k, tm), F32)
        for a in range(k):
            isel = jnp.where(ra == a, i0[a:a + 1], isel)
            jsel = jnp.where(rb == a, i1[a:a + 1], jsel)
        e = jnp.exp(best - best[0:1])
        gates = e / jnp.sum(e, axis=0, keepdims=True)
        i_ref[h * k:(h + 1) * k, :] = isel.astype(jnp.int32)
        j_ref[h * k:(h + 1) * k, :] = jsel.astype(jnp.int32)
        g_ref[h * k:(h + 1) * k, :] = gates


def peer_route(q, sub_keys, tm=256):
    n = q.shape[0]
    tm = min(tm, n)
    pos = jnp.asarray(_peer_cand_layout())
    npair = PEER_HEADS * PEER_TOPK
    out = [jax.ShapeDtypeStruct((npair, n), jnp.int32),
           jax.ShapeDtypeStruct((npair, n), jnp.int32),
           jax.ShapeDtypeStruct((npair, n), F32)]
    ospec = pl.BlockSpec((npair, tm), lambda i: (0, i))
    return pl.pallas_call(
        functools.partial(_peer_route_body, tm=tm),
        out_shape=out,
        grid=(n // tm,),
        in_specs=[pl.BlockSpec((tm, q.shape[1]), lambda i: (i, 0)),
                  pl.BlockSpec(sub_keys.shape, lambda i: (0, 0, 0)),
                  pl.BlockSpec(pos.shape, lambda i: (0, 0))],
        out_specs=[ospec, ospec, ospec],
        compiler_params=_cparams(("parallel",)),
        name="peer_route",
    )(q, sub_keys, pos)


W3_GROUP = 8
W3_BATCH = 2 * W3_GROUP


def _peer_weights_body(i_ref, j_ref, g_ref, o_ref, wa_ref, wb_ref, *, tm):
    nk = PEER_NKEYS
    nb = tm // W3_BATCH
    sub = lax.broadcasted_iota(jnp.int32, (nk, LANES), 0)

    def build(t, w_ref):
        for half in range(2):
            base = pl.multiple_of(t * W3_BATCH + half * W3_GROUP, W3_GROUP)
            it = i_ref[pl.ds(base, W3_GROUP), :]
            jt = j_ref[pl.ds(base, W3_GROUP), :]
            gt = g_ref[pl.ds(base, W3_GROUP), :]
            for s in range(W3_GROUP):
                at = jnp.where(sub == it[s:s + 1], gt[s:s + 1], 0.0).astype(BF16)
                bt = jnp.where(sub == jt[s:s + 1], 1.0, 0.0).astype(BF16)
                w_ref[half, pl.ds(s, nk, stride=W3_GROUP), :] = _dot_nt(at, bt)

    def flush(t, w_ref):
        r0 = pl.multiple_of(t * W3_BATCH, W3_BATCH)
        for i in range(nk):
            rows = jnp.concatenate([w_ref[0, i * W3_GROUP:(i + 1) * W3_GROUP, :],
                                    w_ref[1, i * W3_GROUP:(i + 1) * W3_GROUP, :]], axis=0)
            o_ref[pl.ds(r0, W3_BATCH), i * nk:(i + 1) * nk] = rows.astype(o_ref.dtype)

    build(0, wa_ref)

    def step(u, carry):
        build(2 * u + 1, wb_ref)
        flush(2 * u, wa_ref)
        build(2 * u + 2, wa_ref)
        flush(2 * u + 1, wb_ref)
        return carry

    lax.fori_loop(0, nb // 2 - 1, step, 0)
    build(nb - 1, wb_ref)
    flush(nb - 2, wa_ref)
    flush(nb - 1, wb_ref)


def peer_weights(isel, jsel, gates, tm=128):
    n, npair = isel.shape
    tm = min(tm, n)
    ne = PEER_NKEYS * PEER_NKEYS
    ispec = pl.BlockSpec((tm, npair), lambda i: (i, 0))
    return pl.pallas_call(
        functools.partial(_peer_weights_body, tm=tm),
        out_shape=jax.ShapeDtypeStruct((n, ne), BF16),
        grid=(n // tm,),
        in_specs=[ispec, ispec, ispec],
        out_specs=pl.BlockSpec((tm, ne), lambda i: (i, 0)),
        scratch_shapes=[pltpu.VMEM((2, W3_GROUP * PEER_NKEYS, PEER_NKEYS), F32),
                        pltpu.VMEM((2, W3_GROUP * PEER_NKEYS, PEER_NKEYS), F32)],
        compiler_params=_cparams(("parallel",)),
        name="peer_weights",
    )(isel, jsel, gates)


def _gelu(x):
    return 0.5 * x * (1.0 + lax.erf(x * (1.0 / math.sqrt(2.0))))


def _peer_dense_body(x_ref, u_ref, v_ref, w_ref, h_ref, gn_ref, ho_ref, hn_ref, acc_ref, *, nj):
    j = pl.program_id(1)

    @pl.when(j == 0)
    def _():
        acc_ref[...] = jnp.zeros_like(acc_ref)

    hid = _dot_nt(x_ref[...], u_ref[...])
    a = (_gelu(hid) * w_ref[...].astype(F32)).astype(BF16)
    acc_ref[...] += _dot(a, v_ref[...])

    @pl.when(j == nj - 1)
    def _():
        hnew = h_ref[...] + acc_ref[...]
        ho_ref[...] = hnew
        hn_ref[...] = _rms(hnew, gn_ref[...]).astype(hn_ref.dtype)


def peer_dense(x, u, v, w, h, gain_next, tm=1024, te=512):
    n, d = x.shape
    ne = u.shape[0]
    tm = min(tm, n)
    te = min(te, ne)
    nj = ne // te
    rowspec = pl.BlockSpec((tm, d), lambda i, j: (i, 0))
    return pl.pallas_call(
        functools.partial(_peer_dense_body, nj=nj),
        out_shape=[jax.ShapeDtypeStruct((n, d), F32), jax.ShapeDtypeStruct((n, d), BF16)],
        grid=(n // tm, nj),
        in_specs=[rowspec,
                  pl.BlockSpec((te, d), lambda i, j: (j, 0)),
                  pl.BlockSpec((te, d), lambda i, j: (j, 0)),
                  pl.BlockSpec((tm, te), lambda i, j: (i, j)),
                  rowspec,
                  pl.BlockSpec((1, d), lambda i, j: (0, 0))],
        out_specs=[rowspec, rowspec],
        scratch_shapes=[pltpu.VMEM((tm, d), F32)],
        compiler_params=_cparams(("parallel", "arbitrary")),
        name="peer_dense",
    )(x, u, v, w, h, gain_next.astype(F32).reshape(1, d))


def _fox_layer(hn, b, t, w_in, b_f, q_gain, k_gain):
    d = hn.shape[1]
    w = w_in.astype(BF16)
    qkvg = fox_proj(hn, w[:, 0:4 * d], q_gain, k_gain, FOX_HEAD_DIM ** -0.5, FOX_HEAD_DIM)
    lf = mm_logsig(hn, w[:, 4 * d:], b_f, 1.0)
    c = time_cumsum(lf.reshape(b, t, FOX_HEADS))
    c_rows = c.transpose(0, 2, 1).reshape(b, FOX_HEADS // 2, 2, t)
    o = fox_attention(qkvg.reshape(b, t, 4 * d), c_rows)
    return o.reshape(b * t, d)


def _gla_layer(hn, b, t, w_in, w_up, b_alpha, out_gain):
    d = hn.shape[1]
    gk = GLA_HEADS * GLA_DK
    w = w_in.astype(BF16)
    q = mm_plain(hn, w[:, 0:gk], F32, scale=GLA_DK ** -0.5)
    k = mm_plain(hn, w[:, gk:2 * gk], F32)
    v = mm_plain(hn, w[:, 2 * gk:2 * gk + d], BF16)
    r = mm_plain(hn, w[:, 2 * gk + d:2 * gk + 2 * d], BF16)
    low = mm_plain(hn, w[:, 2 * gk + 2 * d:], BF16)
    log_a = mm_logsig(low, w_up.astype(BF16), b_alpha, 1.0 / GLA_TAU)
    o = gated_linear_attention(q.reshape(b, t, gk), k.reshape(b, t, gk), log_a.reshape(b, t, gk),
                               v.reshape(b, t, d), r.reshape(b, t, d), out_gain,
                               GLA_HEADS, GLA_DK, GLA_DV)
    return o.reshape(b * t, d)


def _hgrn_layer(hn, b, t, layer, w_in, lb_logits, out_gain):
    d = hn.shape[1]
    w = w_in.astype(BF16)
    q = mm_plain(hn, w[:, 0:d], F32, scale=HGRN_DK ** -0.5)
    k, log_g = mm_hgrn_gate(hn, w[:, d:2 * d], lb_logits, layer)
    v = mm_plain(hn, w[:, 2 * d:3 * d], BF16)
    r = mm_plain(hn, w[:, 3 * d:4 * d], BF16)
    shp = (b, t, d)
    o = gated_linear_attention(q.reshape(shp), k.reshape(shp), log_g.reshape(shp),
                               v.reshape(shp), r.reshape(shp), out_gain,
                               HGRN_HEADS, HGRN_DK, HGRN_DV)
    return o.reshape(b * t, d)


def _peer_layer(h, hn, w_q, sub_keys, u, v, gain_next):
    q = mm_plain(hn, w_q.astype(BF16), BF16)
    sk = sub_keys.astype(BF16).reshape(2 * PEER_HEADS, PEER_NKEYS, -1)
    isel, jsel, gates = peer_route(q, sk)
    w = peer_weights(isel.T, jsel.T, gates.T)
    return peer_dense(hn, u.astype(BF16), v.astype(BF16), w, h, gain_next)


def kernel(x, norm_mix, norm_ffn, fox_w_in, fox_b_f, fox_q_gain, fox_k_gain, fox_w_out,
           gla_w_in, gla_w_up, gla_b_alpha, gla_out_gain, gla_w_out,
           hgrn_w_in, hgrn_lb_logits, hgrn_out_gain, hgrn_w_out,
           peer_w_q, peer_sub_keys, peer_u, peer_v):
    b, t, d = x.shape
    depth = norm_mix.shape[0]
    h = x.reshape(b * t, d)
    hn = rmsnorm(h, norm_mix[0])
    for i in range(depth):
        m, j = i % N_MIXERS, i // N_MIXERS
        if m == 0:
            o = _fox_layer(hn, b, t, fox_w_in[j], fox_b_f[j], fox_q_gain[j], fox_k_gain[j])
            w_out = fox_w_out[j]
        elif m == 1:
            o = _gla_layer(hn, b, t, gla_w_in[j], gla_w_up[j], gla_b_alpha[j], gla_out_gain[j])
            w_out = gla_w_out[j]
        else:
            o = _hgrn_layer(hn, b, t, i, hgrn_w_in[j], hgrn_lb_logits, hgrn_out_gain[j])
            w_out = hgrn_w_out[j]
        h, hn = mm_res_norm(o, w_out.astype(BF16), h, norm_ffn[i])
        gain_next = norm_mix[(i + 1) % depth]
        h, hn = _peer_layer(h, hn, peer_w_q[i], peer_sub_keys[i], peer_u[i], peer_v[i], gain_next)
    return h.reshape(b, t, d)
```

```python
import functools
import math

import numpy as np
import jax
import jax.numpy as jnp
from jax import lax
from jax.experimental import pallas as pl
from jax.experimental.pallas import tpu as pltpu

F32 = jnp.float32
BF16 = jnp.bfloat16
EPS = 1e-6

DEPTH = 4
N_MIXERS = 3
CHUNK = 64
FOX_HEADS = 16
FOX_HEAD_DIM = 64
GLA_HEADS = 4
GLA_DK = 128
GLA_DV = 256
GLA_TAU = 16.0
HGRN_HEADS = 8
HGRN_DK = 128
HGRN_DV = 128
PEER_HEADS = 8
PEER_NKEYS = 128
PEER_TOPK = 16
LANES = 128
VMEM_LIMIT = 56 * 1024 * 1024


def _cparams(sem):
    return pltpu.CompilerParams(dimension_semantics=sem, vmem_limit_bytes=VMEM_LIMIT)


def _dot(a, b):
    return jnp.dot(a, b, preferred_element_type=F32)


def _dot_nt(a, b):
    return lax.dot_general(a, b, (((1,), (1,)), ((), ())), preferred_element_type=F32)


def _dot_tn(a, b):
    return lax.dot_general(a, b, (((0,), (0,)), ((), ())), preferred_element_type=F32)


def _split2(x):
    hi = x.astype(BF16)
    lo = (x - hi.astype(F32)).astype(BF16)
    return hi, lo


def _split3(x):
    hi = x.astype(BF16)
    r = x - hi.astype(F32)
    mid = r.astype(BF16)
    lo = (r - mid.astype(F32)).astype(BF16)
    return hi, mid, lo


def _log_sigmoid(x):
    return jnp.minimum(x, 0.0) - jnp.log1p(jnp.exp(-jnp.abs(x)))


def _sigmoid(x):
    return 1.0 / (1.0 + jnp.exp(-x))


def _rms(x, gain):
    ms = jnp.mean(x * x, axis=-1, keepdims=True)
    return x * lax.rsqrt(ms + EPS) * gain


def _rmsnorm_body(x_ref, g_ref, o_ref):
    o_ref[...] = _rms(x_ref[...], g_ref[...]).astype(o_ref.dtype)


def rmsnorm(x, gain, tm=512):
    n, d = x.shape
    tm = min(tm, n)
    return pl.pallas_call(
        _rmsnorm_body,
        out_shape=jax.ShapeDtypeStruct((n, d), BF16),
        grid=(n // tm,),
        in_specs=[pl.BlockSpec((tm, d), lambda i: (i, 0)),
                  pl.BlockSpec((1, d), lambda i: (0, 0))],
        out_specs=pl.BlockSpec((tm, d), lambda i: (i, 0)),
        compiler_params=_cparams(("parallel",)),
        name="rmsnorm",
    )(x, gain.reshape(1, d))


def _mm_call(body, x, w, extras, extra_specs, out_dtypes, tm, tn, name):
    n, k = x.shape
    m = w.shape[1]
    tm = min(tm, n)
    tn = min(tn, m)
    outs = [jax.ShapeDtypeStruct((n, m), dt) for dt in out_dtypes]
    return pl.pallas_call(
        body,
        out_shape=outs,
        grid=(n // tm, m // tn),
        in_specs=[pl.BlockSpec((tm, k), lambda i, j: (i, 0)),
                  pl.BlockSpec((k, tn), lambda i, j: (0, j))] + extra_specs,
        out_specs=[pl.BlockSpec((tm, tn), lambda i, j: (i, j)) for _ in outs],
        compiler_params=_cparams(("parallel", "arbitrary")),
        name=name,
    )(x, w, *extras)


def _mm_plain_body(x_ref, w_ref, o_ref, *, scale):
    acc = _dot(x_ref[...], w_ref[...])
    if scale != 1.0:
        acc = acc * scale
    o_ref[...] = acc.astype(o_ref.dtype)


def mm_plain(x, w, out_dtype, scale=1.0, tm=1024, tn=1024):
    return _mm_call(functools.partial(_mm_plain_body, scale=scale), x, w, [], [],
                    [out_dtype], tm, tn, "mm_plain")[0]


MXU_DIM = 256


def _fox_proj_body(x_ref, w_ref, bd_ref, g_ref, o_ref, *, hd):
    j = pl.program_id(1)
    acc = _dot(x_ref[...], w_ref[...])

    @pl.when(j < 2)
    def _():
        for c in range(acc.shape[1] // MXU_DIM):
            cols = slice(c * MXU_DIM, (c + 1) * MXU_DIM)
            a = acc[:, cols]
            ms = _dot((a * a).astype(BF16), bd_ref[...]) * (1.0 / hd)
            o_ref[:, cols] = (a * lax.rsqrt(ms + EPS) * g_ref[0, :, cols]).astype(o_ref.dtype)

    @pl.when(j >= 2)
    def _():
        o_ref[...] = acc.astype(o_ref.dtype)


def fox_proj(x, w, q_gain, k_gain, scale, hd, tm=512):
    n, d = x.shape
    tm = min(tm, n)
    blk = np.arange(MXU_DIM) // hd
    bd = jnp.asarray(blk[:, None] == blk[None, :], BF16)
    g = jnp.stack([jnp.tile(q_gain.astype(F32) * scale, d // hd),
                   jnp.tile(k_gain.astype(F32), d // hd)]).reshape(2, 1, d)
    return pl.pallas_call(
        functools.partial(_fox_proj_body, hd=hd),
        out_shape=jax.ShapeDtypeStruct((n, 4 * d), BF16),
        grid=(n // tm, 4),
        in_specs=[pl.BlockSpec((tm, d), lambda i, j: (i, 0)),
                  pl.BlockSpec((d, d), lambda i, j: (0, j)),
                  pl.BlockSpec((MXU_DIM, MXU_DIM), lambda i, j: (0, 0)),
                  pl.BlockSpec((1, 1, d), lambda i, j: (jnp.minimum(j, 1), 0, 0))],
        out_specs=pl.BlockSpec((tm, d), lambda i, j: (i, j)),
        compiler_params=_cparams(("parallel", "arbitrary")),
        name="fox_proj",
    )(x, w, bd, g)


def _mm_logsig_body(x_ref, w_ref, b_ref, o_ref, *, mult):
    acc = _dot(x_ref[...], w_ref[...]) + b_ref[...]
    o_ref[...] = (_log_sigmoid(acc) * mult).astype(o_ref.dtype)


def mm_logsig(x, w, bias, mult, tm=512, tn=512):
    m = w.shape[1]
    tn = min(tn, m)
    return _mm_call(functools.partial(_mm_logsig_body, mult=mult), x, w,
                    [bias.astype(F32).reshape(1, m)],
                    [pl.BlockSpec((1, tn), lambda i, j: (0, j))],
                    [F32], tm, tn, "mm_logsig")[0]


def _mm_hgrn_gate_body(x_ref, w_ref, lbl_ref, k_ref, lg_ref, *, layer):
    acc = _dot(x_ref[...], w_ref[...])
    logits = lbl_ref[...]
    e = jnp.exp(logits - jnp.max(logits, axis=0, keepdims=True))
    p = e / jnp.sum(e, axis=0, keepdims=True)
    lb = jnp.zeros_like(p[0:1])
    for u in range(1, layer + 1):
        lb = lb + p[u:u + 1]
    gate = lb + (1.0 - lb) * _sigmoid(acc)
    k_ref[...] = (1.0 - gate).astype(k_ref.dtype)
    lg_ref[...] = jnp.log(gate).astype(lg_ref.dtype)


def mm_hgrn_gate(x, w, lb_logits, layer, tm=512, tn=512):
    m = w.shape[1]
    tn = min(tn, m)
    depth = lb_logits.shape[0]
    return _mm_call(functools.partial(_mm_hgrn_gate_body, layer=layer), x, w,
                    [lb_logits.astype(F32)],
                    [pl.BlockSpec((depth, tn), lambda i, j: (0, j))],
                    [F32, F32], tm, tn, "mm_hgrn_gate")


def _mm_res_norm_body(x_ref, w_ref, h_ref, g_ref, ho_ref, hn_ref):
    hnew = h_ref[...] + _dot(x_ref[...], w_ref[...])
    ho_ref[...] = hnew
    hn_ref[...] = _rms(hnew, g_ref[...]).astype(hn_ref.dtype)


def mm_res_norm(x, w, h, gain, tm=512):
    n, d = h.shape
    return _mm_call(_mm_res_norm_body, x, w, [h, gain.astype(F32).reshape(1, d)],
                    [pl.BlockSpec((min(tm, n), d), lambda i, j: (i, 0)),
                     pl.BlockSpec((1, d), lambda i, j: (0, 0))],
                    [F32, BF16], tm, d, "mm_res_norm")


CUM_BLOCK = 256


def _cumsum_body(x_ref, o_ref, *, t):
    nb = t // CUM_BLOCK
    r = lax.broadcasted_iota(jnp.int32, (CUM_BLOCK, CUM_BLOCK), 0)
    c = lax.broadcasted_iota(jnp.int32, (CUM_BLOCK, CUM_BLOCK), 1)
    tril = jnp.where(r >= c, 1.0, 0.0).astype(BF16)
    carry = jnp.zeros((1, x_ref.shape[-1]), F32)
    for b in range(nb):
        x = x_ref[0, b * CUM_BLOCK:(b + 1) * CUM_BLOCK, :]
        hi, mid, lo = _split3(x)
        cs = _dot(tril, hi) + _dot(tril, mid) + _dot(tril, lo) + carry
        o_ref[0, b * CUM_BLOCK:(b + 1) * CUM_BLOCK, :] = cs
        carry = cs[CUM_BLOCK - 1:CUM_BLOCK, :]


def time_cumsum(x):
    b, t, h = x.shape
    return pl.pallas_call(
        functools.partial(_cumsum_body, t=t),
        out_shape=jax.ShapeDtypeStruct((b, t, h), F32),
        grid=(b,),
        in_specs=[pl.BlockSpec((1, t, h), lambda i: (i, 0, 0))],
        out_specs=pl.BlockSpec((1, t, h), lambda i: (i, 0, 0)),
        compiler_params=_cparams(("parallel",)),
        name="time_cumsum",
    )(x)


def _fox_body(q_ref, k_ref, v_ref, g_ref, c_ref, o_ref, s_ref, m_ref, l_ref, acc_ref, *, tq):
    qi = pl.program_id(2)
    q = q_ref[0]
    lane = lax.broadcasted_iota(jnp.int32, (tq, LANES), 1)
    row = lax.broadcasted_iota(jnp.int32, (tq, tq), 0)
    col = lax.broadcasted_iota(jnp.int32, (tq, tq), 1)
    zero = jnp.zeros_like(q)
    qs = [jnp.where(lane < FOX_HEAD_DIM, q, zero), jnp.where(lane >= FOX_HEAD_DIM, q, zero)]
    nfold = tq // LANES

    def fold(x, op):
        r = x[:, 0:LANES]
        for f in range(1, nfold):
            r = op(r, x[:, f * LANES:(f + 1) * LANES])
        return r

    def scores(j, off):
        ks = k_ref[0, pl.ds(off, tq), :]
        return _dot_nt(qs[j], ks) - c_ref[0, 0, pl.ds(j, 1), pl.ds(off, tq)]

    m_ref[...] = jnp.full(m_ref.shape, -jnp.inf, F32)
    l_ref[...] = jnp.zeros(l_ref.shape, F32)
    acc_ref[...] = jnp.zeros(acc_ref.shape, F32)

    def pass1(kb, carry):
        off = pl.multiple_of(kb * tq, tq)
        for j in range(2):
            s = scores(j, off)
            s_ref[j, :, pl.ds(off, tq)] = s
            m_ref[j] = jnp.maximum(m_ref[j], fold(s, jnp.maximum))
        return carry

    lax.fori_loop(0, qi, pass1, 0)
    offd = pl.multiple_of(qi * tq, tq)
    sd, ms = [], []
    for j in range(2):
        s = jnp.where(row >= col, scores(j, offd), -jnp.inf)
        sd.append(s)
        ms.append(jnp.max(jnp.maximum(m_ref[j], fold(s, jnp.maximum)), axis=-1, keepdims=True))

    def pass2(kb, carry):
        off = pl.multiple_of(kb * tq, tq)
        vs = v_ref[0, pl.ds(off, tq), :]
        for j in range(2):
            p = jnp.exp(s_ref[j, :, pl.ds(off, tq)] - ms[j])
            l_ref[j] += fold(p, jnp.add)
            acc_ref[j] += _dot(p.astype(BF16), vs)
        return carry

    lax.fori_loop(0, qi, pass2, 0)
    vsd = v_ref[0, pl.ds(offd, tq), :]
    outs = []
    for j in range(2):
        p = jnp.exp(sd[j] - ms[j])
        l = jnp.sum(l_ref[j] + fold(p, jnp.add), axis=-1, keepdims=True)
        outs.append((acc_ref[j] + _dot(p.astype(BF16), vsd)) / l)
    o = jnp.where(lane < FOX_HEAD_DIM, outs[0], outs[1])
    o = o * _sigmoid(g_ref[0].astype(F32))
    o_ref[0] = o.astype(o_ref.dtype)


def fox_attention(qkvg, c_rows, tq=512):
    b, t, d4 = qkvg.shape
    d = d4 // 4
    tq = min(tq, t)
    npair = d // LANES
    qspec = pl.BlockSpec((1, tq, LANES), lambda bi, p, i: (bi, i, p))
    return pl.pallas_call(
        functools.partial(_fox_body, tq=tq),
        out_shape=jax.ShapeDtypeStruct((b, t, d), BF16),
        grid=(b, npair, t // tq),
        in_specs=[qspec,
                  pl.BlockSpec((1, t, LANES), lambda bi, p, i: (bi, 0, npair + p)),
                  pl.BlockSpec((1, t, LANES), lambda bi, p, i: (bi, 0, 2 * npair + p)),
                  pl.BlockSpec((1, tq, LANES), lambda bi, p, i: (bi, i, 3 * npair + p)),
                  pl.BlockSpec((1, 1, 2, t), lambda bi, p, i: (bi, p, 0, 0))],
        out_specs=qspec,
        scratch_shapes=[pltpu.VMEM((2, tq, t), F32), pltpu.VMEM((2, tq, LANES), F32),
                        pltpu.VMEM((2, tq, LANES), F32), pltpu.VMEM((2, tq, LANES), F32)],
        compiler_params=_cparams(("parallel", "parallel", "arbitrary")),
        name="fox_attention",
    )(qkvg, qkvg, qkvg, qkvg, c_rows)


_LEVELS = (64, 32, 16, 8, 4, 2)


def _gla_consts():
    c = CHUNK
    t = np.arange(c)
    u = t[None, :]
    blocks = [u <= t[:, None]]
    masks = [np.eye(c, dtype=bool)]
    for lv in _LEVELS:
        half = lv // 2
        blk, pos = t // lv, t % lv
        r = blk * lv + half - 1
        upper = pos >= half
        qrow = upper[:, None] & (u > r[:, None]) & (u <= t[:, None])
        krow = (~upper)[:, None] & (u > t[:, None]) & (u <= r[:, None])
        blocks.append(qrow | krow)
        masks.append((blk[:, None] == blk[None, :]) & upper[:, None] & (~upper)[None, :])
    blocks.append(u > t[:, None])
    mall = np.concatenate(blocks, axis=0).astype(np.float32)
    mask = np.stack(masks).astype(np.float32)
    return jnp.asarray(mall, BF16), jnp.asarray(mask, F32)


def _gla_body(q_ref, k_ref, g_ref, v_ref, r_ref, gain_ref, mall_ref, mask_ref,
              o_ref, st_ref, *, nchunk, hg, dk, dv):
    c = CHUNK
    nl = len(_LEVELS)

    @pl.when(pl.program_id(2) == 0)
    def _():
        st_ref[...] = jnp.zeros_like(st_ref)

    def chunk(ci, carry):
        sl = pl.ds(pl.multiple_of(ci * c, c), c)
        g_hi, g_lo = _split2(g_ref[0, sl, :])
        mall = mall_ref[...]
        xall = jnp.exp(_dot(mall, g_hi) + _dot(mall, g_lo))
        for hh in range(hg):
            ks, vs = slice(hh * dk, (hh + 1) * dk), slice(hh * dv, (hh + 1) * dv)
            q = q_ref[0, sl, ks]
            k = k_ref[0, sl, ks]
            v = v_ref[0, sl, vs]
            x = xall[:, ks]
            scores = mask_ref[0] * _dot_nt(q.astype(BF16), k.astype(BF16))
            for lv in range(nl):
                z = x[(1 + lv) * c:(2 + lv) * c]
                scores = scores + mask_ref[1 + lv] * _dot_nt((q * z).astype(BF16), (k * z).astype(BF16))
            st = st_ref[hh]
            o = _dot(scores.astype(BF16), v) + _dot_nt((q * x[0:c]).astype(BF16), st.astype(BF16))
            kt = (k * x[(1 + nl) * c:(2 + nl) * c]).astype(BF16)
            st_ref[hh] = st * x[c - 1:c, :] + _dot_tn(v, kt)
            r = r_ref[0, sl, vs].astype(F32)
            y = _rms(o, gain_ref[hh]) * (r * _sigmoid(r))
            o_ref[0, sl, vs] = y.astype(o_ref.dtype)
        return carry

    lax.fori_loop(0, nchunk, chunk, 0)


def gated_linear_attention(q, k, g, v, r, gain, heads, dk, dv, hg=4, ts=256):
    b, t, _ = q.shape
    ts = min(ts, t)
    mall, mask = _gla_consts()
    qspec = pl.BlockSpec((1, ts, hg * dk), lambda bi, h, i: (bi, i, h))
    vspec = pl.BlockSpec((1, ts, hg * dv), lambda bi, h, i: (bi, i, h))
    return pl.pallas_call(
        functools.partial(_gla_body, nchunk=ts // CHUNK, hg=hg, dk=dk, dv=dv),
        out_shape=jax.ShapeDtypeStruct((b, t, heads * dv), BF16),
        grid=(b, heads // hg, t // ts),
        in_specs=[qspec, qspec, qspec, vspec, vspec,
                  pl.BlockSpec((hg, 1, dv), lambda bi, h, i: (h, 0, 0)),
                  pl.BlockSpec(mall.shape, lambda bi, h, i: (0, 0)),
                  pl.BlockSpec(mask.shape, lambda bi, h, i: (0, 0, 0))],
        out_specs=vspec,
        scratch_shapes=[pltpu.VMEM((hg, dv, dk), F32)],
        compiler_params=_cparams(("parallel", "parallel", "arbitrary")),
        name="gated_linear_attention",
    )(q, k, g, v, r, gain.astype(F32).reshape(heads, 1, dv), mall, mask)


def _peer_cand_layout():
    k = PEER_TOPK
    ab = [(0, b) for b in range(16)] + [(1, b) for b in range(8)]
    ab += [(2, b) if b < 5 else None for b in range(8)]
    ab += [(3, 0), (3, 1), (3, 2), (3, 3), (4, 0), (4, 1), (4, 2), None]
    ab += [(5, 0), (5, 1), (6, 0), (6, 1), (7, 0), (7, 1), None, None]
    ab += [(a, 0) for a in range(8, 16)]
    assert all(p is None or (p[0] + 1) * (p[1] + 1) <= k for p in ab)
    assert sum(p is not None for p in ab) == sum(k // (a + 1) for a in range(k))
    pos = np.array([PEER_POS_INVALID if p is None else p[0] * k + p[1] for p in ab], np.float32)
    return np.broadcast_to(pos[:, None], (pos.size, LANES)).copy()


PEER_POS_INVALID = 1024.0


def _extract_max(s, ids, big):
    m = jnp.max(s, axis=0, keepdims=True)
    ix = jnp.min(jnp.where(s == m, ids, big), axis=0, keepdims=True)
    return m, ix


def _topk_rows(s, ids, big, k):
    vals, sel = [], []
    for _ in range(k):
        m, ix = _extract_max(s, ids, big)
        vals.append(m)
        sel.append(ix)
        s = jnp.where(ids == ix, -jnp.inf, s)
    return jnp.concatenate(vals, axis=0), jnp.concatenate(sel, axis=0)


def _topk_keys(st, k):
    n = st.shape[0]
    a = jnp.concatenate([st[r:r + 8] for r in range(0, n, 16)], axis=0)
    b = jnp.concatenate([st[r + 8:r + 16] for r in range(0, n, 16)], axis=0)
    r = lax.broadcasted_iota(jnp.int32, a.shape, 0)
    ida = (((r >> 3) << 4) + (r & 7)).astype(F32)
    idb = ida + 8.0
    a_wins = a >= b
    win, idw = jnp.maximum(a, b), jnp.where(a_wins, ida, idb)
    los, idl = jnp.minimum(a, b), jnp.where(a_wins, idb, ida)
    vals, sel = [], []
    for _ in range(k):
        m, ix = _extract_max(win, idw, float(n))
        vals.append(m)
        sel.append(ix)
        hit = idw == ix
        win = jnp.where(hit, los, win)
        idw = jnp.where(hit, idl, idw)
        los = jnp.where(hit, -jnp.inf, los)
    return jnp.concatenate(vals, axis=0), jnp.concatenate(sel, axis=0)


def _peer_route_body(q_ref, sk_ref, pos_ref, i_ref, j_ref, g_ref, *, tm):
    k = PEER_TOPK
    nk = PEER_NKEYS
    pos = jnp.concatenate([pos_ref[...]] * (tm // LANES), axis=1)
    sub = lax.broadcasted_iota(jnp.int32, (8, tm), 0)

    def bc(x, r):
        return jnp.broadcast_to(x[r:r + 1], (8, tm))

    for h in range(PEER_HEADS):
        tops = []
        for p in range(2):
            hp = 2 * h + p
            st = _dot_nt(sk_ref[hp], q_ref[:, hp * nk:(hp + 1) * nk])
            tops.append(_topk_keys(st, k))
        (s0, i0), (s1, i1) = tops
        lo1 = s1[0:8]
        cand = jnp.concatenate([
            bc(s0, 0) + lo1,
            bc(s0, 0) + s1[8:16],
            bc(s0, 1) + lo1,
            bc(s0, 2) + lo1,
            jnp.where(sub < 4, bc(s0, 3), bc(s0, 4)) + jnp.where(sub < 4, lo1, pltpu.roll(lo1, 4, 0)),
            jnp.where(sub < 2, bc(s0, 5), jnp.where(sub < 4, bc(s0, 6), bc(s0, 7)))
            + jnp.where((sub & 1) == 0, bc(s1, 0), bc(s1, 1)),
            s0[8:16] + bc(s1, 0)], axis=0)
        cand = jnp.where(pos < PEER_POS_INVALID, cand, -jnp.inf)
        best, bpos = _topk_rows(cand, pos, 2.0 * PEER_POS_INVALID, k)
        bpos = bpos.astype(jnp.int32)
        ra = bpos >> 4
        rb = bpos & (k - 1)
        isel = jnp.zeros((k, tm), F32)
        jsel = jnp.zeros((k, tm), F32)
        for a in range(k):
            isel = jnp.where(ra == a, i0[a:a + 1], isel)
            jsel = jnp.where(rb == a, i1[a:a + 1], jsel)
        e = jnp.exp(best - best[0:1])
        gates = e / jnp.sum(e, axis=0, keepdims=True)
        i_ref[h * k:(h + 1) * k, :] = isel.astype(jnp.int32)
        j_ref[h * k:(h + 1) * k, :] = jsel.astype(jnp.int32)
        g_ref[h * k:(h + 1) * k, :] = gates


def peer_route(q, sub_keys, tm=256):
    n = q.shape[0]
    tm = min(tm, n)
    pos = jnp.asarray(_peer_cand_layout())
    npair = PEER_HEADS * PEER_TOPK
    out = [jax.ShapeDtypeStruct((npair, n), jnp.int32),
           jax.ShapeDtypeStruct((npair, n), jnp.int32),
           jax.ShapeDtypeStruct((npair, n), F32)]
    ospec = pl.BlockSpec((npair, tm), lambda i: (0, i))
    return pl.pallas_call(
        functools.partial(_peer_route_body, tm=tm),
        out_shape=out,
        grid=(n // tm,),
        in_specs=[pl.BlockSpec((tm, q.shape[1]), lambda i: (i, 0)),
                  pl.BlockSpec(sub_keys.shape, lambda i: (0, 0, 0)),
                  pl.BlockSpec(pos.shape, lambda i: (0, 0))],
        out_specs=[ospec, ospec, ospec],
        compiler_params=_cparams(("parallel",)),
        name="peer_route",
    )(q, sub_keys, pos)


W3_GROUP = 8
W3_BATCH = 2 * W3_GROUP


def _peer_weights_body(i_ref, j_ref, g_ref, o_ref, wa_ref, wb_ref, *, tm):
    nk = PEER_NKEYS
    nb = tm // W3_BATCH
    sub = lax.broadcasted_iota(jnp.int32, (nk, LANES), 0)

    def build(t, w_ref):
        for half in range(2):
            base = pl.multiple_of(t * W3_BATCH + half * W3_GROUP, W3_GROUP)
            it = i_ref[pl.ds(base, W3_GROUP), :]
            jt = j_ref[pl.ds(base, W3_GROUP), :]
            gt = g_ref[pl.ds(base, W3_GROUP), :]
            for s in range(W3_GROUP):
                at = jnp.where(sub == it[s:s + 1], gt[s:s + 1], 0.0).astype(BF16)
                bt = jnp.where(sub == jt[s:s + 1], 1.0, 0.0).astype(BF16)
                w_ref[half, pl.ds(s, nk, stride=W3_GROUP), :] = _dot_nt(at, bt)

    def flush(t, w_ref):
        r0 = pl.multiple_of(t * W3_BATCH, W3_BATCH)
        for i in range(nk):
            rows = jnp.concatenate([w_ref[0, i * W3_GROUP:(i + 1) * W3_GROUP, :],
                                    w_ref[1, i * W3_GROUP:(i + 1) * W3_GROUP, :]], axis=0)
            o_ref[pl.ds(r0, W3_BATCH), i * nk:(i + 1) * nk] = rows.astype(o_ref.dtype)

    build(0, wa_ref)

    def step(u, carry):
        build(2 * u + 1, wb_ref)
        flush(2 * u, wa_ref)
        build(2 * u + 2, wa_ref)
        flush(2 * u + 1, wb_ref)
        return carry

    lax.fori_loop(0, nb // 2 - 1, step, 0)
    build(nb - 1, wb_ref)
    flush(nb - 2, wa_ref)
    flush(nb - 1, wb_ref)


def peer_weights(isel, jsel, gates, tm=128):
    n, npair = isel.shape
    tm = min(tm, n)
    ne = PEER_NKEYS * PEER_NKEYS
    ispec = pl.BlockSpec((tm, npair), lambda i: (i, 0))
    return pl.pallas_call(
        functools.partial(_peer_weights_body, tm=tm),
        out_shape=jax.ShapeDtypeStruct((n, ne), BF16),
        grid=(n // tm,),
        in_specs=[ispec, ispec, ispec],
        out_specs=pl.BlockSpec((tm, ne), lambda i: (i, 0)),
        scratch_shapes=[pltpu.VMEM((2, W3_GROUP * PEER_NKEYS, PEER_NKEYS), F32),
                        pltpu.VMEM((2, W3_GROUP * PEER_NKEYS, PEER_NKEYS), F32)],
        compiler_params=_cparams(("parallel",)),
        name="peer_weights",
    )(isel, jsel, gates)


def _gelu(x):
    return 0.5 * x * (1.0 + lax.erf(x * (1.0 / math.sqrt(2.0))))


def _peer_dense_body(x_ref, u_ref, v_ref, w_ref, h_ref, gn_ref, ho_ref, hn_ref, acc_ref, *, nj):
    j = pl.program_id(1)

    @pl.when(j == 0)
    def _():
        acc_ref[...] = jnp.zeros_like(acc_ref)

    hid = _dot_nt(x_ref[...], u_ref[...])
    a = (_gelu(hid) * w_ref[...].astype(F32)).astype(BF16)
    acc_ref[...] += _dot(a, v_ref[...])

    @pl.when(j == nj - 1)
    def _():
        hnew = h_ref[...] + acc_ref[...]
        ho_ref[...] = hnew
        hn_ref[...] = _rms(hnew, gn_ref[...]).astype(hn_ref.dtype)


def peer_dense(x, u, v, w, h, gain_next, tm=1024, te=1024):
    n, d = x.shape
    ne = u.shape[0]
    tm = min(tm, n)
    te = min(te, ne)
    nj = ne // te
    rowspec = pl.BlockSpec((tm, d), lambda i, j: (i, 0))
    rowspec_in = pl.BlockSpec((tm, d), lambda i, j: (i, 0), pipeline_mode=pl.Buffered(1))
    return pl.pallas_call(
        functools.partial(_peer_dense_body, nj=nj),
        out_shape=[jax.ShapeDtypeStruct((n, d), F32), jax.ShapeDtypeStruct((n, d), BF16)],
        grid=(n // tm, nj),
        in_specs=[rowspec_in,
                  pl.BlockSpec((te, d), lambda i, j: (j, 0)),
                  pl.BlockSpec((te, d), lambda i, j: (j, 0)),
                  pl.BlockSpec((tm, te), lambda i, j: (i, j)),
                  rowspec_in,
                  pl.BlockSpec((1, d), lambda i, j: (0, 0))],
        out_specs=[rowspec, rowspec],
        scratch_shapes=[pltpu.VMEM((tm, d), F32)],
        compiler_params=_cparams(("parallel", "arbitrary")),
        name="peer_dense",
    )(x, u, v, w, h, gain_next.astype(F32).reshape(1, d))


def _fox_layer(hn, b, t, w_in, b_f, q_gain, k_gain):
    d = hn.shape[1]
    w = w_in.astype(BF16)
    qkvg = fox_proj(hn, w[:, 0:4 * d], q_gain, k_gain, FOX_HEAD_DIM ** -0.5, FOX_HEAD_DIM)
    lf = mm_logsig(hn, w[:, 4 * d:], b_f, 1.0)
    c = time_cumsum(lf.reshape(b, t, FOX_HEADS))
    c_rows = c.transpose(0, 2, 1).reshape(b, FOX_HEADS // 2, 2, t)
    o = fox_attention(qkvg.reshape(b, t, 4 * d), c_rows)
    return o.reshape(b * t, d)


def _gla_layer(hn, b, t, w_in, w_up, b_alpha, out_gain):
    d = hn.shape[1]
    gk = GLA_HEADS * GLA_DK
    w = w_in.astype(BF16)
    q = mm_plain(hn, w[:, 0:gk], F32, scale=GLA_DK ** -0.5)
    k = mm_plain(hn, w[:, gk:2 * gk], F32)
    v = mm_plain(hn, w[:, 2 * gk:2 * gk + d], BF16)
    r = mm_plain(hn, w[:, 2 * gk + d:2 * gk + 2 * d], BF16)
    low = mm_plain(hn, w[:, 2 * gk + 2 * d:], BF16)
    log_a = mm_logsig(low, w_up.astype(BF16), b_alpha, 1.0 / GLA_TAU)
    o = gated_linear_attention(q.reshape(b, t, gk), k.reshape(b, t, gk), log_a.reshape(b, t, gk),
                               v.reshape(b, t, d), r.reshape(b, t, d), out_gain,
                               GLA_HEADS, GLA_DK, GLA_DV)
    return o.reshape(b * t, d)


def _hgrn_layer(hn, b, t, layer, w_in, lb_logits, out_gain):
    d = hn.shape[1]
    w = w_in.astype(BF16)
    q = mm_plain(hn, w[:, 0:d], F32, scale=HGRN_DK ** -0.5)
    k, log_g = mm_hgrn_gate(hn, w[:, d:2 * d], lb_logits, layer)
    v = mm_plain(hn, w[:, 2 * d:3 * d], BF16)
    r = mm_plain(hn, w[:, 3 * d:4 * d], BF16)
    shp = (b, t, d)
    o = gated_linear_attention(q.reshape(shp), k.reshape(shp), log_g.reshape(shp),
                               v.reshape(shp), r.reshape(shp), out_gain,
                               HGRN_HEADS, HGRN_DK, HGRN_DV)
    return o.reshape(b * t, d)


def _peer_layer(h, hn, w_q, sub_keys, u, v, gain_next):
    q = mm_plain(hn, w_q.astype(BF16), BF16)
    sk = sub_keys.astype(BF16).reshape(2 * PEER_HEADS, PEER_NKEYS, -1)
    isel, jsel, gates = peer_route(q, sk)
    w = peer_weights(isel.T, jsel.T, gates.T)
    return peer_dense(hn, u.astype(BF16), v.astype(BF16), w, h, gain_next)


def kernel(x, norm_mix, norm_ffn, fox_w_in, fox_b_f, fox_q_gain, fox_k_gain, fox_w_out,
           gla_w_in, gla_w_up, gla_b_alpha, gla_out_gain, gla_w_out,
           hgrn_w_in, hgrn_lb_logits, hgrn_out_gain, hgrn_w_out,
           peer_w_q, peer_sub_keys, peer_u, peer_v):
    b, t, d = x.shape
    depth = norm_mix.shape[0]
    h = x.reshape(b * t, d)
    hn = rmsnorm(h, norm_mix[0])
    for i in range(depth):
        m, j = i % N_MIXERS, i // N_MIXERS
        if m == 0:
            o = _fox_layer(hn, b, t, fox_w_in[j], fox_b_f[j], fox_q_gain[j], fox_k_gain[j])
            w_out = fox_w_out[j]
        elif m == 1:
            o = _gla_layer(hn, b, t, gla_w_in[j], gla_w_up[j], gla_b_alpha[j], gla_out_gain[j])
            w_out = gla_w_out[j]
        else:
            o = _hgrn_layer(hn, b, t, i, hgrn_w_in[j], hgrn_lb_logits, hgrn_out_gain[j])
            w_out = hgrn_w_out[j]
        h, hn = mm_res_norm(o, w_out.astype(BF16), h, norm_ffn[i])
        gain_next = norm_mix[(i + 1) % depth]
        h, hn = _peer_layer(h, hn, peer_w_q[i], peer_sub_keys[i], peer_u[i], peer_v[i], gain_next)
    return h.reshape(b, t, d)
```

```python
import functools
import math

import numpy as np
import jax
import jax.numpy as jnp
from jax import lax
from jax.experimental import pallas as pl
from jax.experimental.pallas import tpu as pltpu

F32 = jnp.float32
BF16 = jnp.bfloat16
EPS = 1e-6

DEPTH = 4
N_MIXERS = 3
CHUNK = 64
FOX_HEADS = 16
FOX_HEAD_DIM = 64
GLA_HEADS = 4
GLA_DK = 128
GLA_DV = 256
GLA_TAU = 16.0
HGRN_HEADS = 8
HGRN_DK = 128
HGRN_DV = 128
PEER_HEADS = 8
PEER_NKEYS = 128
PEER_TOPK = 16
LANES = 128
VMEM_LIMIT = 56 * 1024 * 1024


def _cparams(sem):
    return pltpu.CompilerParams(dimension_semantics=sem, vmem_limit_bytes=VMEM_LIMIT)


def _dot(a, b):
    return jnp.dot(a, b, preferred_element_type=F32)


def _dot_nt(a, b):
    return lax.dot_general(a, b, (((1,), (1,)), ((), ())), preferred_element_type=F32)


def _dot_tn(a, b):
    return lax.dot_general(a, b, (((0,), (0,)), ((), ())), preferred_element_type=F32)


def _split2(x):
    hi = x.astype(BF16)
    lo = (x - hi.astype(F32)).astype(BF16)
    return hi, lo


def _split3(x):
    hi = x.astype(BF16)
    r = x - hi.astype(F32)
    mid = r.astype(BF16)
    lo = (r - mid.astype(F32)).astype(BF16)
    return hi, mid, lo


def _log_sigmoid(x):
    return jnp.minimum(x, 0.0) - jnp.log1p(jnp.exp(-jnp.abs(x)))


def _sigmoid(x):
    return 1.0 / (1.0 + jnp.exp(-x))


def _rms(x, gain):
    ms = jnp.mean(x * x, axis=-1, keepdims=True)
    return x * lax.rsqrt(ms + EPS) * gain


def _rmsnorm_body(x_ref, g_ref, o_ref):
    o_ref[...] = _rms(x_ref[...], g_ref[...]).astype(o_ref.dtype)


def rmsnorm(x, gain, tm=512):
    n, d = x.shape
    tm = min(tm, n)
    return pl.pallas_call(
        _rmsnorm_body,
        out_shape=jax.ShapeDtypeStruct((n, d), BF16),
        grid=(n // tm,),
        in_specs=[pl.BlockSpec((tm, d), lambda i: (i, 0)),
                  pl.BlockSpec((1, d), lambda i: (0, 0))],
        out_specs=pl.BlockSpec((tm, d), lambda i: (i, 0)),
        compiler_params=_cparams(("parallel",)),
        name="rmsnorm",
    )(x, gain.reshape(1, d))


def _mm_call(body, x, w, extras, extra_specs, out_dtypes, tm, tn, name):
    n, k = x.shape
    m = w.shape[1]
    tm = min(tm, n)
    tn = min(tn, m)
    outs = [jax.ShapeDtypeStruct((n, m), dt) for dt in out_dtypes]
    return pl.pallas_call(
        body,
        out_shape=outs,
        grid=(n // tm, m // tn),
        in_specs=[pl.BlockSpec((tm, k), lambda i, j: (i, 0)),
                  pl.BlockSpec((k, tn), lambda i, j: (0, j))] + extra_specs,
        out_specs=[pl.BlockSpec((tm, tn), lambda i, j: (i, j)) for _ in outs],
        compiler_params=_cparams(("parallel", "arbitrary")),
        name=name,
    )(x, w, *extras)


def _mm_plain_body(x_ref, w_ref, o_ref, *, scale):
    acc = _dot(x_ref[...], w_ref[...])
    if scale != 1.0:
        acc = acc * scale
    o_ref[...] = acc.astype(o_ref.dtype)


def mm_plain(x, w, out_dtype, scale=1.0, tm=1024, tn=1024):
    return _mm_call(functools.partial(_mm_plain_body, scale=scale), x, w, [], [],
                    [out_dtype], tm, tn, "mm_plain")[0]


MXU_DIM = 256


def _fox_proj_body(x_ref, w_ref, bd_ref, g_ref, o_ref, *, hd):
    j = pl.program_id(1)
    acc = _dot(x_ref[...], w_ref[...])

    @pl.when(j < 2)
    def _():
        for c in range(acc.shape[1] // MXU_DIM):
            cols = slice(c * MXU_DIM, (c + 1) * MXU_DIM)
            a = acc[:, cols]
            ms = _dot((a * a).astype(BF16), bd_ref[...]) * (1.0 / hd)
            o_ref[:, cols] = (a * lax.rsqrt(ms + EPS) * g_ref[0, :, cols]).astype(o_ref.dtype)

    @pl.when(j >= 2)
    def _():
        o_ref[...] = acc.astype(o_ref.dtype)


def fox_proj(x, w, q_gain, k_gain, scale, hd, tm=512):
    n, d = x.shape
    tm = min(tm, n)
    blk = np.arange(MXU_DIM) // hd
    bd = jnp.asarray(blk[:, None] == blk[None, :], BF16)
    g = jnp.stack([jnp.tile(q_gain.astype(F32) * scale, d // hd),
                   jnp.tile(k_gain.astype(F32), d // hd)]).reshape(2, 1, d)
    return pl.pallas_call(
        functools.partial(_fox_proj_body, hd=hd),
        out_shape=jax.ShapeDtypeStruct((n, 4 * d), BF16),
        grid=(n // tm, 4),
        in_specs=[pl.BlockSpec((tm, d), lambda i, j: (i, 0)),
                  pl.BlockSpec((d, d), lambda i, j: (0, j)),
                  pl.BlockSpec((MXU_DIM, MXU_DIM), lambda i, j: (0, 0)),
                  pl.BlockSpec((1, 1, d), lambda i, j: (jnp.minimum(j, 1), 0, 0))],
        out_specs=pl.BlockSpec((tm, d), lambda i, j: (i, j)),
        compiler_params=_cparams(("parallel", "arbitrary")),
        name="fox_proj",
    )(x, w, bd, g)


def _mm_logsig_body(x_ref, w_ref, b_ref, o_ref, *, mult):
    acc = _dot(x_ref[...], w_ref[...]) + b_ref[...]
    o_ref[...] = (_log_sigmoid(acc) * mult).astype(o_ref.dtype)


def mm_logsig(x, w, bias, mult, tm=512, tn=512):
    m = w.shape[1]
    tn = min(tn, m)
    return _mm_call(functools.partial(_mm_logsig_body, mult=mult), x, w,
                    [bias.astype(F32).reshape(1, m)],
                    [pl.BlockSpec((1, tn), lambda i, j: (0, j))],
                    [F32], tm, tn, "mm_logsig")[0]


def _mm_hgrn_gate_body(x_ref, w_ref, lbl_ref, k_ref, lg_ref, *, layer):
    acc = _dot(x_ref[...], w_ref[...])
    logits = lbl_ref[...]
    e = jnp.exp(logits - jnp.max(logits, axis=0, keepdims=True))
    p = e / jnp.sum(e, axis=0, keepdims=True)
    lb = jnp.zeros_like(p[0:1])
    for u in range(1, layer + 1):
        lb = lb + p[u:u + 1]
    gate = lb + (1.0 - lb) * _sigmoid(acc)
    k_ref[...] = (1.0 - gate).astype(k_ref.dtype)
    lg_ref[...] = jnp.log(gate).astype(lg_ref.dtype)


def mm_hgrn_gate(x, w, lb_logits, layer, tm=512, tn=512):
    m = w.shape[1]
    tn = min(tn, m)
    depth = lb_logits.shape[0]
    return _mm_call(functools.partial(_mm_hgrn_gate_body, layer=layer), x, w,
                    [lb_logits.astype(F32)],
                    [pl.BlockSpec((depth, tn), lambda i, j: (0, j))],
                    [F32, F32], tm, tn, "mm_hgrn_gate")


def _mm_res_norm_body(x_ref, w_ref, h_ref, g_ref, ho_ref, hn_ref):
    hnew = h_ref[...] + _dot(x_ref[...], w_ref[...])
    ho_ref[...] = hnew
    hn_ref[...] = _rms(hnew, g_ref[...]).astype(hn_ref.dtype)


def mm_res_norm(x, w, h, gain, tm=512):
    n, d = h.shape
    return _mm_call(_mm_res_norm_body, x, w, [h, gain.astype(F32).reshape(1, d)],
                    [pl.BlockSpec((min(tm, n), d), lambda i, j: (i, 0)),
                     pl.BlockSpec((1, d), lambda i, j: (0, 0))],
                    [F32, BF16], tm, d, "mm_res_norm")


CUM_BLOCK = 256


def _cumsum_body(x_ref, o_ref, *, t):
    nb = t // CUM_BLOCK
    r = lax.broadcasted_iota(jnp.int32, (CUM_BLOCK, CUM_BLOCK), 0)
    c = lax.broadcasted_iota(jnp.int32, (CUM_BLOCK, CUM_BLOCK), 1)
    tril = jnp.where(r >= c, 1.0, 0.0).astype(BF16)
    carry = jnp.zeros((1, x_ref.shape[-1]), F32)
    for b in range(nb):
        x = x_ref[0, b * CUM_BLOCK:(b + 1) * CUM_BLOCK, :]
        hi, mid, lo = _split3(x)
        cs = _dot(tril, hi) + _dot(tril, mid) + _dot(tril, lo) + carry
        o_ref[0, b * CUM_BLOCK:(b + 1) * CUM_BLOCK, :] = cs
        carry = cs[CUM_BLOCK - 1:CUM_BLOCK, :]


def time_cumsum(x):
    b, t, h = x.shape
    return pl.pallas_call(
        functools.partial(_cumsum_body, t=t),
        out_shape=jax.ShapeDtypeStruct((b, t, h), F32),
        grid=(b,),
        in_specs=[pl.BlockSpec((1, t, h), lambda i: (i, 0, 0))],
        out_specs=pl.BlockSpec((1, t, h), lambda i: (i, 0, 0)),
        compiler_params=_cparams(("parallel",)),
        name="time_cumsum",
    )(x)


def _fox_body(q_ref, k_ref, v_ref, g_ref, c_ref, o_ref, s_ref, m_ref, l_ref, acc_ref, *, tq):
    qi = pl.program_id(2)
    q = q_ref[0]
    lane = lax.broadcasted_iota(jnp.int32, (tq, LANES), 1)
    row = lax.broadcasted_iota(jnp.int32, (tq, tq), 0)
    col = lax.broadcasted_iota(jnp.int32, (tq, tq), 1)
    zero = jnp.zeros_like(q)
    qs = [jnp.where(lane < FOX_HEAD_DIM, q, zero), jnp.where(lane >= FOX_HEAD_DIM, q, zero)]
    nfold = tq // LANES

    def fold(x, op):
        r = x[:, 0:LANES]
        for f in range(1, nfold):
            r = op(r, x[:, f * LANES:(f + 1) * LANES])
        return r

    def scores(j, off):
        ks = k_ref[0, pl.ds(off, tq), :]
        return _dot_nt(qs[j], ks) - c_ref[0, 0, pl.ds(j, 1), pl.ds(off, tq)]

    m_ref[...] = jnp.full(m_ref.shape, -jnp.inf, F32)
    l_ref[...] = jnp.zeros(l_ref.shape, F32)
    acc_ref[...] = jnp.zeros(acc_ref.shape, F32)

    def pass1(kb, carry):
        off = pl.multiple_of(kb * tq, tq)
        for j in range(2):
            s = scores(j, off)
            s_ref[j, :, pl.ds(off, tq)] = s
            m_ref[j] = jnp.maximum(m_ref[j], fold(s, jnp.maximum))
        return carry

    lax.fori_loop(0, qi, pass1, 0)
    offd = pl.multiple_of(qi * tq, tq)
    sd, ms = [], []
    for j in range(2):
        s = jnp.where(row >= col, scores(j, offd), -jnp.inf)
        sd.append(s)
        ms.append(jnp.max(jnp.maximum(m_ref[j], fold(s, jnp.maximum)), axis=-1, keepdims=True))

    def pass2(kb, carry):
        off = pl.multiple_of(kb * tq, tq)
        vs = v_ref[0, pl.ds(off, tq), :]
        for j in range(2):
            p = jnp.exp(s_ref[j, :, pl.ds(off, tq)] - ms[j])
            l_ref[j] += fold(p, jnp.add)
            acc_ref[j] += _dot(p.astype(BF16), vs)
        return carry

    lax.fori_loop(0, qi, pass2, 0)
    vsd = v_ref[0, pl.ds(offd, tq), :]
    outs = []
    for j in range(2):
        p = jnp.exp(sd[j] - ms[j])
        l = jnp.sum(l_ref[j] + fold(p, jnp.add), axis=-1, keepdims=True)
        outs.append((acc_ref[j] + _dot(p.astype(BF16), vsd)) / l)
    o = jnp.where(lane < FOX_HEAD_DIM, outs[0], outs[1])
    o = o * _sigmoid(g_ref[0].astype(F32))
    o_ref[0] = o.astype(o_ref.dtype)


def fox_attention(qkvg, c_rows, tq=512):
    b, t, d4 = qkvg.shape
    d = d4 // 4
    tq = min(tq, t)
    npair = d // LANES
    qspec = pl.BlockSpec((1, tq, LANES), lambda bi, p, i: (bi, i, p))
    return pl.pallas_call(
        functools.partial(_fox_body, tq=tq),
        out_shape=jax.ShapeDtypeStruct((b, t, d), BF16),
        grid=(b, npair, t // tq),
        in_specs=[qspec,
                  pl.BlockSpec((1, t, LANES), lambda bi, p, i: (bi, 0, npair + p)),
                  pl.BlockSpec((1, t, LANES), lambda bi, p, i: (bi, 0, 2 * npair + p)),
                  pl.BlockSpec((1, tq, LANES), lambda bi, p, i: (bi, i, 3 * npair + p)),
                  pl.BlockSpec((1, 1, 2, t), lambda bi, p, i: (bi, p, 0, 0))],
        out_specs=qspec,
        scratch_shapes=[pltpu.VMEM((2, tq, t), F32), pltpu.VMEM((2, tq, LANES), F32),
                        pltpu.VMEM((2, tq, LANES), F32), pltpu.VMEM((2, tq, LANES), F32)],
        compiler_params=_cparams(("parallel", "parallel", "arbitrary")),
        name="fox_attention",
    )(qkvg, qkvg, qkvg, qkvg, c_rows)


_LEVELS = (64, 32, 16, 8, 4, 2)


def _gla_consts():
    c = CHUNK
    t = np.arange(c)
    u = t[None, :]
    blocks = [u <= t[:, None]]
    masks = [np.eye(c, dtype=bool)]
    for lv in _LEVELS:
        half = lv // 2
        blk, pos = t // lv, t % lv
        r = blk * lv + half - 1
        upper = pos >= half
        qrow = upper[:, None] & (u > r[:, None]) & (u <= t[:, None])
        krow = (~upper)[:, None] & (u > t[:, None]) & (u <= r[:, None])
        blocks.append(qrow | krow)
        masks.append((blk[:, None] == blk[None, :]) & upper[:, None] & (~upper)[None, :])
    blocks.append(u > t[:, None])
    mall = np.concatenate(blocks, axis=0).astype(np.float32)
    mask = np.stack(masks).astype(np.float32)
    return jnp.asarray(mall, BF16), jnp.asarray(mask, F32)


def _gla_body(q_ref, k_ref, g_ref, v_ref, r_ref, gain_ref, mall_ref, mask_ref,
              o_ref, st_ref, *, nchunk, hg, dk, dv):
    c = CHUNK
    nl = len(_LEVELS)

    @pl.when(pl.program_id(2) == 0)
    def _():
        st_ref[...] = jnp.zeros_like(st_ref)

    def chunk(ci, carry):
        sl = pl.ds(pl.multiple_of(ci * c, c), c)
        g_hi, g_lo = _split2(g_ref[0, sl, :])
        mall = mall_ref[...]
        xall = jnp.exp(_dot(mall, g_hi) + _dot(mall, g_lo))
        for hh in range(hg):
            ks, vs = slice(hh * dk, (hh + 1) * dk), slice(hh * dv, (hh + 1) * dv)
            q = q_ref[0, sl, ks]
            k = k_ref[0, sl, ks]
            v = v_ref[0, sl, vs]
            x = xall[:, ks]
            scores = mask_ref[0] * _dot_nt(q.astype(BF16), k.astype(BF16))
            for lv in range(nl):
                z = x[(1 + lv) * c:(2 + lv) * c]
                scores = scores + mask_ref[1 + lv] * _dot_nt((q * z).astype(BF16), (k * z).astype(BF16))
            st = st_ref[hh]
            o = _dot(scores.astype(BF16), v) + _dot_nt((q * x[0:c]).astype(BF16), st.astype(BF16))
            kt = (k * x[(1 + nl) * c:(2 + nl) * c]).astype(BF16)
            st_ref[hh] = st * x[c - 1:c, :] + _dot_tn(v, kt)
            r = r_ref[0, sl, vs].astype(F32)
            y = _rms(o, gain_ref[hh]) * (r * _sigmoid(r))
            o_ref[0, sl, vs] = y.astype(o_ref.dtype)
        return carry

    lax.fori_loop(0, nchunk, chunk, 0)


def gated_linear_attention(q, k, g, v, r, gain, heads, dk, dv, hg=4, ts=256):
    b, t, _ = q.shape
    ts = min(ts, t)
    mall, mask = _gla_consts()
    qspec = pl.BlockSpec((1, ts, hg * dk), lambda bi, h, i: (bi, i, h))
    vspec = pl.BlockSpec((1, ts, hg * dv), lambda bi, h, i: (bi, i, h))
    return pl.pallas_call(
        functools.partial(_gla_body, nchunk=ts // CHUNK, hg=hg, dk=dk, dv=dv),
        out_shape=jax.ShapeDtypeStruct((b, t, heads * dv), BF16),
        grid=(b, heads // hg, t // ts),
        in_specs=[qspec, qspec, qspec, vspec, vspec,
                  pl.BlockSpec((hg, 1, dv), lambda bi, h, i: (h, 0, 0)),
                  pl.BlockSpec(mall.shape, lambda bi, h, i: (0, 0)),
                  pl.BlockSpec(mask.shape, lambda bi, h, i: (0, 0, 0))],
        out_specs=vspec,
        scratch_shapes=[pltpu.VMEM((hg, dv, dk), F32)],
        compiler_params=_cparams(("parallel", "parallel", "arbitrary")),
        name="gated_linear_attention",
    )(q, k, g, v, r, gain.astype(F32).reshape(heads, 1, dv), mall, mask)


def _peer_cand_layout():
    k = PEER_TOPK
    ab = [(0, b) for b in range(16)] + [(1, b) for b in range(8)]
    ab += [(2, b) if b < 5 else None for b in range(8)]
    ab += [(3, 0), (3, 1), (3, 2), (3, 3), (4, 0), (4, 1), (4, 2), None]
    ab += [(5, 0), (5, 1), (6, 0), (6, 1), (7, 0), (7, 1), None, None]
    ab += [(a, 0) for a in range(8, 16)]
    assert all(p is None or (p[0] + 1) * (p[1] + 1) <= k for p in ab)
    assert sum(p is not None for p in ab) == sum(k // (a + 1) for a in range(k))
    pos = np.array([PEER_POS_INVALID if p is None else p[0] * k + p[1] for p in ab], np.float32)
    return np.broadcast_to(pos[:, None], (pos.size, LANES)).copy()


PEER_POS_INVALID = 1024.0


def _extract_max(s, ids, big):
    m = jnp.max(s, axis=0, keepdims=True)
    ix = jnp.min(jnp.where(s == m, ids, big), axis=0, keepdims=True)
    return m, ix


def _topk_rows(s, ids, big, k):
    vals, sel = [], []
    for _ in range(k):
        m, ix = _extract_max(s, ids, big)
        vals.append(m)
        sel.append(ix)
        s = jnp.where(ids == ix, -jnp.inf, s)
    return jnp.concatenate(vals, axis=0), jnp.concatenate(sel, axis=0)


def _topk_keys(st, k):
    n = st.shape[0]
    a = jnp.concatenate([st[r:r + 8] for r in range(0, n, 16)], axis=0)
    b = jnp.concatenate([st[r + 8:r + 16] for r in range(0, n, 16)], axis=0)
    r = lax.broadcasted_iota(jnp.int32, a.shape, 0)
    ida = (((r >> 3) << 4) + (r & 7)).astype(F32)
    idb = ida + 8.0
    a_wins = a >= b
    win, idw = jnp.maximum(a, b), jnp.where(a_wins, ida, idb)
    los, idl = jnp.minimum(a, b), jnp.where(a_wins, idb, ida)
    vals, sel = [], []
    for _ in range(k):
        m, ix = _extract_max(win, idw, float(n))
        vals.append(m)
        sel.append(ix)
        hit = idw == ix
        win = jnp.where(hit, los, win)
        idw = jnp.where(hit, idl, idw)
        los = jnp.where(hit, -jnp.inf, los)
    return jnp.concatenate(vals, axis=0), jnp.concatenate(sel, axis=0)


def _route_head(q_ref, sk_ref, pos, h, tm):
    k = PEER_TOPK
    nk = PEER_NKEYS
    sub = lax.broadcasted_iota(jnp.int32, (8, tm), 0)

    def bc(x, r):
        return jnp.broadcast_to(x[r:r + 1], (8, tm))

    tops = []
    for p in range(2):
        hp = 2 * h + p
        st = _dot_nt(sk_ref[hp], q_ref[:, hp * nk:(hp + 1) * nk])
        tops.append(_topk_keys(st, k))
    (s0, i0), (s1, i1) = tops
    lo1 = s1[0:8]
    cand = jnp.concatenate([
        bc(s0, 0) + lo1,
        bc(s0, 0) + s1[8:16],
        bc(s0, 1) + lo1,
        bc(s0, 2) + lo1,
        jnp.where(sub < 4, bc(s0, 3), bc(s0, 4)) + jnp.where(sub < 4, lo1, pltpu.roll(lo1, 4, 0)),
        jnp.where(sub < 2, bc(s0, 5), jnp.where(sub < 4, bc(s0, 6), bc(s0, 7)))
        + jnp.where((sub & 1) == 0, bc(s1, 0), bc(s1, 1)),
        s0[8:16] + bc(s1, 0)], axis=0)
    cand = jnp.where(pos < PEER_POS_INVALID, cand, -jnp.inf)
    best, bpos = _topk_rows(cand, pos, 2.0 * PEER_POS_INVALID, k)
    bpos = bpos.astype(jnp.int32)
    ra = bpos >> 4
    rb = bpos & (k - 1)
    isel = jnp.zeros((k, tm), F32)
    jsel = jnp.zeros((k, tm), F32)
    for a in range(k):
        isel = jnp.where(ra == a, i0[a:a + 1], isel)
        jsel = jnp.where(rb == a, i1[a:a + 1], jsel)
    e = jnp.exp(best - best[0:1])
    gates = e / jnp.sum(e, axis=0, keepdims=True)
    return isel, jsel, gates


W3_GROUP = 8
W3_BATCH = 2 * W3_GROUP


W3_NBUF = 4


def _peer_route_weights_body(q_ref, sk_ref, pos_ref, o_ref, ri_ref, rj_ref, rg_ref, pt_ref,
                             *w_refs, tm):
    k = PEER_TOPK
    nk = PEER_NKEYS
    ngroup = tm // LANES
    nbatch = tm // W3_BATCH
    per_head = nbatch // PEER_HEADS

    @pl.when(pl.program_id(0) == 0)
    def _():
        ri_ref[...] = jnp.zeros_like(ri_ref)
        rj_ref[...] = jnp.zeros_like(rj_ref)
        rg_ref[...] = jnp.zeros_like(rg_ref)

    for a, r_ref in enumerate((ri_ref, rj_ref, rg_ref)):
        for c in range(ngroup):
            pt_ref[a, c * LANES:(c + 1) * LANES, :] = r_ref[:, c * LANES:(c + 1) * LANES].T
    pos = jnp.concatenate([pos_ref[...]] * ngroup, axis=1)
    sub = lax.broadcasted_iota(jnp.int32, (nk, LANES), 0).astype(F32)

    def build(t):
        w_ref = w_refs[t % W3_NBUF]
        for half in range(2):
            tok0 = t * W3_BATCH + half * W3_GROUP
            for s in range(W3_GROUP):
                irow, jrow, grow = (pt_ref[a, tok0 + s:tok0 + s + 1, :] for a in range(3))
                at = jnp.where(sub == irow, grow, 0.0).astype(BF16)
                bt = jnp.where(sub == jrow, 1.0, 0.0).astype(BF16)
                w_ref[half, pl.ds(s, nk, stride=W3_GROUP), :] = _dot_nt(at, bt)

    def flush(t):
        w_ref = w_refs[t % W3_NBUF]
        for i in range(nk):
            rows = jnp.concatenate([w_ref[0, i * W3_GROUP:(i + 1) * W3_GROUP, :],
                                    w_ref[1, i * W3_GROUP:(i + 1) * W3_GROUP, :]], axis=0)
            o_ref[t * W3_BATCH:(t + 1) * W3_BATCH, i * nk:(i + 1) * nk] = rows.astype(o_ref.dtype)

    for h in range(PEER_HEADS):
        isel, jsel, gates = _route_head(q_ref, sk_ref, pos, h, tm)
        ri_ref[h * k:(h + 1) * k, :] = isel
        rj_ref[h * k:(h + 1) * k, :] = jsel
        rg_ref[h * k:(h + 1) * k, :] = gates
        for t in range(h * per_head, (h + 1) * per_head):
            build(t)
        for t in range(h * per_head, (h + 1) * per_head):
            flush(t)


def peer_route_weights(q, sub_keys, tm=128):
    n = q.shape[0]
    tm = min(tm, n)
    nblk = n // tm
    ne = PEER_NKEYS * PEER_NKEYS
    npair = PEER_HEADS * PEER_TOPK
    pos = jnp.asarray(_peer_cand_layout())
    return pl.pallas_call(
        functools.partial(_peer_route_weights_body, tm=tm),
        out_shape=jax.ShapeDtypeStruct((n, ne), BF16),
        grid=(nblk + 1,),
        in_specs=[pl.BlockSpec((tm, q.shape[1]), lambda s: (jnp.minimum(s, nblk - 1), 0)),
                  pl.BlockSpec(sub_keys.shape, lambda s: (0, 0, 0)),
                  pl.BlockSpec(pos.shape, lambda s: (0, 0))],
        out_specs=pl.BlockSpec((tm, ne), lambda s: (jnp.maximum(s - 1, 0), 0)),
        scratch_shapes=[pltpu.VMEM((npair, tm), F32)] * 3 + [pltpu.VMEM((3, tm, npair), F32)]
        + [pltpu.VMEM((2, W3_GROUP * PEER_NKEYS, PEER_NKEYS), F32)] * W3_NBUF,
        compiler_params=_cparams(("arbitrary",)),
        name="peer_route_weights",
    )(q, sub_keys, pos)


def _gelu(x):
    return 0.5 * x * (1.0 + lax.erf(x * (1.0 / math.sqrt(2.0))))


def _peer_dense_body(x_ref, u_ref, v_ref, w_ref, h_ref, gn_ref, ho_ref, hn_ref, acc_ref, *, nj):
    j = pl.program_id(1)

    @pl.when(j == 0)
    def _():
        acc_ref[...] = jnp.zeros_like(acc_ref)

    hid = _dot_nt(x_ref[...], u_ref[...])
    a = (_gelu(hid) * w_ref[...].astype(F32)).astype(BF16)
    acc_ref[...] += _dot(a, v_ref[...])

    @pl.when(j == nj - 1)
    def _():
        hnew = h_ref[...] + acc_ref[...]
        ho_ref[...] = hnew
        hn_ref[...] = _rms(hnew, gn_ref[...]).astype(hn_ref.dtype)


def peer_dense(x, u, v, w, h, gain_next, tm=1024, te=1024):
    n, d = x.shape
    ne = u.shape[0]
    tm = min(tm, n)
    te = min(te, ne)
    nj = ne // te
    rowspec = pl.BlockSpec((tm, d), lambda i, j: (i, 0))
    rowspec_in = pl.BlockSpec((tm, d), lambda i, j: (i, 0), pipeline_mode=pl.Buffered(1))
    return pl.pallas_call(
        functools.partial(_peer_dense_body, nj=nj),
        out_shape=[jax.ShapeDtypeStruct((n, d), F32), jax.ShapeDtypeStruct((n, d), BF16)],
        grid=(n // tm, nj),
        in_specs=[rowspec_in,
                  pl.BlockSpec((te, d), lambda i, j: (j, 0)),
                  pl.BlockSpec((te, d), lambda i, j: (j, 0)),
                  pl.BlockSpec((tm, te), lambda i, j: (i, j)),
                  rowspec_in,
                  pl.BlockSpec((1, d), lambda i, j: (0, 0))],
        out_specs=[rowspec, rowspec],
        scratch_shapes=[pltpu.VMEM((tm, d), F32)],
        compiler_params=_cparams(("parallel", "arbitrary")),
        name="peer_dense",
    )(x, u, v, w, h, gain_next.astype(F32).reshape(1, d))


def _fox_layer(hn, b, t, w_in, b_f, q_gain, k_gain):
    d = hn.shape[1]
    w = w_in.astype(BF16)
    qkvg = fox_proj(hn, w[:, 0:4 * d], q_gain, k_gain, FOX_HEAD_DIM ** -0.5, FOX_HEAD_DIM)
    lf = mm_logsig(hn, w[:, 4 * d:], b_f, 1.0)
    c = time_cumsum(lf.reshape(b, t, FOX_HEADS))
    c_rows = c.transpose(0, 2, 1).reshape(b, FOX_HEADS // 2, 2, t)
    o = fox_attention(qkvg.reshape(b, t, 4 * d), c_rows)
    return o.reshape(b * t, d)


def _gla_layer(hn, b, t, w_in, w_up, b_alpha, out_gain):
    d = hn.shape[1]
    gk = GLA_HEADS * GLA_DK
    w = w_in.astype(BF16)
    q = mm_plain(hn, w[:, 0:gk], F32, scale=GLA_DK ** -0.5)
    k = mm_plain(hn, w[:, gk:2 * gk], F32)
    v = mm_plain(hn, w[:, 2 * gk:2 * gk + d], BF16)
    r = mm_plain(hn, w[:, 2 * gk + d:2 * gk + 2 * d], BF16)
    low = mm_plain(hn, w[:, 2 * gk + 2 * d:], BF16)
    log_a = mm_logsig(low, w_up.astype(BF16), b_alpha, 1.0 / GLA_TAU)
    o = gated_linear_attention(q.reshape(b, t, gk), k.reshape(b, t, gk), log_a.reshape(b, t, gk),
                               v.reshape(b, t, d), r.reshape(b, t, d), out_gain,
                               GLA_HEADS, GLA_DK, GLA_DV)
    return o.reshape(b * t, d)


def _hgrn_layer(hn, b, t, layer, w_in, lb_logits, out_gain):
    d = hn.shape[1]
    w = w_in.astype(BF16)
    q = mm_plain(hn, w[:, 0:d], F32, scale=HGRN_DK ** -0.5)
    k, log_g = mm_hgrn_gate(hn, w[:, d:2 * d], lb_logits, layer)
    v = mm_plain(hn, w[:, 2 * d:3 * d], BF16)
    r = mm_plain(hn, w[:, 3 * d:4 * d], BF16)
    shp = (b, t, d)
    o = gated_linear_attention(q.reshape(shp), k.reshape(shp), log_g.reshape(shp),
                               v.reshape(shp), r.reshape(shp), out_gain,
                               HGRN_HEADS, HGRN_DK, HGRN_DV)
    return o.reshape(b * t, d)


def _peer_layer(h, hn, w_q, sub_keys, u, v, gain_next):
    q = mm_plain(hn, w_q.astype(BF16), BF16)
    sk = sub_keys.astype(BF16).reshape(2 * PEER_HEADS, PEER_NKEYS, -1)
    w = peer_route_weights(q, sk)
    return peer_dense(hn, u.astype(BF16), v.astype(BF16), w, h, gain_next)


def kernel(x, norm_mix, norm_ffn, fox_w_in, fox_b_f, fox_q_gain, fox_k_gain, fox_w_out,
           gla_w_in, gla_w_up, gla_b_alpha, gla_out_gain, gla_w_out,
           hgrn_w_in, hgrn_lb_logits, hgrn_out_gain, hgrn_w_out,
           peer_w_q, peer_sub_keys, peer_u, peer_v):
    b, t, d = x.shape
    depth = norm_mix.shape[0]
    h = x.reshape(b * t, d)
    hn = rmsnorm(h, norm_mix[0])
    for i in range(depth):
        m, j = i % N_MIXERS, i // N_MIXERS
        if m == 0:
            o = _fox_layer(hn, b, t, fox_w_in[j], fox_b_f[j], fox_q_gain[j], fox_k_gain[j])
            w_out = fox_w_out[j]
        elif m == 1:
            o = _gla_layer(hn, b, t, gla_w_in[j], gla_w_up[j], gla_b_alpha[j], gla_out_gain[j])
            w_out = gla_w_out[j]
        else:
            o = _hgrn_layer(hn, b, t, i, hgrn_w_in[j], hgrn_lb_logits, hgrn_out_gain[j])
            w_out = hgrn_w_out[j]
        h, hn = mm_res_norm(o, w_out.astype(BF16), h, norm_ffn[i])
        gain_next = norm_mix[(i + 1) % depth]
        h, hn = _peer_layer(h, hn, peer_w_q[i], peer_sub_keys[i], peer_u[i], peer_v[i], gain_next)
    return h.reshape(b, t, d)
```

```python
import functools
import math

import numpy as np
import jax
import jax.numpy as jnp
from jax import lax
from jax.experimental import pallas as pl
from jax.experimental.pallas import tpu as pltpu

F32 = jnp.float32
BF16 = jnp.bfloat16
EPS = 1e-6

DEPTH = 4
N_MIXERS = 3
CHUNK = 64
FOX_HEADS = 16
FOX_HEAD_DIM = 64
GLA_HEADS = 4
GLA_DK = 128
GLA_DV = 256
GLA_TAU = 16.0
HGRN_HEADS = 8
HGRN_DK = 128
HGRN_DV = 128
PEER_HEADS = 8
PEER_NKEYS = 128
PEER_TOPK = 16
LANES = 128
VMEM_LIMIT = 56 * 1024 * 1024


def _cparams(sem):
    return pltpu.CompilerParams(dimension_semantics=sem, vmem_limit_bytes=VMEM_LIMIT)


def _dot(a, b):
    return jnp.dot(a, b, preferred_element_type=F32)


def _dot_nt(a, b):
    return lax.dot_general(a, b, (((1,), (1,)), ((), ())), preferred_element_type=F32)


def _dot_tn(a, b):
    return lax.dot_general(a, b, (((0,), (0,)), ((), ())), preferred_element_type=F32)


def _split2(x):
    hi = x.astype(BF16)
    lo = (x - hi.astype(F32)).astype(BF16)
    return hi, lo


def _split3(x):
    hi = x.astype(BF16)
    r = x - hi.astype(F32)
    mid = r.astype(BF16)
    lo = (r - mid.astype(F32)).astype(BF16)
    return hi, mid, lo


def _log_sigmoid(x):
    return jnp.minimum(x, 0.0) - jnp.log1p(jnp.exp(-jnp.abs(x)))


def _sigmoid(x):
    return 1.0 / (1.0 + jnp.exp(-x))


def _rms(x, gain):
    ms = jnp.mean(x * x, axis=-1, keepdims=True)
    return x * lax.rsqrt(ms + EPS) * gain


def _rmsnorm_body(x_ref, g_ref, o_ref):
    o_ref[...] = _rms(x_ref[...], g_ref[...]).astype(o_ref.dtype)


def rmsnorm(x, gain, tm=512):
    n, d = x.shape
    tm = min(tm, n)
    return pl.pallas_call(
        _rmsnorm_body,
        out_shape=jax.ShapeDtypeStruct((n, d), BF16),
        grid=(n // tm,),
        in_specs=[pl.BlockSpec((tm, d), lambda i: (i, 0)),
                  pl.BlockSpec((1, d), lambda i: (0, 0))],
        out_specs=pl.BlockSpec((tm, d), lambda i: (i, 0)),
        compiler_params=_cparams(("parallel",)),
        name="rmsnorm",
    )(x, gain.reshape(1, d))


def _mm_call(body, x, w, extras, extra_specs, out_dtypes, tm, tn, name):
    n, k = x.shape
    m = w.shape[1]
    tm = min(tm, n)
    tn = min(tn, m)
    outs = [jax.ShapeDtypeStruct((n, m), dt) for dt in out_dtypes]
    return pl.pallas_call(
        body,
        out_shape=outs,
        grid=(n // tm, m // tn),
        in_specs=[pl.BlockSpec((tm, k), lambda i, j: (i, 0)),
                  pl.BlockSpec((k, tn), lambda i, j: (0, j))] + extra_specs,
        out_specs=[pl.BlockSpec((tm, tn), lambda i, j: (i, j)) for _ in outs],
        compiler_params=_cparams(("parallel", "arbitrary")),
        name=name,
    )(x, w, *extras)


def _mm_plain_body(x_ref, w_ref, o_ref, *, scale):
    acc = _dot(x_ref[...], w_ref[...])
    if scale != 1.0:
        acc = acc * scale
    o_ref[...] = acc.astype(o_ref.dtype)


def mm_plain(x, w, out_dtype, scale=1.0, tm=1024, tn=1024):
    return _mm_call(functools.partial(_mm_plain_body, scale=scale), x, w, [], [],
                    [out_dtype], tm, tn, "mm_plain")[0]


MXU_DIM = 256


def _fox_proj_body(x_ref, w_ref, bd_ref, g_ref, o_ref, *, hd):
    j = pl.program_id(1)
    acc = _dot(x_ref[...], w_ref[...])

    @pl.when(j < 2)
    def _():
        for c in range(acc.shape[1] // MXU_DIM):
            cols = slice(c * MXU_DIM, (c + 1) * MXU_DIM)
            a = acc[:, cols]
            ms = _dot((a * a).astype(BF16), bd_ref[...]) * (1.0 / hd)
            o_ref[:, cols] = (a * lax.rsqrt(ms + EPS) * g_ref[0, :, cols]).astype(o_ref.dtype)

    @pl.when(j >= 2)
    def _():
        o_ref[...] = acc.astype(o_ref.dtype)


def fox_proj(x, w, q_gain, k_gain, scale, hd, tm=512):
    n, d = x.shape
    tm = min(tm, n)
    blk = np.arange(MXU_DIM) // hd
    bd = jnp.asarray(blk[:, None] == blk[None, :], BF16)
    g = jnp.stack([jnp.tile(q_gain.astype(F32) * scale, d // hd),
                   jnp.tile(k_gain.astype(F32), d // hd)]).reshape(2, 1, d)
    return pl.pallas_call(
        functools.partial(_fox_proj_body, hd=hd),
        out_shape=jax.ShapeDtypeStruct((n, 4 * d), BF16),
        grid=(n // tm, 4),
        in_specs=[pl.BlockSpec((tm, d), lambda i, j: (i, 0)),
                  pl.BlockSpec((d, d), lambda i, j: (0, j)),
                  pl.BlockSpec((MXU_DIM, MXU_DIM), lambda i, j: (0, 0)),
                  pl.BlockSpec((1, 1, d), lambda i, j: (jnp.minimum(j, 1), 0, 0))],
        out_specs=pl.BlockSpec((tm, d), lambda i, j: (i, j)),
        compiler_params=_cparams(("parallel", "arbitrary")),
        name="fox_proj",
    )(x, w, bd, g)


def _mm_logsig_body(x_ref, w_ref, b_ref, o_ref, *, mult):
    acc = _dot(x_ref[...], w_ref[...]) + b_ref[...]
    o_ref[...] = (_log_sigmoid(acc) * mult).astype(o_ref.dtype)


def mm_logsig(x, w, bias, mult, tm=512, tn=512):
    m = w.shape[1]
    tn = min(tn, m)
    return _mm_call(functools.partial(_mm_logsig_body, mult=mult), x, w,
                    [bias.astype(F32).reshape(1, m)],
                    [pl.BlockSpec((1, tn), lambda i, j: (0, j))],
                    [F32], tm, tn, "mm_logsig")[0]


def _mm_hgrn_gate_body(x_ref, w_ref, lbl_ref, k_ref, lg_ref, *, layer):
    acc = _dot(x_ref[...], w_ref[...])
    logits = lbl_ref[...]
    e = jnp.exp(logits - jnp.max(logits, axis=0, keepdims=True))
    p = e / jnp.sum(e, axis=0, keepdims=True)
    lb = jnp.zeros_like(p[0:1])
    for u in range(1, layer + 1):
        lb = lb + p[u:u + 1]
    gate = lb + (1.0 - lb) * _sigmoid(acc)
    k_ref[...] = (1.0 - gate).astype(k_ref.dtype)
    lg_ref[...] = jnp.log(gate).astype(lg_ref.dtype)


def mm_hgrn_gate(x, w, lb_logits, layer, tm=512, tn=512):
    m = w.shape[1]
    tn = min(tn, m)
    depth = lb_logits.shape[0]
    return _mm_call(functools.partial(_mm_hgrn_gate_body, layer=layer), x, w,
                    [lb_logits.astype(F32)],
                    [pl.BlockSpec((depth, tn), lambda i, j: (0, j))],
                    [F32, F32], tm, tn, "mm_hgrn_gate")


def _mm_res_norm_body(x_ref, w_ref, h_ref, g_ref, ho_ref, hn_ref):
    hnew = h_ref[...] + _dot(x_ref[...], w_ref[...])
    ho_ref[...] = hnew
    hn_ref[...] = _rms(hnew, g_ref[...]).astype(hn_ref.dtype)


def mm_res_norm(x, w, h, gain, tm=512):
    n, d = h.shape
    return _mm_call(_mm_res_norm_body, x, w, [h, gain.astype(F32).reshape(1, d)],
                    [pl.BlockSpec((min(tm, n), d), lambda i, j: (i, 0)),
                     pl.BlockSpec((1, d), lambda i, j: (0, 0))],
                    [F32, BF16], tm, d, "mm_res_norm")


CUM_BLOCK = 256


def _cumsum_body(x_ref, o_ref, *, t):
    nb = t // CUM_BLOCK
    r = lax.broadcasted_iota(jnp.int32, (CUM_BLOCK, CUM_BLOCK), 0)
    c = lax.broadcasted_iota(jnp.int32, (CUM_BLOCK, CUM_BLOCK), 1)
    tril = jnp.where(r >= c, 1.0, 0.0).astype(BF16)
    carry = jnp.zeros((1, x_ref.shape[-1]), F32)
    for b in range(nb):
        x = x_ref[0, b * CUM_BLOCK:(b + 1) * CUM_BLOCK, :]
        hi, mid, lo = _split3(x)
        cs = _dot(tril, hi) + _dot(tril, mid) + _dot(tril, lo) + carry
        o_ref[0, b * CUM_BLOCK:(b + 1) * CUM_BLOCK, :] = cs
        carry = cs[CUM_BLOCK - 1:CUM_BLOCK, :]


def time_cumsum(x):
    b, t, h = x.shape
    return pl.pallas_call(
        functools.partial(_cumsum_body, t=t),
        out_shape=jax.ShapeDtypeStruct((b, t, h), F32),
        grid=(b,),
        in_specs=[pl.BlockSpec((1, t, h), lambda i: (i, 0, 0))],
        out_specs=pl.BlockSpec((1, t, h), lambda i: (i, 0, 0)),
        compiler_params=_cparams(("parallel",)),
        name="time_cumsum",
    )(x)


def _fox_body(q_ref, k_ref, v_ref, g_ref, c_ref, o_ref, s_ref, m_ref, l_ref, acc_ref, *, tq):
    qi = pl.program_id(2)
    q = q_ref[0]
    lane = lax.broadcasted_iota(jnp.int32, (tq, LANES), 1)
    row = lax.broadcasted_iota(jnp.int32, (tq, tq), 0)
    col = lax.broadcasted_iota(jnp.int32, (tq, tq), 1)
    zero = jnp.zeros_like(q)
    qs = [jnp.where(lane < FOX_HEAD_DIM, q, zero), jnp.where(lane >= FOX_HEAD_DIM, q, zero)]
    nfold = tq // LANES

    def fold(x, op):
        r = x[:, 0:LANES]
        for f in range(1, nfold):
            r = op(r, x[:, f * LANES:(f + 1) * LANES])
        return r

    def scores(j, off):
        ks = k_ref[0, pl.ds(off, tq), :]
        return _dot_nt(qs[j], ks) - c_ref[0, 0, pl.ds(j, 1), pl.ds(off, tq)]

    m_ref[...] = jnp.full(m_ref.shape, -jnp.inf, F32)
    l_ref[...] = jnp.zeros(l_ref.shape, F32)
    acc_ref[...] = jnp.zeros(acc_ref.shape, F32)

    def pass1(kb, carry):
        off = pl.multiple_of(kb * tq, tq)
        for j in range(2):
            s = scores(j, off)
            s_ref[j, :, pl.ds(off, tq)] = s
            m_ref[j] = jnp.maximum(m_ref[j], fold(s, jnp.maximum))
        return carry

    lax.fori_loop(0, qi, pass1, 0)
    offd = pl.multiple_of(qi * tq, tq)
    sd, ms = [], []
    for j in range(2):
        s = jnp.where(row >= col, scores(j, offd), -jnp.inf)
        sd.append(s)
        ms.append(jnp.max(jnp.maximum(m_ref[j], fold(s, jnp.maximum)), axis=-1, keepdims=True))

    def pass2(kb, carry):
        off = pl.multiple_of(kb * tq, tq)
        vs = v_ref[0, pl.ds(off, tq), :]
        for j in range(2):
            p = jnp.exp(s_ref[j, :, pl.ds(off, tq)] - ms[j])
            l_ref[j] += fold(p, jnp.add)
            acc_ref[j] += _dot(p.astype(BF16), vs)
        return carry

    lax.fori_loop(0, qi, pass2, 0)
    vsd = v_ref[0, pl.ds(offd, tq), :]
    outs = []
    for j in range(2):
        p = jnp.exp(sd[j] - ms[j])
        l = jnp.sum(l_ref[j] + fold(p, jnp.add), axis=-1, keepdims=True)
        outs.append((acc_ref[j] + _dot(p.astype(BF16), vsd)) / l)
    o = jnp.where(lane < FOX_HEAD_DIM, outs[0], outs[1])
    o = o * _sigmoid(g_ref[0].astype(F32))
    o_ref[0] = o.astype(o_ref.dtype)


def fox_attention(qkvg, c_rows, tq=512):
    b, t, d4 = qkvg.shape
    d = d4 // 4
    tq = min(tq, t)
    npair = d // LANES
    qspec = pl.BlockSpec((1, tq, LANES), lambda bi, p, i: (bi, i, p))
    return pl.pallas_call(
        functools.partial(_fox_body, tq=tq),
        out_shape=jax.ShapeDtypeStruct((b, t, d), BF16),
        grid=(b, npair, t // tq),
        in_specs=[qspec,
                  pl.BlockSpec((1, t, LANES), lambda bi, p, i: (bi, 0, npair + p)),
                  pl.BlockSpec((1, t, LANES), lambda bi, p, i: (bi, 0, 2 * npair + p)),
                  pl.BlockSpec((1, tq, LANES), lambda bi, p, i: (bi, i, 3 * npair + p)),
                  pl.BlockSpec((1, 1, 2, t), lambda bi, p, i: (bi, p, 0, 0))],
        out_specs=qspec,
        scratch_shapes=[pltpu.VMEM((2, tq, t), F32), pltpu.VMEM((2, tq, LANES), F32),
                        pltpu.VMEM((2, tq, LANES), F32), pltpu.VMEM((2, tq, LANES), F32)],
        compiler_params=_cparams(("parallel", "parallel", "arbitrary")),
        name="fox_attention",
    )(qkvg, qkvg, qkvg, qkvg, c_rows)


_LEVELS = (64, 32, 16, 8, 4, 2)


def _gla_consts():
    c = CHUNK
    t = np.arange(c)
    u = t[None, :]
    blocks = [u <= t[:, None]]
    masks = [np.eye(c, dtype=bool)]
    for lv in _LEVELS:
        half = lv // 2
        blk, pos = t // lv, t % lv
        r = blk * lv + half - 1
        upper = pos >= half
        qrow = upper[:, None] & (u > r[:, None]) & (u <= t[:, None])
        krow = (~upper)[:, None] & (u > t[:, None]) & (u <= r[:, None])
        blocks.append(qrow | krow)
        masks.append((blk[:, None] == blk[None, :]) & upper[:, None] & (~upper)[None, :])
    blocks.append(u > t[:, None])
    mall = np.concatenate(blocks, axis=0).astype(np.float32)
    mask = np.stack(masks).astype(np.float32)
    return jnp.asarray(mall, BF16), jnp.asarray(mask, F32)


def _gla_body(q_ref, k_ref, g_ref, v_ref, r_ref, gain_ref, mall_ref, mask_ref,
              o_ref, st_ref, *, nchunk, hg, dk, dv):
    c = CHUNK
    nl = len(_LEVELS)

    @pl.when(pl.program_id(2) == 0)
    def _():
        st_ref[...] = jnp.zeros_like(st_ref)

    def chunk(ci, carry):
        sl = pl.ds(pl.multiple_of(ci * c, c), c)
        g_hi, g_lo = _split2(g_ref[0, sl, :])
        mall = mall_ref[...]
        xall = jnp.exp(_dot(mall, g_hi) + _dot(mall, g_lo))
        for hh in range(hg):
            ks, vs = slice(hh * dk, (hh + 1) * dk), slice(hh * dv, (hh + 1) * dv)
            q = q_ref[0, sl, ks]
            k = k_ref[0, sl, ks]
            v = v_ref[0, sl, vs]
            x = xall[:, ks]
            scores = mask_ref[0] * _dot_nt(q.astype(BF16), k.astype(BF16))
            for lv in range(nl):
                z = x[(1 + lv) * c:(2 + lv) * c]
                scores = scores + mask_ref[1 + lv] * _dot_nt((q * z).astype(BF16), (k * z).astype(BF16))
            st = st_ref[hh]
            o = _dot(scores.astype(BF16), v) + _dot_nt((q * x[0:c]).astype(BF16), st.astype(BF16))
            kt = (k * x[(1 + nl) * c:(2 + nl) * c]).astype(BF16)
            st_ref[hh] = st * x[c - 1:c, :] + _dot_tn(v, kt)
            r = r_ref[0, sl, vs].astype(F32)
            y = _rms(o, gain_ref[hh]) * (r * _sigmoid(r))
            o_ref[0, sl, vs] = y.astype(o_ref.dtype)
        return carry

    lax.fori_loop(0, nchunk, chunk, 0, unroll=2)


def gated_linear_attention(q, k, g, v, r, gain, heads, dk, dv, hg=4, ts=256):
    b, t, _ = q.shape
    ts = min(ts, t)
    mall, mask = _gla_consts()
    qspec = pl.BlockSpec((1, ts, hg * dk), lambda bi, h, i: (bi, i, h))
    vspec = pl.BlockSpec((1, ts, hg * dv), lambda bi, h, i: (bi, i, h))
    return pl.pallas_call(
        functools.partial(_gla_body, nchunk=ts // CHUNK, hg=hg, dk=dk, dv=dv),
        out_shape=jax.ShapeDtypeStruct((b, t, heads * dv), BF16),
        grid=(b, heads // hg, t // ts),
        in_specs=[qspec, qspec, qspec, vspec, vspec,
                  pl.BlockSpec((hg, 1, dv), lambda bi, h, i: (h, 0, 0)),
                  pl.BlockSpec(mall.shape, lambda bi, h, i: (0, 0)),
                  pl.BlockSpec(mask.shape, lambda bi, h, i: (0, 0, 0))],
        out_specs=vspec,
        scratch_shapes=[pltpu.VMEM((hg, dv, dk), F32)],
        compiler_params=_cparams(("parallel", "parallel", "arbitrary")),
        name="gated_linear_attention",
    )(q, k, g, v, r, gain.astype(F32).reshape(heads, 1, dv), mall, mask)


def _peer_cand_layout():
    k = PEER_TOPK
    ab = [(0, b) for b in range(16)] + [(1, b) for b in range(8)]
    ab += [(2, b) if b < 5 else None for b in range(8)]
    ab += [(3, 0), (3, 1), (3, 2), (3, 3), (4, 0), (4, 1), (4, 2), None]
    ab += [(5, 0), (5, 1), (6, 0), (6, 1), (7, 0), (7, 1), None, None]
    ab += [(a, 0) for a in range(8, 16)]
    assert all(p is None or (p[0] + 1) * (p[1] + 1) <= k for p in ab)
    assert sum(p is not None for p in ab) == sum(k // (a + 1) for a in range(k))
    pos = np.array([PEER_POS_INVALID if p is None else p[0] * k + p[1] for p in ab], np.float32)
    return np.broadcast_to(pos[:, None], (pos.size, LANES)).copy()


PEER_POS_INVALID = 1024.0


def _extract_max(s, ids, big):
    m = jnp.max(s, axis=0, keepdims=True)
    ix = jnp.min(jnp.where(s == m, ids, big), axis=0, keepdims=True)
    return m, ix


def _topk_rows(s, ids, big, k):
    vals, sel = [], []
    for _ in range(k):
        m, ix = _extract_max(s, ids, big)
        vals.append(m)
        sel.append(ix)
        s = jnp.where(ids == ix, -jnp.inf, s)
    return jnp.concatenate(vals, axis=0), jnp.concatenate(sel, axis=0)


def _topk_keys(st, k):
    n = st.shape[0]
    a = jnp.concatenate([st[r:r + 8] for r in range(0, n, 16)], axis=0)
    b = jnp.concatenate([st[r + 8:r + 16] for r in range(0, n, 16)], axis=0)
    r = lax.broadcasted_iota(jnp.int32, a.shape, 0)
    ida = (((r >> 3) << 4) + (r & 7)).astype(F32)
    idb = ida + 8.0
    a_wins = a >= b
    win, idw = jnp.maximum(a, b), jnp.where(a_wins, ida, idb)
    los, idl = jnp.minimum(a, b), jnp.where(a_wins, idb, ida)
    vals, sel = [], []
    for _ in range(k):
        m, ix = _extract_max(win, idw, float(n))
        vals.append(m)
        sel.append(ix)
        hit = idw == ix
        win = jnp.where(hit, los, win)
        idw = jnp.where(hit, idl, idw)
        los = jnp.where(hit, -jnp.inf, los)
    return jnp.concatenate(vals, axis=0), jnp.concatenate(sel, axis=0)


def _route_head(q_ref, sk_ref, pos, h, tm):
    k = PEER_TOPK
    nk = PEER_NKEYS
    sub = lax.broadcasted_iota(jnp.int32, (8, tm), 0)

    def bc(x, r):
        return jnp.broadcast_to(x[r:r + 1], (8, tm))

    tops = []
    for p in range(2):
        hp = 2 * h + p
        qh = q_ref[:, pl.ds(pl.multiple_of(hp * nk, nk), nk)]
        st = _dot_nt(sk_ref[hp], qh)
        tops.append(_topk_keys(st, k))
    (s0, i0), (s1, i1) = tops
    lo1 = s1[0:8]
    cand = jnp.concatenate([
        bc(s0, 0) + lo1,
        bc(s0, 0) + s1[8:16],
        bc(s0, 1) + lo1,
        bc(s0, 2) + lo1,
        jnp.where(sub < 4, bc(s0, 3), bc(s0, 4)) + jnp.where(sub < 4, lo1, pltpu.roll(lo1, 4, 0)),
        jnp.where(sub < 2, bc(s0, 5), jnp.where(sub < 4, bc(s0, 6), bc(s0, 7)))
        + jnp.where((sub & 1) == 0, bc(s1, 0), bc(s1, 1)),
        s0[8:16] + bc(s1, 0)], axis=0)
    cand = jnp.where(pos < PEER_POS_INVALID, cand, -jnp.inf)
    best, bpos = _topk_rows(cand, pos, 2.0 * PEER_POS_INVALID, k)
    bpos = bpos.astype(jnp.int32)
    ra = bpos >> 4
    rb = bpos & (k - 1)
    isel = jnp.zeros((k, tm), F32)
    jsel = jnp.zeros((k, tm), F32)
    for a in range(k):
        isel = jnp.where(ra == a, i0[a:a + 1], isel)
        jsel = jnp.where(rb == a, i1[a:a + 1], jsel)
    e = jnp.exp(best - best[0:1])
    gates = e / jnp.sum(e, axis=0, keepdims=True)
    return isel, jsel, gates


W3_GROUP = 8
W3_BATCH = 2 * W3_GROUP
ROUTE_HEADS_PER_TRIP = 2


def _peer_route_weights_body(q_ref, sk_ref, pos_ref, o_ref, ri_ref, rj_ref, rg_ref, pt_ref,
                             w_ref, *, tm):
    k = PEER_TOPK
    nk = PEER_NKEYS
    ngroup = tm // LANES
    nbatch = tm // W3_BATCH
    per_head = nbatch // PEER_HEADS

    @pl.when(pl.program_id(0) == 0)
    def _():
        ri_ref[...] = jnp.zeros_like(ri_ref)
        rj_ref[...] = jnp.zeros_like(rj_ref)
        rg_ref[...] = jnp.zeros_like(rg_ref)

    for a, r_ref in enumerate((ri_ref, rj_ref, rg_ref)):
        for c in range(ngroup):
            pt_ref[a, c * LANES:(c + 1) * LANES, :] = r_ref[:, c * LANES:(c + 1) * LANES].T
    pos = jnp.concatenate([pos_ref[...]] * ngroup, axis=1)
    sub = lax.broadcasted_iota(jnp.int32, (nk, LANES), 0).astype(F32)

    def build(t):
        slot = lax.rem(t, 2 * ROUTE_HEADS_PER_TRIP * per_head)
        for half in range(2):
            tok0 = pl.multiple_of(t * W3_BATCH + half * W3_GROUP, W3_GROUP)
            it, jt, gt = (pt_ref[a, pl.ds(tok0, W3_GROUP), :] for a in range(3))
            for s in range(W3_GROUP):
                at = jnp.where(sub == it[s:s + 1], gt[s:s + 1], 0.0).astype(BF16)
                bt = jnp.where(sub == jt[s:s + 1], 1.0, 0.0).astype(BF16)
                w_ref[2 * slot + half, pl.ds(s, nk, stride=W3_GROUP), :] = _dot_nt(at, bt)

    def flush(t):
        slot = lax.rem(t, 2 * ROUTE_HEADS_PER_TRIP * per_head)
        r0 = pl.multiple_of(t * W3_BATCH, W3_BATCH)
        for i in range(nk):
            rows = jnp.concatenate([w_ref[2 * slot, i * W3_GROUP:(i + 1) * W3_GROUP, :],
                                    w_ref[2 * slot + 1, i * W3_GROUP:(i + 1) * W3_GROUP, :]], axis=0)
            o_ref[pl.ds(r0, W3_BATCH), i * nk:(i + 1) * nk] = rows.astype(o_ref.dtype)

    hpt = ROUTE_HEADS_PER_TRIP

    def do_heads(u, flush_prev):
        if flush_prev:
            for t in range(hpt * per_head):
                flush((u - 1) * hpt * per_head + t)
        for hh in range(hpt):
            h = u * hpt + hh
            isel, jsel, gates = _route_head(q_ref, sk_ref, pos, h, tm)
            rows = pl.ds(pl.multiple_of(h * k, k), k)
            ri_ref[rows, :] = isel
            rj_ref[rows, :] = jsel
            rg_ref[rows, :] = gates
            for t in range(per_head):
                build(h * per_head + t)

    def trip(u, carry):
        do_heads(u, True)
        return carry

    ntrip = PEER_HEADS // hpt
    do_heads(jnp.int32(0), False)
    lax.fori_loop(1, ntrip, trip, 0)
    for t in range(hpt * per_head):
        flush(jnp.int32((ntrip - 1) * hpt * per_head + t))


def peer_route_weights(q, sub_keys, tm=256):
    n = q.shape[0]
    tm = min(tm, n)
    nblk = n // tm
    ne = PEER_NKEYS * PEER_NKEYS
    npair = PEER_HEADS * PEER_TOPK
    pos = jnp.asarray(_peer_cand_layout())
    return pl.pallas_call(
        functools.partial(_peer_route_weights_body, tm=tm),
        out_shape=jax.ShapeDtypeStruct((n, ne), BF16),
        grid=(nblk + 1,),
        in_specs=[pl.BlockSpec((tm, q.shape[1]), lambda s: (jnp.minimum(s, nblk - 1), 0)),
                  pl.BlockSpec(sub_keys.shape, lambda s: (0, 0, 0)),
                  pl.BlockSpec(pos.shape, lambda s: (0, 0))],
        out_specs=pl.BlockSpec((tm, ne), lambda s: (jnp.maximum(s - 1, 0), 0)),
        scratch_shapes=[pltpu.VMEM((npair, tm), F32)] * 3 + [pltpu.VMEM((3, tm, npair), F32)]
        + [pltpu.VMEM((4 * ROUTE_HEADS_PER_TRIP * tm // (W3_BATCH * PEER_HEADS),
                       W3_GROUP * PEER_NKEYS, PEER_NKEYS), F32)],
        compiler_params=_cparams(("arbitrary",)),
        name="peer_route_weights",
    )(q, sub_keys, pos)


def _gelu(x):
    return 0.5 * x * (1.0 + lax.erf(x * (1.0 / math.sqrt(2.0))))


def _peer_dense_body(x_ref, u_ref, v_ref, w_ref, h_ref, gn_ref, ho_ref, hn_ref, acc_ref, *, nj):
    j = pl.program_id(1)

    @pl.when(j == 0)
    def _():
        acc_ref[...] = jnp.zeros_like(acc_ref)

    hid = _dot_nt(x_ref[...], u_ref[...])
    a = (_gelu(hid) * w_ref[...].astype(F32)).astype(BF16)
    acc_ref[...] += _dot(a, v_ref[...])

    @pl.when(j == nj - 1)
    def _():
        hnew = h_ref[...] + acc_ref[...]
        ho_ref[...] = hnew
        hn_ref[...] = _rms(hnew, gn_ref[...]).astype(hn_ref.dtype)


def peer_dense(x, u, v, layer, w, h, gain_next, tm=1024, te=1024):
    n, d = x.shape
    ne = u.shape[1]
    tm = min(tm, n)
    te = min(te, ne)
    nj = ne // te
    rowspec = pl.BlockSpec((tm, d), lambda i, j: (i, 0))
    rowspec_in = pl.BlockSpec((tm, d), lambda i, j: (i, 0), pipeline_mode=pl.Buffered(1))
    return pl.pallas_call(
        functools.partial(_peer_dense_body, nj=nj),
        out_shape=[jax.ShapeDtypeStruct((n, d), F32), jax.ShapeDtypeStruct((n, d), BF16)],
        grid=(n // tm, nj),
        in_specs=[rowspec_in,
                  pl.BlockSpec((None, te, d), lambda i, j: (layer, j, 0)),
                  pl.BlockSpec((None, te, d), lambda i, j: (layer, j, 0)),
                  pl.BlockSpec((tm, te), lambda i, j: (i, j)),
                  rowspec_in,
                  pl.BlockSpec((1, d), lambda i, j: (0, 0))],
        out_specs=[rowspec, rowspec],
        scratch_shapes=[pltpu.VMEM((tm, d), F32)],
        compiler_params=_cparams(("parallel", "arbitrary")),
        name="peer_dense",
    )(x, u, v, w, h, gain_next.astype(F32).reshape(1, d))


def _fox_layer(hn, b, t, w_in, b_f, q_gain, k_gain):
    d = hn.shape[1]
    w = w_in.astype(BF16)
    qkvg = fox_proj(hn, w[:, 0:4 * d], q_gain, k_gain, FOX_HEAD_DIM ** -0.5, FOX_HEAD_DIM)
    lf = mm_logsig(hn, w[:, 4 * d:], b_f, 1.0)
    c = time_cumsum(lf.reshape(b, t, FOX_HEADS))
    c_rows = c.transpose(0, 2, 1).reshape(b, FOX_HEADS // 2, 2, t)
    o = fox_attention(qkvg.reshape(b, t, 4 * d), c_rows)
    return o.reshape(b * t, d)


def _gla_layer(hn, b, t, w_in, w_up, b_alpha, out_gain):
    d = hn.shape[1]
    gk = GLA_HEADS * GLA_DK
    w = w_in.astype(BF16)
    q = mm_plain(hn, w[:, 0:gk], F32, scale=GLA_DK ** -0.5)
    k = mm_plain(hn, w[:, gk:2 * gk], F32)
    v = mm_plain(hn, w[:, 2 * gk:2 * gk + d], BF16)
    r = mm_plain(hn, w[:, 2 * gk + d:2 * gk + 2 * d], BF16)
    low = mm_plain(hn, w[:, 2 * gk + 2 * d:], BF16)
    log_a = mm_logsig(low, w_up.astype(BF16), b_alpha, 1.0 / GLA_TAU)
    o = gated_linear_attention(q.reshape(b, t, gk), k.reshape(b, t, gk), log_a.reshape(b, t, gk),
                               v.reshape(b, t, d), r.reshape(b, t, d), out_gain,
                               GLA_HEADS, GLA_DK, GLA_DV)
    return o.reshape(b * t, d)


def _hgrn_layer(hn, b, t, layer, w_in, lb_logits, out_gain):
    d = hn.shape[1]
    w = w_in.astype(BF16)
    q = mm_plain(hn, w[:, 0:d], F32, scale=HGRN_DK ** -0.5)
    k, log_g = mm_hgrn_gate(hn, w[:, d:2 * d], lb_logits, layer)
    v = mm_plain(hn, w[:, 2 * d:3 * d], BF16)
    r = mm_plain(hn, w[:, 3 * d:4 * d], BF16)
    shp = (b, t, d)
    o = gated_linear_attention(q.reshape(shp), k.reshape(shp), log_g.reshape(shp),
                               v.reshape(shp), r.reshape(shp), out_gain,
                               HGRN_HEADS, HGRN_DK, HGRN_DV)
    return o.reshape(b * t, d)


def _peer_layer(h, hn, w_q, sub_keys, u, v, layer, gain_next):
    q = mm_plain(hn, w_q.astype(BF16), BF16)
    sk = sub_keys.astype(BF16).reshape(2 * PEER_HEADS, PEER_NKEYS, -1)
    w = peer_route_weights(q, sk)
    return peer_dense(hn, u, v, layer, w, h, gain_next)


def kernel(x, norm_mix, norm_ffn, fox_w_in, fox_b_f, fox_q_gain, fox_k_gain, fox_w_out,
           gla_w_in, gla_w_up, gla_b_alpha, gla_out_gain, gla_w_out,
           hgrn_w_in, hgrn_lb_logits, hgrn_out_gain, hgrn_w_out,
           peer_w_q, peer_sub_keys, peer_u, peer_v):
    b, t, d = x.shape
    depth = norm_mix.shape[0]
    h = x.reshape(b * t, d)
    hn = rmsnorm(h, norm_mix[0])
    u_all, v_all = peer_u.astype(BF16), peer_v.astype(BF16)
    for i in range(depth):
        m, j = i % N_MIXERS, i // N_MIXERS
        if m == 0:
            o = _fox_layer(hn, b, t, fox_w_in[j], fox_b_f[j], fox_q_gain[j], fox_k_gain[j])
            w_out = fox_w_out[j]
        elif m == 1:
            o = _gla_layer(hn, b, t, gla_w_in[j], gla_w_up[j], gla_b_alpha[j], gla_out_gain[j])
            w_out = gla_w_out[j]
        else:
            o = _hgrn_layer(hn, b, t, i, hgrn_w_in[j], hgrn_lb_logits, hgrn_out_gain[j])
            w_out = hgrn_w_out[j]
        h, hn = mm_res_norm(o, w_out.astype(BF16), h, norm_ffn[i])
        gain_next = norm_mix[(i + 1) % depth]
        h, hn = _peer_layer(h, hn, peer_w_q[i], peer_sub_keys[i], u_all, v_all, i, gain_next)
    return h.reshape(b, t, d)
```

```python
import functools
import math

import numpy as np
import jax
import jax.numpy as jnp
from jax import lax
from jax.experimental import pallas as pl
from jax.experimental.pallas import tpu as pltpu

F32 = jnp.float32
BF16 = jnp.bfloat16
EPS = 1e-6

DEPTH = 4
N_MIXERS = 3
CHUNK = 64
FOX_HEADS = 16
FOX_HEAD_DIM = 64
GLA_HEADS = 4
GLA_DK = 128
GLA_DV = 256
GLA_TAU = 16.0
HGRN_HEADS = 8
HGRN_DK = 128
HGRN_DV = 128
PEER_HEADS = 8
PEER_NKEYS = 128
PEER_TOPK = 16
LANES = 128
VMEM_LIMIT = 56 * 1024 * 1024


def _cparams(sem):
    return pltpu.CompilerParams(dimension_semantics=sem, vmem_limit_bytes=VMEM_LIMIT)


def _dot(a, b):
    return jnp.dot(a, b, preferred_element_type=F32)


def _dot_nt(a, b):
    return lax.dot_general(a, b, (((1,), (1,)), ((), ())), preferred_element_type=F32)


def _dot_tn(a, b):
    return lax.dot_general(a, b, (((0,), (0,)), ((), ())), preferred_element_type=F32)


def _split2(x):
    hi = x.astype(BF16)
    lo = (x - hi.astype(F32)).astype(BF16)
    return hi, lo


def _split3(x):
    hi = x.astype(BF16)
    r = x - hi.astype(F32)
    mid = r.astype(BF16)
    lo = (r - mid.astype(F32)).astype(BF16)
    return hi, mid, lo


def _log_sigmoid(x):
    return jnp.minimum(x, 0.0) - jnp.log1p(jnp.exp(-jnp.abs(x)))


def _sigmoid(x):
    return 1.0 / (1.0 + jnp.exp(-x))


def _rms(x, gain):
    ms = jnp.mean(x * x, axis=-1, keepdims=True)
    return x * lax.rsqrt(ms + EPS) * gain


def _rmsnorm_body(x_ref, g_ref, o_ref):
    o_ref[...] = _rms(x_ref[...], g_ref[...]).astype(o_ref.dtype)


def rmsnorm(x, gain, tm=512):
    n, d = x.shape
    tm = min(tm, n)
    return pl.pallas_call(
        _rmsnorm_body,
        out_shape=jax.ShapeDtypeStruct((n, d), BF16),
        grid=(n // tm,),
        in_specs=[pl.BlockSpec((tm, d), lambda i: (i, 0)),
                  pl.BlockSpec((1, d), lambda i: (0, 0))],
        out_specs=pl.BlockSpec((tm, d), lambda i: (i, 0)),
        compiler_params=_cparams(("parallel",)),
        name="rmsnorm",
    )(x, gain.reshape(1, d))


def _mm_call(body, x, w, extras, extra_specs, out_dtypes, tm, tn, name):
    n, k = x.shape
    m = w.shape[1]
    tm = min(tm, n)
    tn = min(tn, m)
    outs = [jax.ShapeDtypeStruct((n, m), dt) for dt in out_dtypes]
    return pl.pallas_call(
        body,
        out_shape=outs,
        grid=(n // tm, m // tn),
        in_specs=[pl.BlockSpec((tm, k), lambda i, j: (i, 0)),
                  pl.BlockSpec((k, tn), lambda i, j: (0, j))] + extra_specs,
        out_specs=[pl.BlockSpec((tm, tn), lambda i, j: (i, j)) for _ in outs],
        compiler_params=_cparams(("parallel", "arbitrary")),
        name=name,
    )(x, w, *extras)


def _mm_plain_body(x_ref, w_ref, o_ref, *, scale):
    acc = _dot(x_ref[...], w_ref[...])
    if scale != 1.0:
        acc = acc * scale
    o_ref[...] = acc.astype(o_ref.dtype)


def mm_plain(x, w, out_dtype, scale=1.0, tm=1024, tn=1024):
    return _mm_call(functools.partial(_mm_plain_body, scale=scale), x, w, [], [],
                    [out_dtype], tm, tn, "mm_plain")[0]


MXU_DIM = 256


def _fox_proj_body(x_ref, w_ref, bd_ref, g_ref, o_ref, *, hd):
    j = pl.program_id(1)
    acc = _dot(x_ref[...], w_ref[...])

    @pl.when(j < 2)
    def _():
        for c in range(acc.shape[1] // MXU_DIM):
            cols = slice(c * MXU_DIM, (c + 1) * MXU_DIM)
            a = acc[:, cols]
            ms = _dot((a * a).astype(BF16), bd_ref[...]) * (1.0 / hd)
            o_ref[:, cols] = (a * lax.rsqrt(ms + EPS) * g_ref[0, :, cols]).astype(o_ref.dtype)

    @pl.when(j >= 2)
    def _():
        o_ref[...] = acc.astype(o_ref.dtype)


def fox_proj(x, w, q_gain, k_gain, scale, hd, tm=512):
    n, d = x.shape
    tm = min(tm, n)
    blk = np.arange(MXU_DIM) // hd
    bd = jnp.asarray(blk[:, None] == blk[None, :], BF16)
    g = jnp.stack([jnp.tile(q_gain.astype(F32) * scale, d // hd),
                   jnp.tile(k_gain.astype(F32), d // hd)]).reshape(2, 1, d)
    return pl.pallas_call(
        functools.partial(_fox_proj_body, hd=hd),
        out_shape=jax.ShapeDtypeStruct((n, 4 * d), BF16),
        grid=(n // tm, 4),
        in_specs=[pl.BlockSpec((tm, d), lambda i, j: (i, 0)),
                  pl.BlockSpec((d, d), lambda i, j: (0, j)),
                  pl.BlockSpec((MXU_DIM, MXU_DIM), lambda i, j: (0, 0)),
                  pl.BlockSpec((1, 1, d), lambda i, j: (jnp.minimum(j, 1), 0, 0))],
        out_specs=pl.BlockSpec((tm, d), lambda i, j: (i, j)),
        compiler_params=_cparams(("parallel", "arbitrary")),
        name="fox_proj",
    )(x, w, bd, g)


def _mm_logsig_body(x_ref, w_ref, b_ref, o_ref, *, mult):
    acc = _dot(x_ref[...], w_ref[...]) + b_ref[...]
    o_ref[...] = (_log_sigmoid(acc) * mult).astype(o_ref.dtype)


def mm_logsig(x, w, bias, mult, tm=512, tn=512):
    m = w.shape[1]
    tn = min(tn, m)
    return _mm_call(functools.partial(_mm_logsig_body, mult=mult), x, w,
                    [bias.astype(F32).reshape(1, m)],
                    [pl.BlockSpec((1, tn), lambda i, j: (0, j))],
                    [F32], tm, tn, "mm_logsig")[0]


def _mm_hgrn_gate_body(x_ref, w_ref, lbl_ref, k_ref, lg_ref, *, layer):
    acc = _dot(x_ref[...], w_ref[...])
    logits = lbl_ref[...]
    e = jnp.exp(logits - jnp.max(logits, axis=0, keepdims=True))
    p = e / jnp.sum(e, axis=0, keepdims=True)
    lb = jnp.zeros_like(p[0:1])
    for u in range(1, layer + 1):
        lb = lb + p[u:u + 1]
    gate = lb + (1.0 - lb) * _sigmoid(acc)
    k_ref[...] = (1.0 - gate).astype(k_ref.dtype)
    lg_ref[...] = jnp.log(gate).astype(lg_ref.dtype)


def mm_hgrn_gate(x, w, lb_logits, layer, tm=512, tn=512):
    m = w.shape[1]
    tn = min(tn, m)
    depth = lb_logits.shape[0]
    return _mm_call(functools.partial(_mm_hgrn_gate_body, layer=layer), x, w,
                    [lb_logits.astype(F32)],
                    [pl.BlockSpec((depth, tn), lambda i, j: (0, j))],
                    [F32, F32], tm, tn, "mm_hgrn_gate")


def _mm_res_norm_body(x_ref, w_ref, h_ref, g_ref, ho_ref, hn_ref):
    hnew = h_ref[...] + _dot(x_ref[...], w_ref[...])
    ho_ref[...] = hnew
    hn_ref[...] = _rms(hnew, g_ref[...]).astype(hn_ref.dtype)


def mm_res_norm(x, w, h, gain, tm=512):
    n, d = h.shape
    return _mm_call(_mm_res_norm_body, x, w, [h, gain.astype(F32).reshape(1, d)],
                    [pl.BlockSpec((min(tm, n), d), lambda i, j: (i, 0)),
                     pl.BlockSpec((1, d), lambda i, j: (0, 0))],
                    [F32, BF16], tm, d, "mm_res_norm")


CUM_BLOCK = 256


def _cumsum_body(x_ref, o_ref, *, t):
    nb = t // CUM_BLOCK
    r = lax.broadcasted_iota(jnp.int32, (CUM_BLOCK, CUM_BLOCK), 0)
    c = lax.broadcasted_iota(jnp.int32, (CUM_BLOCK, CUM_BLOCK), 1)
    tril = jnp.where(r >= c, 1.0, 0.0).astype(BF16)
    carry = jnp.zeros((1, x_ref.shape[-1]), F32)
    for b in range(nb):
        x = x_ref[0, b * CUM_BLOCK:(b + 1) * CUM_BLOCK, :]
        hi, mid, lo = _split3(x)
        cs = _dot(tril, hi) + _dot(tril, mid) + _dot(tril, lo) + carry
        o_ref[0, b * CUM_BLOCK:(b + 1) * CUM_BLOCK, :] = cs
        carry = cs[CUM_BLOCK - 1:CUM_BLOCK, :]


def time_cumsum(x):
    b, t, h = x.shape
    return pl.pallas_call(
        functools.partial(_cumsum_body, t=t),
        out_shape=jax.ShapeDtypeStruct((b, t, h), F32),
        grid=(b,),
        in_specs=[pl.BlockSpec((1, t, h), lambda i: (i, 0, 0))],
        out_specs=pl.BlockSpec((1, t, h), lambda i: (i, 0, 0)),
        compiler_params=_cparams(("parallel",)),
        name="time_cumsum",
    )(x)


def _fox_body(q_ref, k_ref, v_ref, g_ref, c_ref, o_ref, s_ref, m_ref, l_ref, acc_ref, *, tq):
    qi = pl.program_id(2)
    q = q_ref[0]
    lane = lax.broadcasted_iota(jnp.int32, (tq, LANES), 1)
    row = lax.broadcasted_iota(jnp.int32, (tq, tq), 0)
    col = lax.broadcasted_iota(jnp.int32, (tq, tq), 1)
    zero = jnp.zeros_like(q)
    qs = [jnp.where(lane < FOX_HEAD_DIM, q, zero), jnp.where(lane >= FOX_HEAD_DIM, q, zero)]
    nfold = tq // LANES

    def fold(x, op):
        r = x[:, 0:LANES]
        for f in range(1, nfold):
            r = op(r, x[:, f * LANES:(f + 1) * LANES])
        return r

    def scores(j, off):
        ks = k_ref[0, pl.ds(off, tq), :]
        return _dot_nt(qs[j], ks) - c_ref[0, 0, pl.ds(j, 1), pl.ds(off, tq)]

    m_ref[...] = jnp.full(m_ref.shape, -jnp.inf, F32)
    l_ref[...] = jnp.zeros(l_ref.shape, F32)
    acc_ref[...] = jnp.zeros(acc_ref.shape, F32)

    def pass1(kb, carry):
        off = pl.multiple_of(kb * tq, tq)
        for j in range(2):
            s = scores(j, off)
            s_ref[j, :, pl.ds(off, tq)] = s
            m_ref[j] = jnp.maximum(m_ref[j], fold(s, jnp.maximum))
        return carry

    lax.fori_loop(0, qi, pass1, 0)
    offd = pl.multiple_of(qi * tq, tq)
    sd, ms = [], []
    for j in range(2):
        s = jnp.where(row >= col, scores(j, offd), -jnp.inf)
        sd.append(s)
        ms.append(jnp.max(jnp.maximum(m_ref[j], fold(s, jnp.maximum)), axis=-1, keepdims=True))

    def pass2(kb, carry):
        off = pl.multiple_of(kb * tq, tq)
        vs = v_ref[0, pl.ds(off, tq), :]
        for j in range(2):
            p = jnp.exp(s_ref[j, :, pl.ds(off, tq)] - ms[j])
            l_ref[j] += fold(p, jnp.add)
            acc_ref[j] += _dot(p.astype(BF16), vs)
        return carry

    lax.fori_loop(0, qi, pass2, 0)
    vsd = v_ref[0, pl.ds(offd, tq), :]
    outs = []
    for j in range(2):
        p = jnp.exp(sd[j] - ms[j])
        l = jnp.sum(l_ref[j] + fold(p, jnp.add), axis=-1, keepdims=True)
        outs.append((acc_ref[j] + _dot(p.astype(BF16), vsd)) / l)
    o = jnp.where(lane < FOX_HEAD_DIM, outs[0], outs[1])
    o = o * _sigmoid(g_ref[0].astype(F32))
    o_ref[0] = o.astype(o_ref.dtype)


def fox_attention(qkvg, c_rows, tq=512):
    b, t, d4 = qkvg.shape
    d = d4 // 4
    tq = min(tq, t)
    npair = d // LANES
    qspec = pl.BlockSpec((1, tq, LANES), lambda bi, p, i: (bi, i, p))
    return pl.pallas_call(
        functools.partial(_fox_body, tq=tq),
        out_shape=jax.ShapeDtypeStruct((b, t, d), BF16),
        grid=(b, npair, t // tq),
        in_specs=[qspec,
                  pl.BlockSpec((1, t, LANES), lambda bi, p, i: (bi, 0, npair + p)),
                  pl.BlockSpec((1, t, LANES), lambda bi, p, i: (bi, 0, 2 * npair + p)),
                  pl.BlockSpec((1, tq, LANES), lambda bi, p, i: (bi, i, 3 * npair + p)),
                  pl.BlockSpec((1, 1, 2, t), lambda bi, p, i: (bi, p, 0, 0))],
        out_specs=qspec,
        scratch_shapes=[pltpu.VMEM((2, tq, t), F32), pltpu.VMEM((2, tq, LANES), F32),
                        pltpu.VMEM((2, tq, LANES), F32), pltpu.VMEM((2, tq, LANES), F32)],
        compiler_params=_cparams(("parallel", "parallel", "arbitrary")),
        name="fox_attention",
    )(qkvg, qkvg, qkvg, qkvg, c_rows)


_LEVELS = (64, 32, 16, 8, 4, 2)


def _gla_consts():
    c = CHUNK
    t = np.arange(c)
    u = t[None, :]
    blocks = [u <= t[:, None]]
    masks = [np.eye(c, dtype=bool)]
    for lv in _LEVELS:
        half = lv // 2
        blk, pos = t // lv, t % lv
        r = blk * lv + half - 1
        upper = pos >= half
        qrow = upper[:, None] & (u > r[:, None]) & (u <= t[:, None])
        krow = (~upper)[:, None] & (u > t[:, None]) & (u <= r[:, None])
        blocks.append(qrow | krow)
        masks.append((blk[:, None] == blk[None, :]) & upper[:, None] & (~upper)[None, :])
    blocks.append(u > t[:, None])
    mall = np.concatenate(blocks, axis=0).astype(np.float32)
    mask = np.stack(masks).astype(np.float32)
    return jnp.asarray(mall, BF16), jnp.asarray(mask, F32)


def _gla_body(q_ref, k_ref, g_ref, v_ref, r_ref, gain_ref, mall_ref, mask_ref,
              o_ref, st_ref, *, nchunk, hg, dk, dv):
    c = CHUNK
    nl = len(_LEVELS)

    @pl.when(pl.program_id(2) == 0)
    def _():
        st_ref[...] = jnp.zeros_like(st_ref)

    def chunk(ci, carry):
        sl = pl.ds(pl.multiple_of(ci * c, c), c)
        g_hi, g_lo = _split2(g_ref[0, sl, :])
        mall = mall_ref[...]
        xall = jnp.exp(_dot(mall, g_hi) + _dot(mall, g_lo))
        for hh in range(hg):
            ks, vs = slice(hh * dk, (hh + 1) * dk), slice(hh * dv, (hh + 1) * dv)
            q = q_ref[0, sl, ks]
            k = k_ref[0, sl, ks]
            v = v_ref[0, sl, vs]
            x = xall[:, ks]
            scores = mask_ref[0] * _dot_nt(q.astype(BF16), k.astype(BF16))
            for lv in range(nl):
                z = x[(1 + lv) * c:(2 + lv) * c]
                scores = scores + mask_ref[1 + lv] * _dot_nt((q * z).astype(BF16), (k * z).astype(BF16))
            st = st_ref[hh]
            o = _dot(scores.astype(BF16), v) + _dot_nt((q * x[0:c]).astype(BF16), st.astype(BF16))
            kt = (k * x[(1 + nl) * c:(2 + nl) * c]).astype(BF16)
            st_ref[hh] = st * x[c - 1:c, :] + _dot_tn(v, kt)
            r = r_ref[0, sl, vs].astype(F32)
            y = _rms(o, gain_ref[hh]) * (r * _sigmoid(r))
            o_ref[0, sl, vs] = y.astype(o_ref.dtype)
        return carry

    lax.fori_loop(0, nchunk, chunk, 0, unroll=2)


def gated_linear_attention(q, k, g, v, r, gain, heads, dk, dv, hg=4, ts=256):
    b, t, _ = q.shape
    ts = min(ts, t)
    mall, mask = _gla_consts()
    qspec = pl.BlockSpec((1, ts, hg * dk), lambda bi, h, i: (bi, i, h))
    vspec = pl.BlockSpec((1, ts, hg * dv), lambda bi, h, i: (bi, i, h))
    return pl.pallas_call(
        functools.partial(_gla_body, nchunk=ts // CHUNK, hg=hg, dk=dk, dv=dv),
        out_shape=jax.ShapeDtypeStruct((b, t, heads * dv), BF16),
        grid=(b, heads // hg, t // ts),
        in_specs=[qspec, qspec, qspec, vspec, vspec,
                  pl.BlockSpec((hg, 1, dv), lambda bi, h, i: (h, 0, 0)),
                  pl.BlockSpec(mall.shape, lambda bi, h, i: (0, 0)),
                  pl.BlockSpec(mask.shape, lambda bi, h, i: (0, 0, 0))],
        out_specs=vspec,
        scratch_shapes=[pltpu.VMEM((hg, dv, dk), F32)],
        compiler_params=_cparams(("parallel", "parallel", "arbitrary")),
        name="gated_linear_attention",
    )(q, k, g, v, r, gain.astype(F32).reshape(heads, 1, dv), mall, mask)


def _peer_cand_layout():
    k = PEER_TOPK
    ab = [(0, b) for b in range(16)] + [(1, b) for b in range(8)]
    ab += [(2, b) if b < 5 else None for b in range(8)]
    ab += [(3, 0), (3, 1), (3, 2), (3, 3), (4, 0), (4, 1), (4, 2), None]
    ab += [(5, 0), (5, 1), (6, 0), (6, 1), (7, 0), (7, 1), None, None]
    ab += [(a, 0) for a in range(8, 16)]
    assert all(p is None or (p[0] + 1) * (p[1] + 1) <= k for p in ab)
    assert sum(p is not None for p in ab) == sum(k // (a + 1) for a in range(k))
    pos = np.array([PEER_POS_INVALID if p is None else p[0] * k + p[1] for p in ab], np.float32)
    return np.broadcast_to(pos[:, None], (pos.size, LANES)).copy()


PEER_POS_INVALID = 1024.0


def _extract_max(s, ids, big):
    m = jnp.max(s, axis=0, keepdims=True)
    ix = jnp.min(jnp.where(s == m, ids, big), axis=0, keepdims=True)
    return m, ix


def _topk_rows(s, ids, big, k):
    vals, sel = [], []
    for _ in range(k):
        m, ix = _extract_max(s, ids, big)
        vals.append(m)
        sel.append(ix)
        s = jnp.where(ids == ix, -jnp.inf, s)
    return jnp.concatenate(vals, axis=0), jnp.concatenate(sel, axis=0)


def _topk_keys(st, k):
    n = st.shape[0]
    a = jnp.concatenate([st[r:r + 8] for r in range(0, n, 16)], axis=0)
    b = jnp.concatenate([st[r + 8:r + 16] for r in range(0, n, 16)], axis=0)
    r = lax.broadcasted_iota(jnp.int32, a.shape, 0)
    ida = (((r >> 3) << 4) + (r & 7)).astype(F32)
    idb = ida + 8.0
    a_wins = a >= b
    win, idw = jnp.maximum(a, b), jnp.where(a_wins, ida, idb)
    los, idl = jnp.minimum(a, b), jnp.where(a_wins, idb, ida)
    vals, sel = [], []
    for _ in range(k):
        m, ix = _extract_max(win, idw, float(n))
        vals.append(m)
        sel.append(ix)
        hit = idw == ix
        win = jnp.where(hit, los, win)
        idw = jnp.where(hit, idl, idw)
        los = jnp.where(hit, -jnp.inf, los)
    return jnp.concatenate(vals, axis=0), jnp.concatenate(sel, axis=0)


def _route_head(q_ref, sk_ref, pos, h, tm):
    k = PEER_TOPK
    nk = PEER_NKEYS
    sub = lax.broadcasted_iota(jnp.int32, (8, tm), 0)

    def bc(x, r):
        return jnp.broadcast_to(x[r:r + 1], (8, tm))

    tops = []
    for p in range(2):
        hp = 2 * h + p
        qh = q_ref[:, pl.ds(pl.multiple_of(hp * nk, nk), nk)]
        st = _dot_nt(sk_ref[hp], qh)
        tops.append(_topk_keys(st, k))
    (s0, i0), (s1, i1) = tops
    lo1 = s1[0:8]
    cand = jnp.concatenate([
        bc(s0, 0) + lo1,
        bc(s0, 0) + s1[8:16],
        bc(s0, 1) + lo1,
        bc(s0, 2) + lo1,
        jnp.where(sub < 4, bc(s0, 3), bc(s0, 4)) + jnp.where(sub < 4, lo1, pltpu.roll(lo1, 4, 0)),
        jnp.where(sub < 2, bc(s0, 5), jnp.where(sub < 4, bc(s0, 6), bc(s0, 7)))
        + jnp.where((sub & 1) == 0, bc(s1, 0), bc(s1, 1)),
        s0[8:16] + bc(s1, 0)], axis=0)
    cand = jnp.where(pos < PEER_POS_INVALID, cand, -jnp.inf)
    best, bpos = _topk_rows(cand, pos, 2.0 * PEER_POS_INVALID, k)
    bpos = bpos.astype(jnp.int32)
    ra = bpos >> 4
    rb = bpos & (k - 1)
    isel = jnp.zeros((k, tm), F32)
    jsel = jnp.zeros((k, tm), F32)
    for a in range(k):
        isel = jnp.where(ra == a, i0[a:a + 1], isel)
        jsel = jnp.where(rb == a, i1[a:a + 1], jsel)
    e = jnp.exp(best - best[0:1])
    gates = e / jnp.sum(e, axis=0, keepdims=True)
    return isel, jsel, gates


W3_GROUP = 8
W3_BATCH = 2 * W3_GROUP
ROUTE_HEADS_PER_TRIP = 2


def _peer_route_weights_body(q_ref, sk_ref, pos_ref, o_ref, ri_ref, rj_ref, rg_ref, pt_ref,
                             w_ref, *, tm):
    k = PEER_TOPK
    nk = PEER_NKEYS
    ngroup = tm // LANES
    nbatch = tm // W3_BATCH
    per_head = nbatch // PEER_HEADS

    @pl.when(pl.program_id(0) == 0)
    def _():
        ri_ref[...] = jnp.zeros_like(ri_ref)
        rj_ref[...] = jnp.zeros_like(rj_ref)
        rg_ref[...] = jnp.zeros_like(rg_ref)

    for a, r_ref in enumerate((ri_ref, rj_ref, rg_ref)):
        for c in range(ngroup):
            pt_ref[a, c * LANES:(c + 1) * LANES, :] = r_ref[:, c * LANES:(c + 1) * LANES].T
    pos = jnp.concatenate([pos_ref[...]] * ngroup, axis=1)
    sub = lax.broadcasted_iota(jnp.int32, (nk, LANES), 0).astype(F32)

    def build(t):
        slot = lax.rem(t, 2 * ROUTE_HEADS_PER_TRIP * per_head)
        for half in range(2):
            tok0 = pl.multiple_of(t * W3_BATCH + half * W3_GROUP, W3_GROUP)
            it, jt, gt = (pt_ref[a, pl.ds(tok0, W3_GROUP), :] for a in range(3))
            for s in range(W3_GROUP):
                at = jnp.where(sub == it[s:s + 1], gt[s:s + 1], 0.0).astype(BF16)
                bt = jnp.where(sub == jt[s:s + 1], 1.0, 0.0).astype(BF16)
                w_ref[2 * slot + half, pl.ds(s, nk, stride=W3_GROUP), :] = _dot_nt(at, bt)

    def flush(t):
        slot = lax.rem(t, 2 * ROUTE_HEADS_PER_TRIP * per_head)
        r0 = pl.multiple_of(t * W3_BATCH, W3_BATCH)
        for i in range(nk):
            rows = jnp.concatenate([w_ref[2 * slot, i * W3_GROUP:(i + 1) * W3_GROUP, :],
                                    w_ref[2 * slot + 1, i * W3_GROUP:(i + 1) * W3_GROUP, :]], axis=0)
            o_ref[pl.ds(r0, W3_BATCH), i * nk:(i + 1) * nk] = rows.astype(o_ref.dtype)

    hpt = ROUTE_HEADS_PER_TRIP

    def do_heads(u, flush_prev):
        if flush_prev:
            for t in range(hpt * per_head):
                flush((u - 1) * hpt * per_head + t)
        for hh in range(hpt):
            h = u * hpt + hh
            isel, jsel, gates = _route_head(q_ref, sk_ref, pos, h, tm)
            rows = pl.ds(pl.multiple_of(h * k, k), k)
            ri_ref[rows, :] = isel
            rj_ref[rows, :] = jsel
            rg_ref[rows, :] = gates
            for t in range(per_head):
                build(h * per_head + t)

    def trip(u, carry):
        do_heads(u, True)
        return carry

    ntrip = PEER_HEADS // hpt
    do_heads(jnp.int32(0), False)
    lax.fori_loop(1, ntrip, trip, 0)
    for t in range(hpt * per_head):
        flush(jnp.int32((ntrip - 1) * hpt * per_head + t))


def peer_route_weights(q, sub_keys, tm=256):
    n = q.shape[0]
    tm = min(tm, n)
    nblk = n // tm
    ne = PEER_NKEYS * PEER_NKEYS
    npair = PEER_HEADS * PEER_TOPK
    pos = jnp.asarray(_peer_cand_layout())
    return pl.pallas_call(
        functools.partial(_peer_route_weights_body, tm=tm),
        out_shape=jax.ShapeDtypeStruct((n, ne), BF16),
        grid=(nblk + 1,),
        in_specs=[pl.BlockSpec((tm, q.shape[1]), lambda s: (jnp.minimum(s, nblk - 1), 0)),
                  pl.BlockSpec(sub_keys.shape, lambda s: (0, 0, 0)),
                  pl.BlockSpec(pos.shape, lambda s: (0, 0))],
        out_specs=pl.BlockSpec((tm, ne), lambda s: (jnp.maximum(s - 1, 0), 0)),
        scratch_shapes=[pltpu.VMEM((npair, tm), F32)] * 3 + [pltpu.VMEM((3, tm, npair), F32)]
        + [pltpu.VMEM((4 * ROUTE_HEADS_PER_TRIP * tm // (W3_BATCH * PEER_HEADS),
                       W3_GROUP * PEER_NKEYS, PEER_NKEYS), F32)],
        compiler_params=_cparams(("arbitrary",)),
        name="peer_route_weights",
    )(q, sub_keys, pos)


DENSE_SUBTILE = 1024


def _gelu(x):
    return 0.5 * x * (1.0 + lax.erf(x * (1.0 / math.sqrt(2.0))))


def _peer_dense_body(x_ref, u_ref, v_ref, w_ref, h_ref, gn_ref, ho_ref, hn_ref, acc_ref, *, nj):
    j = pl.program_id(1)

    @pl.when(j == 0)
    def _():
        acc_ref[...] = jnp.zeros_like(acc_ref)

    x = x_ref[...]
    te = u_ref.shape[0]
    for s in range(te // min(te, DENSE_SUBTILE)):
        rows = slice(s * DENSE_SUBTILE, (s + 1) * DENSE_SUBTILE)
        hid = _dot_nt(x, u_ref[rows, :])
        a = (_gelu(hid) * w_ref[:, rows].astype(F32)).astype(BF16)
        acc_ref[...] += _dot(a, v_ref[rows, :])

    @pl.when(j == nj - 1)
    def _():
        hnew = h_ref[...] + acc_ref[...]
        ho_ref[...] = hnew
        hn_ref[...] = _rms(hnew, gn_ref[...]).astype(hn_ref.dtype)


def peer_dense(x, u, v, layer, w, h, gain_next, tm=1024, te=2048):
    n, d = x.shape
    ne = u.shape[1]
    tm = min(tm, n)
    te = min(te, ne)
    nj = ne // te
    rowspec_in = pl.BlockSpec((tm, d), lambda i, j: (i, 0), pipeline_mode=pl.Buffered(1))
    rowspec_out = pl.BlockSpec((tm, d), lambda i, j: (i, 0), pipeline_mode=pl.Buffered(1))
    return pl.pallas_call(
        functools.partial(_peer_dense_body, nj=nj),
        out_shape=[jax.ShapeDtypeStruct((n, d), F32), jax.ShapeDtypeStruct((n, d), BF16)],
        grid=(n // tm, nj),
        in_specs=[rowspec_in,
                  pl.BlockSpec((None, te, d), lambda i, j: (layer, j, 0)),
                  pl.BlockSpec((None, te, d), lambda i, j: (layer, j, 0)),
                  pl.BlockSpec((tm, te), lambda i, j: (i, j)),
                  rowspec_in,
                  pl.BlockSpec((1, d), lambda i, j: (0, 0))],
        out_specs=[rowspec_out, rowspec_out],
        scratch_shapes=[pltpu.VMEM((tm, d), F32)],
        compiler_params=_cparams(("parallel", "arbitrary")),
        name="peer_dense",
    )(x, u, v, w, h, gain_next.astype(F32).reshape(1, d))


def _fox_layer(hn, b, t, w_in, b_f, q_gain, k_gain):
    d = hn.shape[1]
    w = w_in.astype(BF16)
    qkvg = fox_proj(hn, w[:, 0:4 * d], q_gain, k_gain, FOX_HEAD_DIM ** -0.5, FOX_HEAD_DIM)
    lf = mm_logsig(hn, w[:, 4 * d:], b_f, 1.0)
    c = time_cumsum(lf.reshape(b, t, FOX_HEADS))
    c_rows = c.transpose(0, 2, 1).reshape(b, FOX_HEADS // 2, 2, t)
    o = fox_attention(qkvg.reshape(b, t, 4 * d), c_rows)
    return o.reshape(b * t, d)


def _gla_layer(hn, b, t, w_in, w_up, b_alpha, out_gain):
    d = hn.shape[1]
    gk = GLA_HEADS * GLA_DK
    w = w_in.astype(BF16)
    q = mm_plain(hn, w[:, 0:gk], F32, scale=GLA_DK ** -0.5)
    k = mm_plain(hn, w[:, gk:2 * gk], F32)
    v = mm_plain(hn, w[:, 2 * gk:2 * gk + d], BF16)
    r = mm_plain(hn, w[:, 2 * gk + d:2 * gk + 2 * d], BF16)
    low = mm_plain(hn, w[:, 2 * gk + 2 * d:], BF16)
    log_a = mm_logsig(low, w_up.astype(BF16), b_alpha, 1.0 / GLA_TAU)
    o = gated_linear_attention(q.reshape(b, t, gk), k.reshape(b, t, gk), log_a.reshape(b, t, gk),
                               v.reshape(b, t, d), r.reshape(b, t, d), out_gain,
                               GLA_HEADS, GLA_DK, GLA_DV)
    return o.reshape(b * t, d)


def _hgrn_layer(hn, b, t, layer, w_in, lb_logits, out_gain):
    d = hn.shape[1]
    w = w_in.astype(BF16)
    q = mm_plain(hn, w[:, 0:d], F32, scale=HGRN_DK ** -0.5)
    k, log_g = mm_hgrn_gate(hn, w[:, d:2 * d], lb_logits, layer)
    v = mm_plain(hn, w[:, 2 * d:3 * d], BF16)
    r = mm_plain(hn, w[:, 3 * d:4 * d], BF16)
    shp = (b, t, d)
    o = gated_linear_attention(q.reshape(shp), k.reshape(shp), log_g.reshape(shp),
                               v.reshape(shp), r.reshape(shp), out_gain,
                               HGRN_HEADS, HGRN_DK, HGRN_DV)
    return o.reshape(b * t, d)


def _peer_layer(h, hn, w_q, sub_keys, u, v, layer, gain_next):
    q = mm_plain(hn, w_q.astype(BF16), BF16)
    sk = sub_keys.astype(BF16).reshape(2 * PEER_HEADS, PEER_NKEYS, -1)
    w = peer_route_weights(q, sk)
    return peer_dense(hn, u, v, layer, w, h, gain_next)


def kernel(x, norm_mix, norm_ffn, fox_w_in, fox_b_f, fox_q_gain, fox_k_gain, fox_w_out,
           gla_w_in, gla_w_up, gla_b_alpha, gla_out_gain, gla_w_out,
           hgrn_w_in, hgrn_lb_logits, hgrn_out_gain, hgrn_w_out,
           peer_w_q, peer_sub_keys, peer_u, peer_v):
    b, t, d = x.shape
    depth = norm_mix.shape[0]
    h = x.reshape(b * t, d)
    hn = rmsnorm(h, norm_mix[0])
    u_all, v_all = peer_u.astype(BF16), peer_v.astype(BF16)
    for i in range(depth):
        m, j = i % N_MIXERS, i // N_MIXERS
        if m == 0:
            o = _fox_layer(hn, b, t, fox_w_in[j], fox_b_f[j], fox_q_gain[j], fox_k_gain[j])
            w_out = fox_w_out[j]
        elif m == 1:
            o = _gla_layer(hn, b, t, gla_w_in[j], gla_w_up[j], gla_b_alpha[j], gla_out_gain[j])
            w_out = gla_w_out[j]
        else:
            o = _hgrn_layer(hn, b, t, i, hgrn_w_in[j], hgrn_lb_logits, hgrn_out_gain[j])
            w_out = hgrn_w_out[j]
        h, hn = mm_res_norm(o, w_out.astype(BF16), h, norm_ffn[i])
        gain_next = norm_mix[(i + 1) % depth]
        h, hn = _peer_layer(h, hn, peer_w_q[i], peer_sub_keys[i], u_all, v_all, i, gain_next)
    return h.reshape(b, t, d)
```

```python
import functools
import math

import numpy as np
import jax
import jax.numpy as jnp
from jax import lax
from jax.experimental import pallas as pl
from jax.experimental.pallas import tpu as pltpu

F32 = jnp.float32
BF16 = jnp.bfloat16
EPS = 1e-6

DEPTH = 4
N_MIXERS = 3
CHUNK = 128
FOX_HEADS = 16
FOX_HEAD_DIM = 64
GLA_HEADS = 4
GLA_DK = 128
GLA_DV = 256
GLA_TAU = 16.0
HGRN_HEADS = 8
HGRN_DK = 128
HGRN_DV = 128
PEER_HEADS = 8
PEER_NKEYS = 128
PEER_TOPK = 16
LANES = 128
VMEM_LIMIT = 56 * 1024 * 1024


def _cparams(sem):
    return pltpu.CompilerParams(dimension_semantics=sem, vmem_limit_bytes=VMEM_LIMIT)


def _dot(a, b):
    return jnp.dot(a, b, preferred_element_type=F32)


def _dot_nt(a, b):
    return lax.dot_general(a, b, (((1,), (1,)), ((), ())), preferred_element_type=F32)


def _dot_tn(a, b):
    return lax.dot_general(a, b, (((0,), (0,)), ((), ())), preferred_element_type=F32)


def _split2(x):
    hi = x.astype(BF16)
    lo = (x - hi.astype(F32)).astype(BF16)
    return hi, lo


def _split3(x):
    hi = x.astype(BF16)
    r = x - hi.astype(F32)
    mid = r.astype(BF16)
    lo = (r - mid.astype(F32)).astype(BF16)
    return hi, mid, lo


def _log_sigmoid(x):
    return jnp.minimum(x, 0.0) - jnp.log1p(jnp.exp(-jnp.abs(x)))


def _sigmoid(x):
    return 1.0 / (1.0 + jnp.exp(-x))


def _rms(x, gain):
    ms = jnp.mean(x * x, axis=-1, keepdims=True)
    return x * lax.rsqrt(ms + EPS) * gain


def _rmsnorm_body(x_ref, g_ref, o_ref):
    o_ref[...] = _rms(x_ref[...], g_ref[...]).astype(o_ref.dtype)


def rmsnorm(x, gain, tm=512):
    n, d = x.shape
    tm = min(tm, n)
    return pl.pallas_call(
        _rmsnorm_body,
        out_shape=jax.ShapeDtypeStruct((n, d), BF16),
        grid=(n // tm,),
        in_specs=[pl.BlockSpec((tm, d), lambda i: (i, 0)),
                  pl.BlockSpec((1, d), lambda i: (0, 0))],
        out_specs=pl.BlockSpec((tm, d), lambda i: (i, 0)),
        compiler_params=_cparams(("parallel",)),
        name="rmsnorm",
    )(x, gain.reshape(1, d))


def _mm_call(body, x, w, extras, extra_specs, out_dtypes, tm, tn, name):
    n, k = x.shape
    m = w.shape[1]
    tm = min(tm, n)
    tn = min(tn, m)
    outs = [jax.ShapeDtypeStruct((n, m), dt) for dt in out_dtypes]
    return pl.pallas_call(
        body,
        out_shape=outs,
        grid=(n // tm, m // tn),
        in_specs=[pl.BlockSpec((tm, k), lambda i, j: (i, 0)),
                  pl.BlockSpec((k, tn), lambda i, j: (0, j))] + extra_specs,
        out_specs=[pl.BlockSpec((tm, tn), lambda i, j: (i, j)) for _ in outs],
        compiler_params=_cparams(("parallel", "arbitrary")),
        name=name,
    )(x, w, *extras)


def _mm_plain_body(x_ref, w_ref, o_ref, *, scale):
    acc = _dot(x_ref[...], w_ref[...])
    if scale != 1.0:
        acc = acc * scale
    o_ref[...] = acc.astype(o_ref.dtype)


def mm_plain(x, w, out_dtype, scale=1.0, tm=1024, tn=1024):
    return _mm_call(functools.partial(_mm_plain_body, scale=scale), x, w, [], [],
                    [out_dtype], tm, tn, "mm_plain")[0]


MXU_DIM = 256


def _fox_proj_body(x_ref, w_ref, bd_ref, g_ref, o_ref, *, hd):
    j = pl.program_id(1)
    acc = _dot(x_ref[...], w_ref[...])

    @pl.when(j < 2)
    def _():
        for c in range(acc.shape[1] // MXU_DIM):
            cols = slice(c * MXU_DIM, (c + 1) * MXU_DIM)
            a = acc[:, cols]
            ms = _dot((a * a).astype(BF16), bd_ref[...]) * (1.0 / hd)
            o_ref[:, cols] = (a * lax.rsqrt(ms + EPS) * g_ref[0, :, cols]).astype(o_ref.dtype)

    @pl.when(j >= 2)
    def _():
        o_ref[...] = acc.astype(o_ref.dtype)


def fox_proj(x, w, q_gain, k_gain, scale, hd, tm=1024):
    n, d = x.shape
    tm = min(tm, n)
    blk = np.arange(MXU_DIM) // hd
    bd = jnp.asarray(blk[:, None] == blk[None, :], BF16)
    g = jnp.stack([jnp.tile(q_gain.astype(F32) * scale, d // hd),
                   jnp.tile(k_gain.astype(F32), d // hd)]).reshape(2, 1, d)
    return pl.pallas_call(
        functools.partial(_fox_proj_body, hd=hd),
        out_shape=jax.ShapeDtypeStruct((n, 4 * d), BF16),
        grid=(n // tm, 4),
        in_specs=[pl.BlockSpec((tm, d), lambda i, j: (i, 0)),
                  pl.BlockSpec((d, d), lambda i, j: (0, j)),
                  pl.BlockSpec((MXU_DIM, MXU_DIM), lambda i, j: (0, 0)),
                  pl.BlockSpec((1, 1, d), lambda i, j: (jnp.minimum(j, 1), 0, 0))],
        out_specs=pl.BlockSpec((tm, d), lambda i, j: (i, j)),
        compiler_params=_cparams(("parallel", "arbitrary")),
        name="fox_proj",
    )(x, w, bd, g)


def _mm_logsig_body(x_ref, w_ref, b_ref, o_ref, *, mult):
    acc = _dot(x_ref[...], w_ref[...]) + b_ref[...]
    o_ref[...] = (_log_sigmoid(acc) * mult).astype(o_ref.dtype)


def mm_logsig(x, w, bias, mult, tm=512, tn=512):
    m = w.shape[1]
    tn = min(tn, m)
    return _mm_call(functools.partial(_mm_logsig_body, mult=mult), x, w,
                    [bias.astype(F32).reshape(1, m)],
                    [pl.BlockSpec((1, tn), lambda i, j: (0, j))],
                    [F32], tm, tn, "mm_logsig")[0]


def _mm_hgrn_gate_body(x_ref, w_ref, lbl_ref, k_ref, lg_ref, *, layer):
    acc = _dot(x_ref[...], w_ref[...])
    logits = lbl_ref[...]
    e = jnp.exp(logits - jnp.max(logits, axis=0, keepdims=True))
    p = e / jnp.sum(e, axis=0, keepdims=True)
    lb = jnp.zeros_like(p[0:1])
    for u in range(1, layer + 1):
        lb = lb + p[u:u + 1]
    gate = lb + (1.0 - lb) * _sigmoid(acc)
    k_ref[...] = (1.0 - gate).astype(k_ref.dtype)
    lg_ref[...] = jnp.log(gate).astype(lg_ref.dtype)


def mm_hgrn_gate(x, w, lb_logits, layer, tm=512, tn=512):
    m = w.shape[1]
    tn = min(tn, m)
    depth = lb_logits.shape[0]
    return _mm_call(functools.partial(_mm_hgrn_gate_body, layer=layer), x, w,
                    [lb_logits.astype(F32)],
                    [pl.BlockSpec((depth, tn), lambda i, j: (0, j))],
                    [F32, F32], tm, tn, "mm_hgrn_gate")


def _mm_res_norm_body(x_ref, w_ref, h_ref, g_ref, ho_ref, hn_ref):
    hnew = h_ref[...] + _dot(x_ref[...], w_ref[...])
    ho_ref[...] = hnew
    hn_ref[...] = _rms(hnew, g_ref[...]).astype(hn_ref.dtype)


def mm_res_norm(x, w, h, gain, tm=512):
    n, d = h.shape
    return _mm_call(_mm_res_norm_body, x, w, [h, gain.astype(F32).reshape(1, d)],
                    [pl.BlockSpec((min(tm, n), d), lambda i, j: (i, 0)),
                     pl.BlockSpec((1, d), lambda i, j: (0, 0))],
                    [F32, BF16], tm, d, "mm_res_norm")


CUM_BLOCK = 256


def _cumsum_body(x_ref, o_ref, *, t):
    nb = t // CUM_BLOCK
    r = lax.broadcasted_iota(jnp.int32, (CUM_BLOCK, CUM_BLOCK), 0)
    c = lax.broadcasted_iota(jnp.int32, (CUM_BLOCK, CUM_BLOCK), 1)
    tril = jnp.where(r >= c, 1.0, 0.0).astype(BF16)
    carry = jnp.zeros((1, x_ref.shape[-1]), F32)
    for b in range(nb):
        x = x_ref[0, b * CUM_BLOCK:(b + 1) * CUM_BLOCK, :]
        hi, mid, lo = _split3(x)
        cs = _dot(tril, hi) + _dot(tril, mid) + _dot(tril, lo) + carry
        o_ref[0, b * CUM_BLOCK:(b + 1) * CUM_BLOCK, :] = cs
        carry = cs[CUM_BLOCK - 1:CUM_BLOCK, :]


def time_cumsum(x):
    b, t, h = x.shape
    return pl.pallas_call(
        functools.partial(_cumsum_body, t=t),
        out_shape=jax.ShapeDtypeStruct((b, t, h), F32),
        grid=(b,),
        in_specs=[pl.BlockSpec((1, t, h), lambda i: (i, 0, 0))],
        out_specs=pl.BlockSpec((1, t, h), lambda i: (i, 0, 0)),
        compiler_params=_cparams(("parallel",)),
        name="time_cumsum",
    )(x)


def _fox_body(q_ref, k_ref, v_ref, g_ref, c_ref, o_ref, s_ref, m_ref, l_ref, acc_ref, *, tq):
    qi = pl.program_id(2)
    q = q_ref[0]
    lane = lax.broadcasted_iota(jnp.int32, (tq, LANES), 1)
    row = lax.broadcasted_iota(jnp.int32, (tq, tq), 0)
    col = lax.broadcasted_iota(jnp.int32, (tq, tq), 1)
    zero = jnp.zeros_like(q)
    qs = [jnp.where(lane < FOX_HEAD_DIM, q, zero), jnp.where(lane >= FOX_HEAD_DIM, q, zero)]
    nfold = tq // LANES

    def fold(x, op):
        r = x[:, 0:LANES]
        for f in range(1, nfold):
            r = op(r, x[:, f * LANES:(f + 1) * LANES])
        return r

    def scores(j, off):
        ks = k_ref[0, pl.ds(off, tq), :]
        return _dot_nt(qs[j], ks) - c_ref[0, 0, pl.ds(j, 1), pl.ds(off, tq)]

    m_ref[...] = jnp.full(m_ref.shape, -jnp.inf, F32)
    l_ref[...] = jnp.zeros(l_ref.shape, F32)
    acc_ref[...] = jnp.zeros(acc_ref.shape, F32)

    def pass1(kb, carry):
        off = pl.multiple_of(kb * tq, tq)
        for j in range(2):
            s = scores(j, off)
            s_ref[j, :, pl.ds(off, tq)] = s
            m_ref[j] = jnp.maximum(m_ref[j], fold(s, jnp.maximum))
        return carry

    lax.fori_loop(0, qi, pass1, 0)
    offd = pl.multiple_of(qi * tq, tq)
    sd, ms = [], []
    for j in range(2):
        s = jnp.where(row >= col, scores(j, offd), -jnp.inf)
        sd.append(s)
        ms.append(jnp.max(jnp.maximum(m_ref[j], fold(s, jnp.maximum)), axis=-1, keepdims=True))

    def pass2(kb, carry):
        off = pl.multiple_of(kb * tq, tq)
        vs = v_ref[0, pl.ds(off, tq), :]
        for j in range(2):
            p = jnp.exp(s_ref[j, :, pl.ds(off, tq)] - ms[j])
            l_ref[j] += fold(p, jnp.add)
            acc_ref[j] += _dot(p.astype(BF16), vs)
        return carry

    lax.fori_loop(0, qi, pass2, 0)
    vsd = v_ref[0, pl.ds(offd, tq), :]
    outs = []
    for j in range(2):
        p = jnp.exp(sd[j] - ms[j])
        l = jnp.sum(l_ref[j] + fold(p, jnp.add), axis=-1, keepdims=True)
        outs.append((acc_ref[j] + _dot(p.astype(BF16), vsd)) / l)
    o = jnp.where(lane < FOX_HEAD_DIM, outs[0], outs[1])
    o = o * _sigmoid(g_ref[0].astype(F32))
    o_ref[0] = o.astype(o_ref.dtype)


def fox_attention(qkvg, c_rows, tq=512):
    b, t, d4 = qkvg.shape
    d = d4 // 4
    tq = min(tq, t)
    npair = d // LANES
    qspec = pl.BlockSpec((1, tq, LANES), lambda bi, p, i: (bi, i, p))
    return pl.pallas_call(
        functools.partial(_fox_body, tq=tq),
        out_shape=jax.ShapeDtypeStruct((b, t, d), BF16),
        grid=(b, npair, t // tq),
        in_specs=[qspec,
                  pl.BlockSpec((1, t, LANES), lambda bi, p, i: (bi, 0, npair + p)),
                  pl.BlockSpec((1, t, LANES), lambda bi, p, i: (bi, 0, 2 * npair + p)),
                  pl.BlockSpec((1, tq, LANES), lambda bi, p, i: (bi, i, 3 * npair + p)),
                  pl.BlockSpec((1, 1, 2, t), lambda bi, p, i: (bi, p, 0, 0))],
        out_specs=qspec,
        scratch_shapes=[pltpu.VMEM((2, tq, t), F32), pltpu.VMEM((2, tq, LANES), F32),
                        pltpu.VMEM((2, tq, LANES), F32), pltpu.VMEM((2, tq, LANES), F32)],
        compiler_params=_cparams(("parallel", "parallel", "arbitrary")),
        name="fox_attention",
    )(qkvg, qkvg, qkvg, qkvg, c_rows)


_LEVELS = tuple(CHUNK >> s for s in range(CHUNK.bit_length() - 1))


def _gla_consts():
    c = CHUNK
    t = np.arange(c)
    u = t[None, :]
    blocks = [u <= t[:, None]]
    masks = [np.eye(c, dtype=bool)]
    for lv in _LEVELS:
        half = lv // 2
        blk, pos = t // lv, t % lv
        r = blk * lv + half - 1
        upper = pos >= half
        qrow = upper[:, None] & (u > r[:, None]) & (u <= t[:, None])
        krow = (~upper)[:, None] & (u > t[:, None]) & (u <= r[:, None])
        blocks.append(qrow | krow)
        masks.append((blk[:, None] == blk[None, :]) & upper[:, None] & (~upper)[None, :])
    blocks.append(u > t[:, None])
    mall = np.concatenate(blocks, axis=0).astype(np.float32)
    mask = np.stack(masks).astype(np.float32)
    return jnp.asarray(mall, BF16), jnp.asarray(mask, F32)


def _gla_body(q_ref, k_ref, g_ref, v_ref, r_ref, gain_ref, mall_ref, mask_ref,
              o_ref, st_ref, *, nchunk, hg, dk, dv):
    c = CHUNK
    nl = len(_LEVELS)

    @pl.when(pl.program_id(2) == 0)
    def _():
        st_ref[...] = jnp.zeros_like(st_ref)

    def chunk(ci, carry):
        sl = pl.ds(pl.multiple_of(ci * c, c), c)
        g_hi, g_lo = _split2(g_ref[0, sl, :])
        mall = mall_ref[...]
        xall = jnp.exp(_dot(mall, g_hi) + _dot(mall, g_lo))
        for hh in range(hg):
            ks, vs = slice(hh * dk, (hh + 1) * dk), slice(hh * dv, (hh + 1) * dv)
            q = q_ref[0, sl, ks]
            k = k_ref[0, sl, ks]
            v = v_ref[0, sl, vs]
            x = xall[:, ks]
            scores = mask_ref[0] * _dot_nt(q.astype(BF16), k.astype(BF16))
            for lv in range(nl):
                z = x[(1 + lv) * c:(2 + lv) * c]
                scores = scores + mask_ref[1 + lv] * _dot_nt((q * z).astype(BF16), (k * z).astype(BF16))
            st = st_ref[hh]
            o = _dot(scores.astype(BF16), v) + _dot_nt((q * x[0:c]).astype(BF16), st.astype(BF16))
            kt = (k * x[(1 + nl) * c:(2 + nl) * c]).astype(BF16)
            st_ref[hh] = st * x[c - 1:c, :] + _dot_tn(v, kt)
            r = r_ref[0, sl, vs].astype(F32)
            y = _rms(o, gain_ref[hh]) * (r * _sigmoid(r))
            o_ref[0, sl, vs] = y.astype(o_ref.dtype)
        return carry

    lax.fori_loop(0, nchunk, chunk, 0, unroll=2)


def gated_linear_attention(q, k, g, v, r, gain, heads, dk, dv, hg=4, ts=512):
    b, t, _ = q.shape
    ts = min(ts, t)
    mall, mask = _gla_consts()
    qspec = pl.BlockSpec((1, ts, hg * dk), lambda bi, h, i: (bi, i, h))
    vspec = pl.BlockSpec((1, ts, hg * dv), lambda bi, h, i: (bi, i, h))
    return pl.pallas_call(
        functools.partial(_gla_body, nchunk=ts // CHUNK, hg=hg, dk=dk, dv=dv),
        out_shape=jax.ShapeDtypeStruct((b, t, heads * dv), BF16),
        grid=(b, heads // hg, t // ts),
        in_specs=[qspec, qspec, qspec, vspec, vspec,
                  pl.BlockSpec((hg, 1, dv), lambda bi, h, i: (h, 0, 0)),
                  pl.BlockSpec(mall.shape, lambda bi, h, i: (0, 0)),
                  pl.BlockSpec(mask.shape, lambda bi, h, i: (0, 0, 0))],
        out_specs=vspec,
        scratch_shapes=[pltpu.VMEM((hg, dv, dk), F32)],
        compiler_params=_cparams(("parallel", "parallel", "arbitrary")),
        name="gated_linear_attention",
    )(q, k, g, v, r, gain.astype(F32).reshape(heads, 1, dv), mall, mask)


def _peer_cand_layout():
    k = PEER_TOPK
    ab = [(0, b) for b in range(16)] + [(1, b) for b in range(8)]
    ab += [(2, b) if b < 5 else None for b in range(8)]
    ab += [(3, 0), (3, 1), (3, 2), (3, 3), (4, 0), (4, 1), (4, 2), None]
    ab += [(5, 0), (5, 1), (6, 0), (6, 1), (7, 0), (7, 1), None, None]
    ab += [(a, 0) for a in range(8, 16)]
    assert all(p is None or (p[0] + 1) * (p[1] + 1) <= k for p in ab)
    assert sum(p is not None for p in ab) == sum(k // (a + 1) for a in range(k))
    pos = np.array([PEER_POS_INVALID if p is None else p[0] * k + p[1] for p in ab], np.float32)
    return np.broadcast_to(pos[:, None], (pos.size, LANES)).copy()


PEER_POS_INVALID = 1024.0


def _extract_max(s, ids, big):
    m = jnp.max(s, axis=0, keepdims=True)
    ix = jnp.min(jnp.where(s == m, ids, big), axis=0, keepdims=True)
    return m, ix


def _topk_rows(s, ids, big, k):
    vals, sel = [], []
    for _ in range(k):
        m, ix = _extract_max(s, ids, big)
        vals.append(m)
        sel.append(ix)
        s = jnp.where(ids == ix, -jnp.inf, s)
    return jnp.concatenate(vals, axis=0), jnp.concatenate(sel, axis=0)


def _topk_keys(st, k):
    n = st.shape[0]
    a = jnp.concatenate([st[r:r + 8] for r in range(0, n, 16)], axis=0)
    b = jnp.concatenate([st[r + 8:r + 16] for r in range(0, n, 16)], axis=0)
    r = lax.broadcasted_iota(jnp.int32, a.shape, 0)
    ida = (((r >> 3) << 4) + (r & 7)).astype(F32)
    idb = ida + 8.0
    a_wins = a >= b
    win, idw = jnp.maximum(a, b), jnp.where(a_wins, ida, idb)
    los, idl = jnp.minimum(a, b), jnp.where(a_wins, idb, ida)
    vals, sel = [], []
    for _ in range(k):
        m, ix = _extract_max(win, idw, float(n))
        vals.append(m)
        sel.append(ix)
        hit = idw == ix
        win = jnp.where(hit, los, win)
        idw = jnp.where(hit, idl, idw)
        los = jnp.where(hit, -jnp.inf, los)
    return jnp.concatenate(vals, axis=0), jnp.concatenate(sel, axis=0)


def _route_head(q_ref, sk_ref, pos, h, tm):
    k = PEER_TOPK
    nk = PEER_NKEYS
    sub = lax.broadcasted_iota(jnp.int32, (8, tm), 0)

    def bc(x, r):
        return jnp.broadcast_to(x[r:r + 1], (8, tm))

    tops = []
    for p in range(2):
        hp = 2 * h + p
        qh = q_ref[:, pl.ds(pl.multiple_of(hp * nk, nk), nk)]
        st = _dot_nt(sk_ref[hp], qh)
        tops.append(_topk_keys(st, k))
    (s0, i0), (s1, i1) = tops
    lo1 = s1[0:8]
    cand = jnp.concatenate([
        bc(s0, 0) + lo1,
        bc(s0, 0) + s1[8:16],
        bc(s0, 1) + lo1,
        bc(s0, 2) + lo1,
        jnp.where(sub < 4, bc(s0, 3), bc(s0, 4)) + jnp.where(sub < 4, lo1, pltpu.roll(lo1, 4, 0)),
        jnp.where(sub < 2, bc(s0, 5), jnp.where(sub < 4, bc(s0, 6), bc(s0, 7)))
        + jnp.where((sub & 1) == 0, bc(s1, 0), bc(s1, 1)),
        s0[8:16] + bc(s1, 0)], axis=0)
    cand = jnp.where(pos < PEER_POS_INVALID, cand, -jnp.inf)
    best, bpos = _topk_rows(cand, pos, 2.0 * PEER_POS_INVALID, k)
    bpos = bpos.astype(jnp.int32)
    ra = bpos >> 4
    rb = bpos & (k - 1)
    isel = jnp.zeros((k, tm), F32)
    jsel = jnp.zeros((k, tm), F32)
    for a in range(k):
        isel = jnp.where(ra == a, i0[a:a + 1], isel)
        jsel = jnp.where(rb == a, i1[a:a + 1], jsel)
    e = jnp.exp(best - best[0:1])
    gates = e / jnp.sum(e, axis=0, keepdims=True)
    return isel, jsel, gates


W3_GROUP = 8
W3_BATCH = 2 * W3_GROUP
ROUTE_HEADS_PER_TRIP = 2


def _peer_route_weights_body(q_ref, sk_ref, pos_ref, o_ref, ri_ref, rj_ref, rg_ref, pt_ref,
                             w_ref, *, tm):
    k = PEER_TOPK
    nk = PEER_NKEYS
    ngroup = tm // LANES
    nbatch = tm // W3_BATCH
    per_head = nbatch // PEER_HEADS

    @pl.when(pl.program_id(0) == 0)
    def _():
        ri_ref[...] = jnp.zeros_like(ri_ref)
        rj_ref[...] = jnp.zeros_like(rj_ref)
        rg_ref[...] = jnp.zeros_like(rg_ref)

    for a, r_ref in enumerate((ri_ref, rj_ref, rg_ref)):
        for c in range(ngroup):
            pt_ref[a, c * LANES:(c + 1) * LANES, :] = r_ref[:, c * LANES:(c + 1) * LANES].T
    pos = jnp.concatenate([pos_ref[...]] * ngroup, axis=1)
    sub = lax.broadcasted_iota(jnp.int32, (nk, LANES), 0).astype(F32)

    def build(t):
        slot = lax.rem(t, 2 * ROUTE_HEADS_PER_TRIP * per_head)
        for half in range(2):
            tok0 = pl.multiple_of(t * W3_BATCH + half * W3_GROUP, W3_GROUP)
            it, jt, gt = (pt_ref[a, pl.ds(tok0, W3_GROUP), :] for a in range(3))
            for s in range(W3_GROUP):
                at = jnp.where(sub == it[s:s + 1], gt[s:s + 1], 0.0).astype(BF16)
                bt = jnp.where(sub == jt[s:s + 1], 1.0, 0.0).astype(BF16)
                w_ref[2 * slot + half, pl.ds(s, nk, stride=W3_GROUP), :] = _dot_nt(at, bt)

    def flush(t):
        slot = lax.rem(t, 2 * ROUTE_HEADS_PER_TRIP * per_head)
        r0 = pl.multiple_of(t * W3_BATCH, W3_BATCH)
        for i in range(nk):
            rows = jnp.concatenate([w_ref[2 * slot, i * W3_GROUP:(i + 1) * W3_GROUP, :],
                                    w_ref[2 * slot + 1, i * W3_GROUP:(i + 1) * W3_GROUP, :]], axis=0)
            o_ref[pl.ds(r0, W3_BATCH), i * nk:(i + 1) * nk] = rows.astype(o_ref.dtype)

    hpt = ROUTE_HEADS_PER_TRIP

    def do_heads(u, flush_prev):
        if flush_prev:
            for t in range(hpt * per_head):
                flush((u - 1) * hpt * per_head + t)
        for hh in range(hpt):
            h = u * hpt + hh
            isel, jsel, gates = _route_head(q_ref, sk_ref, pos, h, tm)
            rows = pl.ds(pl.multiple_of(h * k, k), k)
            ri_ref[rows, :] = isel
            rj_ref[rows, :] = jsel
            rg_ref[rows, :] = gates
            for t in range(per_head):
                build(h * per_head + t)

    def trip(u, carry):
        do_heads(u, True)
        return carry

    ntrip = PEER_HEADS // hpt
    do_heads(jnp.int32(0), False)
    lax.fori_loop(1, ntrip, trip, 0)
    for t in range(hpt * per_head):
        flush(jnp.int32((ntrip - 1) * hpt * per_head + t))


def peer_route_weights(q, sub_keys, tm=256):
    n = q.shape[0]
    tm = min(tm, n)
    nblk = n // tm
    ne = PEER_NKEYS * PEER_NKEYS
    npair = PEER_HEADS * PEER_TOPK
    pos = jnp.asarray(_peer_cand_layout())
    return pl.pallas_call(
        functools.partial(_peer_route_weights_body, tm=tm),
        out_shape=jax.ShapeDtypeStruct((n, ne), BF16),
        grid=(nblk + 1,),
        in_specs=[pl.BlockSpec((tm, q.shape[1]), lambda s: (jnp.minimum(s, nblk - 1), 0)),
                  pl.BlockSpec(sub_keys.shape, lambda s: (0, 0, 0)),
                  pl.BlockSpec(pos.shape, lambda s: (0, 0))],
        out_specs=pl.BlockSpec((tm, ne), lambda s: (jnp.maximum(s - 1, 0), 0)),
        scratch_shapes=[pltpu.VMEM((npair, tm), F32)] * 3 + [pltpu.VMEM((3, tm, npair), F32)]
        + [pltpu.VMEM((4 * ROUTE_HEADS_PER_TRIP * tm // (W3_BATCH * PEER_HEADS),
                       W3_GROUP * PEER_NKEYS, PEER_NKEYS), F32)],
        compiler_params=_cparams(("arbitrary",)),
        name="peer_route_weights",
    )(q, sub_keys, pos)


DENSE_SUBTILE = 1024


def _gelu(x):
    return 0.5 * x * (1.0 + lax.erf(x * (1.0 / math.sqrt(2.0))))


def _peer_dense_body(x_ref, u_ref, v_ref, w_ref, h_ref, gn_ref, ho_ref, hn_ref, acc_ref, *, nj):
    j = pl.program_id(1)

    @pl.when(j == 0)
    def _():
        acc_ref[...] = jnp.zeros_like(acc_ref)

    x = x_ref[...]
    te = u_ref.shape[0]
    for s in range(te // min(te, DENSE_SUBTILE)):
        rows = slice(s * DENSE_SUBTILE, (s + 1) * DENSE_SUBTILE)
        hid = _dot_nt(x, u_ref[rows, :])
        a = (_gelu(hid) * w_ref[:, rows].astype(F32)).astype(BF16)
        acc_ref[...] += _dot(a, v_ref[rows, :])

    @pl.when(j == nj - 1)
    def _():
        hnew = h_ref[...] + acc_ref[...]
        ho_ref[...] = hnew
        hn_ref[...] = _rms(hnew, gn_ref[...]).astype(hn_ref.dtype)


def peer_dense(x, u, v, layer, w, h, gain_next, tm=1024, te=2048):
    n, d = x.shape
    ne = u.shape[1]
    tm = min(tm, n)
    te = min(te, ne)
    nj = ne // te
    rowspec_in = pl.BlockSpec((tm, d), lambda i, j: (i, 0), pipeline_mode=pl.Buffered(1))
    rowspec_out = pl.BlockSpec((tm, d), lambda i, j: (i, 0), pipeline_mode=pl.Buffered(1))
    return pl.pallas_call(
        functools.partial(_peer_dense_body, nj=nj),
        out_shape=[jax.ShapeDtypeStruct((n, d), F32), jax.ShapeDtypeStruct((n, d), BF16)],
        grid=(n // tm, nj),
        in_specs=[rowspec_in,
                  pl.BlockSpec((None, te, d), lambda i, j: (layer, j, 0)),
                  pl.BlockSpec((None, te, d), lambda i, j: (layer, j, 0)),
                  pl.BlockSpec((tm, te), lambda i, j: (i, j)),
                  rowspec_in,
                  pl.BlockSpec((1, d), lambda i, j: (0, 0))],
        out_specs=[rowspec_out, rowspec_out],
        scratch_shapes=[pltpu.VMEM((tm, d), F32)],
        compiler_params=_cparams(("parallel", "arbitrary")),
        name="peer_dense",
    )(x, u, v, w, h, gain_next.astype(F32).reshape(1, d))


def _fox_layer(hn, b, t, w_in, b_f, q_gain, k_gain):
    d = hn.shape[1]
    w = w_in.astype(BF16)
    qkvg = fox_proj(hn, w[:, 0:4 * d], q_gain, k_gain, FOX_HEAD_DIM ** -0.5, FOX_HEAD_DIM)
    lf = mm_logsig(hn, w[:, 4 * d:], b_f, 1.0)
    c = time_cumsum(lf.reshape(b, t, FOX_HEADS))
    c_rows = c.transpose(0, 2, 1).reshape(b, FOX_HEADS // 2, 2, t)
    o = fox_attention(qkvg.reshape(b, t, 4 * d), c_rows)
    return o.reshape(b * t, d)


def _gla_layer(hn, b, t, w_in, w_up, b_alpha, out_gain):
    d = hn.shape[1]
    gk = GLA_HEADS * GLA_DK
    w = w_in.astype(BF16)
    q = mm_plain(hn, w[:, 0:gk], F32, scale=GLA_DK ** -0.5)
    k = mm_plain(hn, w[:, gk:2 * gk], F32)
    v = mm_plain(hn, w[:, 2 * gk:2 * gk + d], BF16)
    r = mm_plain(hn, w[:, 2 * gk + d:2 * gk + 2 * d], BF16)
    low = mm_plain(hn, w[:, 2 * gk + 2 * d:], BF16)
    log_a = mm_logsig(low, w_up.astype(BF16), b_alpha, 1.0 / GLA_TAU)
    o = gated_linear_attention(q.reshape(b, t, gk), k.reshape(b, t, gk), log_a.reshape(b, t, gk),
                               v.reshape(b, t, d), r.reshape(b, t, d), out_gain,
                               GLA_HEADS, GLA_DK, GLA_DV)
    return o.reshape(b * t, d)


def _hgrn_layer(hn, b, t, layer, w_in, lb_logits, out_gain):
    d = hn.shape[1]
    w = w_in.astype(BF16)
    q = mm_plain(hn, w[:, 0:d], F32, scale=HGRN_DK ** -0.5)
    k, log_g = mm_hgrn_gate(hn, w[:, d:2 * d], lb_logits, layer)
    v = mm_plain(hn, w[:, 2 * d:3 * d], BF16)
    r = mm_plain(hn, w[:, 3 * d:4 * d], BF16)
    shp = (b, t, d)
    o = gated_linear_attention(q.reshape(shp), k.reshape(shp), log_g.reshape(shp),
                               v.reshape(shp), r.reshape(shp), out_gain,
                               HGRN_HEADS, HGRN_DK, HGRN_DV)
    return o.reshape(b * t, d)


def _peer_layer(h, hn, w_q, sub_keys, u, v, layer, gain_next):
    q = mm_plain(hn, w_q.astype(BF16), BF16)
    sk = sub_keys.astype(BF16).reshape(2 * PEER_HEADS, PEER_NKEYS, -1)
    w = peer_route_weights(q, sk)
    return peer_dense(hn, u, v, layer, w, h, gain_next)


def kernel(x, norm_mix, norm_ffn, fox_w_in, fox_b_f, fox_q_gain, fox_k_gain, fox_w_out,
           gla_w_in, gla_w_up, gla_b_alpha, gla_out_gain, gla_w_out,
           hgrn_w_in, hgrn_lb_logits, hgrn_out_gain, hgrn_w_out,
           peer_w_q, peer_sub_keys, peer_u, peer_v):
    b, t, d = x.shape
    depth = norm_mix.shape[0]
    h = x.reshape(b * t, d)
    hn = rmsnorm(h, norm_mix[0])
    u_all, v_all = peer_u.astype(BF16), peer_v.astype(BF16)
    for i in range(depth):
        m, j = i % N_MIXERS, i // N_MIXERS
        if m == 0:
            o = _fox_layer(hn, b, t, fox_w_in[j], fox_b_f[j], fox_q_gain[j], fox_k_gain[j])
            w_out = fox_w_out[j]
        elif m == 1:
            o = _gla_layer(hn, b, t, gla_w_in[j], gla_w_up[j], gla_b_alpha[j], gla_out_gain[j])
            w_out = gla_w_out[j]
        else:
            o = _hgrn_layer(hn, b, t, i, hgrn_w_in[j], hgrn_lb_logits, hgrn_out_gain[j])
            w_out = hgrn_w_out[j]
        h, hn = mm_res_norm(o, w_out.astype(BF16), h, norm_ffn[i])
        gain_next = norm_mix[(i + 1) % depth]
        h, hn = _peer_layer(h, hn, peer_w_q[i], peer_sub_keys[i], u_all, v_all, i, gain_next)
    return h.reshape(b, t, d)
```

```python
import functools
import math

import numpy as np
import jax
import jax.numpy as jnp
from jax import lax
from jax.experimental import pallas as pl
from jax.experimental.pallas import tpu as pltpu

F32 = jnp.float32
BF16 = jnp.bfloat16
EPS = 1e-6

DEPTH = 4
N_MIXERS = 3
CHUNK = 128
FOX_HEADS = 16
FOX_HEAD_DIM = 64
GLA_HEADS = 4
GLA_DK = 128
GLA_DV = 256
GLA_TAU = 16.0
HGRN_HEADS = 8
HGRN_DK = 128
HGRN_DV = 128
PEER_HEADS = 8
PEER_NKEYS = 128
PEER_TOPK = 16
LANES = 128
VMEM_LIMIT = 56 * 1024 * 1024


def _cparams(sem):
    return pltpu.CompilerParams(dimension_semantics=sem, vmem_limit_bytes=VMEM_LIMIT)


def _dot(a, b):
    return jnp.dot(a, b, preferred_element_type=F32)


def _dot_nt(a, b):
    return lax.dot_general(a, b, (((1,), (1,)), ((), ())), preferred_element_type=F32)


def _dot_tn(a, b):
    return lax.dot_general(a, b, (((0,), (0,)), ((), ())), preferred_element_type=F32)


def _split2(x):
    hi = x.astype(BF16)
    lo = (x - hi.astype(F32)).astype(BF16)
    return hi, lo


def _split3(x):
    hi = x.astype(BF16)
    r = x - hi.astype(F32)
    mid = r.astype(BF16)
    lo = (r - mid.astype(F32)).astype(BF16)
    return hi, mid, lo


def _log_sigmoid(x):
    return jnp.minimum(x, 0.0) - jnp.log1p(jnp.exp(-jnp.abs(x)))


def _sigmoid(x):
    return 1.0 / (1.0 + jnp.exp(-x))


def _rms(x, gain):
    ms = jnp.mean(x * x, axis=-1, keepdims=True)
    return x * lax.rsqrt(ms + EPS) * gain


def _rmsnorm_body(x_ref, g_ref, o_ref):
    o_ref[...] = _rms(x_ref[...], g_ref[...]).astype(o_ref.dtype)


def rmsnorm(x, gain, tm=512):
    n, d = x.shape
    tm = min(tm, n)
    return pl.pallas_call(
        _rmsnorm_body,
        out_shape=jax.ShapeDtypeStruct((n, d), BF16),
        grid=(n // tm,),
        in_specs=[pl.BlockSpec((tm, d), lambda i: (i, 0)),
                  pl.BlockSpec((1, d), lambda i: (0, 0))],
        out_specs=pl.BlockSpec((tm, d), lambda i: (i, 0)),
        compiler_params=_cparams(("parallel",)),
        name="rmsnorm",
    )(x, gain.reshape(1, d))


def _mm_call(body, x, w, extras, extra_specs, out_dtypes, tm, tn, name):
    n, k = x.shape
    m = w.shape[1]
    tm = min(tm, n)
    tn = min(tn, m)
    outs = [jax.ShapeDtypeStruct((n, m), dt) for dt in out_dtypes]
    return pl.pallas_call(
        body,
        out_shape=outs,
        grid=(n // tm, m // tn),
        in_specs=[pl.BlockSpec((tm, k), lambda i, j: (i, 0)),
                  pl.BlockSpec((k, tn), lambda i, j: (0, j))] + extra_specs,
        out_specs=[pl.BlockSpec((tm, tn), lambda i, j: (i, j)) for _ in outs],
        compiler_params=_cparams(("parallel", "arbitrary")),
        name=name,
    )(x, w, *extras)


def _mm_plain_body(x_ref, w_ref, o_ref, *, scale):
    acc = _dot(x_ref[...], w_ref[...])
    if scale != 1.0:
        acc = acc * scale
    o_ref[...] = acc.astype(o_ref.dtype)


def mm_plain(x, w, out_dtype, scale=1.0, tm=1024, tn=1024):
    return _mm_call(functools.partial(_mm_plain_body, scale=scale), x, w, [], [],
                    [out_dtype], tm, tn, "mm_plain")[0]


MXU_DIM = 256


def _fox_proj_body(x_ref, w_ref, bd_ref, g_ref, o_ref, *, hd):
    j = pl.program_id(1)
    acc = _dot(x_ref[...], w_ref[...])

    @pl.when(j < 2)
    def _():
        for c in range(acc.shape[1] // MXU_DIM):
            cols = slice(c * MXU_DIM, (c + 1) * MXU_DIM)
            a = acc[:, cols]
            ms = _dot((a * a).astype(BF16), bd_ref[...]) * (1.0 / hd)
            o_ref[:, cols] = (a * lax.rsqrt(ms + EPS) * g_ref[0, :, cols]).astype(o_ref.dtype)

    @pl.when(j >= 2)
    def _():
        o_ref[...] = acc.astype(o_ref.dtype)


def fox_proj(x, w, q_gain, k_gain, scale, hd, tm=1024):
    n, d = x.shape
    tm = min(tm, n)
    blk = np.arange(MXU_DIM) // hd
    bd = jnp.asarray(blk[:, None] == blk[None, :], BF16)
    g = jnp.stack([jnp.tile(q_gain.astype(F32) * scale, d // hd),
                   jnp.tile(k_gain.astype(F32), d // hd)]).reshape(2, 1, d)
    return pl.pallas_call(
        functools.partial(_fox_proj_body, hd=hd),
        out_shape=jax.ShapeDtypeStruct((n, 4 * d), BF16),
        grid=(n // tm, 4),
        in_specs=[pl.BlockSpec((tm, d), lambda i, j: (i, 0)),
                  pl.BlockSpec((d, d), lambda i, j: (0, j)),
                  pl.BlockSpec((MXU_DIM, MXU_DIM), lambda i, j: (0, 0)),
                  pl.BlockSpec((1, 1, d), lambda i, j: (jnp.minimum(j, 1), 0, 0))],
        out_specs=pl.BlockSpec((tm, d), lambda i, j: (i, j)),
        compiler_params=_cparams(("parallel", "arbitrary")),
        name="fox_proj",
    )(x, w, bd, g)


def _mm_logsig_body(x_ref, w_ref, b_ref, o_ref, *, mult):
    acc = _dot(x_ref[...], w_ref[...]) + b_ref[...]
    o_ref[...] = (_log_sigmoid(acc) * mult).astype(o_ref.dtype)


def mm_logsig(x, w, bias, mult, tm=512, tn=512):
    m = w.shape[1]
    tn = min(tn, m)
    return _mm_call(functools.partial(_mm_logsig_body, mult=mult), x, w,
                    [bias.astype(F32).reshape(1, m)],
                    [pl.BlockSpec((1, tn), lambda i, j: (0, j))],
                    [F32], tm, tn, "mm_logsig")[0]


def _mm_hgrn_gate_body(x_ref, w_ref, lbl_ref, k_ref, lg_ref, *, layer):
    acc = _dot(x_ref[...], w_ref[...])
    logits = lbl_ref[...]
    e = jnp.exp(logits - jnp.max(logits, axis=0, keepdims=True))
    p = e / jnp.sum(e, axis=0, keepdims=True)
    lb = jnp.zeros_like(p[0:1])
    for u in range(1, layer + 1):
        lb = lb + p[u:u + 1]
    gate = lb + (1.0 - lb) * _sigmoid(acc)
    k_ref[...] = (1.0 - gate).astype(k_ref.dtype)
    lg_ref[...] = jnp.log(gate).astype(lg_ref.dtype)


def mm_hgrn_gate(x, w, lb_logits, layer, tm=512, tn=512):
    m = w.shape[1]
    tn = min(tn, m)
    depth = lb_logits.shape[0]
    return _mm_call(functools.partial(_mm_hgrn_gate_body, layer=layer), x, w,
                    [lb_logits.astype(F32)],
                    [pl.BlockSpec((depth, tn), lambda i, j: (0, j))],
                    [F32, F32], tm, tn, "mm_hgrn_gate")


def _mm_res_norm_body(x_ref, w_ref, h_ref, g_ref, ho_ref, hn_ref):
    hnew = h_ref[...] + _dot(x_ref[...], w_ref[...])
    ho_ref[...] = hnew
    hn_ref[...] = _rms(hnew, g_ref[...]).astype(hn_ref.dtype)


def mm_res_norm(x, w, h, gain, tm=512):
    n, d = h.shape
    return _mm_call(_mm_res_norm_body, x, w, [h, gain.astype(F32).reshape(1, d)],
                    [pl.BlockSpec((min(tm, n), d), lambda i, j: (i, 0)),
                     pl.BlockSpec((1, d), lambda i, j: (0, 0))],
                    [F32, BF16], tm, d, "mm_res_norm")


CUM_BLOCK = 256


def _cumsum_body(x_ref, o_ref, *, t):
    nb = t // CUM_BLOCK
    r = lax.broadcasted_iota(jnp.int32, (CUM_BLOCK, CUM_BLOCK), 0)
    c = lax.broadcasted_iota(jnp.int32, (CUM_BLOCK, CUM_BLOCK), 1)
    tril = jnp.where(r >= c, 1.0, 0.0).astype(BF16)
    carry = jnp.zeros((1, x_ref.shape[-1]), F32)
    for b in range(nb):
        x = x_ref[0, b * CUM_BLOCK:(b + 1) * CUM_BLOCK, :]
        hi, mid, lo = _split3(x)
        cs = _dot(tril, hi) + _dot(tril, mid) + _dot(tril, lo) + carry
        o_ref[0, b * CUM_BLOCK:(b + 1) * CUM_BLOCK, :] = cs
        carry = cs[CUM_BLOCK - 1:CUM_BLOCK, :]


def time_cumsum(x):
    b, t, h = x.shape
    return pl.pallas_call(
        functools.partial(_cumsum_body, t=t),
        out_shape=jax.ShapeDtypeStruct((b, t, h), F32),
        grid=(b,),
        in_specs=[pl.BlockSpec((1, t, h), lambda i: (i, 0, 0))],
        out_specs=pl.BlockSpec((1, t, h), lambda i: (i, 0, 0)),
        compiler_params=_cparams(("parallel",)),
        name="time_cumsum",
    )(x)


def _fox_body(q_ref, k_ref, v_ref, g_ref, c_ref, o_ref, s_ref, m_ref, l_ref, acc_ref, *, tq):
    qi = pl.program_id(2)
    q = q_ref[0]
    lane = lax.broadcasted_iota(jnp.int32, (tq, LANES), 1)
    row = lax.broadcasted_iota(jnp.int32, (tq, tq), 0)
    col = lax.broadcasted_iota(jnp.int32, (tq, tq), 1)
    zero = jnp.zeros_like(q)
    qs = [jnp.where(lane < FOX_HEAD_DIM, q, zero), jnp.where(lane >= FOX_HEAD_DIM, q, zero)]
    nfold = tq // LANES

    def fold(x, op):
        r = x[:, 0:LANES]
        for f in range(1, nfold):
            r = op(r, x[:, f * LANES:(f + 1) * LANES])
        return r

    def scores(j, off):
        ks = k_ref[0, pl.ds(off, tq), :]
        return _dot_nt(qs[j], ks) - c_ref[0, 0, pl.ds(j, 1), pl.ds(off, tq)]

    m_ref[...] = jnp.full(m_ref.shape, -jnp.inf, F32)
    l_ref[...] = jnp.zeros(l_ref.shape, F32)
    acc_ref[...] = jnp.zeros(acc_ref.shape, F32)

    def pass1(kb, carry):
        off = pl.multiple_of(kb * tq, tq)
        for j in range(2):
            s = scores(j, off)
            s_ref[j, :, pl.ds(off, tq)] = s
            m_ref[j] = jnp.maximum(m_ref[j], fold(s, jnp.maximum))
        return carry

    lax.fori_loop(0, qi, pass1, 0)
    offd = pl.multiple_of(qi * tq, tq)
    sd, ms = [], []
    for j in range(2):
        s = jnp.where(row >= col, scores(j, offd), -jnp.inf)
        sd.append(s)
        ms.append(jnp.max(jnp.maximum(m_ref[j], fold(s, jnp.maximum)), axis=-1, keepdims=True))

    def pass2(kb, carry):
        off = pl.multiple_of(kb * tq, tq)
        vs = v_ref[0, pl.ds(off, tq), :]
        for j in range(2):
            p = jnp.exp(s_ref[j, :, pl.ds(off, tq)] - ms[j])
            l_ref[j] += fold(p, jnp.add)
            acc_ref[j] += _dot(p.astype(BF16), vs)
        return carry

    lax.fori_loop(0, qi, pass2, 0)
    vsd = v_ref[0, pl.ds(offd, tq), :]
    outs = []
    for j in range(2):
        p = jnp.exp(sd[j] - ms[j])
        l = jnp.sum(l_ref[j] + fold(p, jnp.add), axis=-1, keepdims=True)
        outs.append((acc_ref[j] + _dot(p.astype(BF16), vsd)) / l)
    o = jnp.where(lane < FOX_HEAD_DIM, outs[0], outs[1])
    o = o * _sigmoid(g_ref[0].astype(F32))
    o_ref[0] = o.astype(o_ref.dtype)


def fox_attention(qkvg, c_rows, tq=512):
    b, t, d4 = qkvg.shape
    d = d4 // 4
    tq = min(tq, t)
    npair = d // LANES
    qspec = pl.BlockSpec((1, tq, LANES), lambda bi, p, i: (bi, i, p))
    return pl.pallas_call(
        functools.partial(_fox_body, tq=tq),
        out_shape=jax.ShapeDtypeStruct((b, t, d), BF16),
        grid=(b, npair, t // tq),
        in_specs=[qspec,
                  pl.BlockSpec((1, t, LANES), lambda bi, p, i: (bi, 0, npair + p)),
                  pl.BlockSpec((1, t, LANES), lambda bi, p, i: (bi, 0, 2 * npair + p)),
                  pl.BlockSpec((1, tq, LANES), lambda bi, p, i: (bi, i, 3 * npair + p)),
                  pl.BlockSpec((1, 1, 2, t), lambda bi, p, i: (bi, p, 0, 0))],
        out_specs=qspec,
        scratch_shapes=[pltpu.VMEM((2, tq, t), F32), pltpu.VMEM((2, tq, LANES), F32),
                        pltpu.VMEM((2, tq, LANES), F32), pltpu.VMEM((2, tq, LANES), F32)],
        compiler_params=_cparams(("parallel", "parallel", "arbitrary")),
        name="fox_attention",
    )(qkvg, qkvg, qkvg, qkvg, c_rows)


_LEVELS = tuple(CHUNK >> s for s in range(CHUNK.bit_length() - 1))


def _gla_consts():
    c = CHUNK
    t = np.arange(c)
    u = t[None, :]
    blocks = [u <= t[:, None]]
    masks = [np.eye(c, dtype=bool)]
    for lv in _LEVELS:
        half = lv // 2
        blk, pos = t // lv, t % lv
        r = blk * lv + half - 1
        upper = pos >= half
        qrow = upper[:, None] & (u > r[:, None]) & (u <= t[:, None])
        krow = (~upper)[:, None] & (u > t[:, None]) & (u <= r[:, None])
        blocks.append(qrow | krow)
        masks.append((blk[:, None] == blk[None, :]) & upper[:, None] & (~upper)[None, :])
    blocks.append(u > t[:, None])
    mall = np.concatenate(blocks, axis=0).astype(np.float32)
    mask = np.stack(masks).astype(np.float32)
    return jnp.asarray(mall, BF16), jnp.asarray(mask, F32)


def _gla_body(q_ref, k_ref, g_ref, v_ref, r_ref, gain_ref, mall_ref, mask_ref,
              o_ref, st_ref, *, nchunk, hg, dk, dv):
    c = CHUNK
    nl = len(_LEVELS)

    @pl.when(pl.program_id(2) == 0)
    def _():
        st_ref[...] = jnp.zeros_like(st_ref)

    def chunk(ci, carry):
        sl = pl.ds(pl.multiple_of(ci * c, c), c)
        g_hi, g_lo = _split2(g_ref[0, sl, :])
        mall = mall_ref[...]
        xall = jnp.exp(_dot(mall, g_hi) + _dot(mall, g_lo))
        for hh in range(hg):
            ks, vs = slice(hh * dk, (hh + 1) * dk), slice(hh * dv, (hh + 1) * dv)
            q = q_ref[0, sl, ks]
            k = k_ref[0, sl, ks]
            v = v_ref[0, sl, vs]
            x = xall[:, ks]
            scores = mask_ref[0] * _dot_nt(q.astype(BF16), k.astype(BF16))
            for lv in range(nl):
                z = x[(1 + lv) * c:(2 + lv) * c]
                scores = scores + mask_ref[1 + lv] * _dot_nt((q * z).astype(BF16), (k * z).astype(BF16))
            st = st_ref[hh]
            o = _dot(scores.astype(BF16), v) + _dot_nt((q * x[0:c]).astype(BF16), st.astype(BF16))
            kt = (k * x[(1 + nl) * c:(2 + nl) * c]).astype(BF16)
            st_ref[hh] = st * x[c - 1:c, :] + _dot_tn(v, kt)
            r = r_ref[0, sl, vs].astype(F32)
            y = _rms(o, gain_ref[hh]) * (r * _sigmoid(r))
            o_ref[0, sl, vs] = y.astype(o_ref.dtype)
        return carry

    lax.fori_loop(0, nchunk, chunk, 0, unroll=2)


def gated_linear_attention(q, k, g, v, r, gain, heads, dk, dv, hg=4, ts=512):
    b, t, _ = q.shape
    ts = min(ts, t)
    mall, mask = _gla_consts()
    qspec = pl.BlockSpec((1, ts, hg * dk), lambda bi, h, i: (bi, i, h))
    vspec = pl.BlockSpec((1, ts, hg * dv), lambda bi, h, i: (bi, i, h))
    return pl.pallas_call(
        functools.partial(_gla_body, nchunk=ts // CHUNK, hg=hg, dk=dk, dv=dv),
        out_shape=jax.ShapeDtypeStruct((b, t, heads * dv), BF16),
        grid=(b, heads // hg, t // ts),
        in_specs=[qspec, qspec, qspec, vspec, vspec,
                  pl.BlockSpec((hg, 1, dv), lambda bi, h, i: (h, 0, 0)),
                  pl.BlockSpec(mall.shape, lambda bi, h, i: (0, 0)),
                  pl.BlockSpec(mask.shape, lambda bi, h, i: (0, 0, 0))],
        out_specs=vspec,
        scratch_shapes=[pltpu.VMEM((hg, dv, dk), F32)],
        compiler_params=_cparams(("parallel", "parallel", "arbitrary")),
        name="gated_linear_attention",
    )(q, k, g, v, r, gain.astype(F32).reshape(heads, 1, dv), mall, mask)


def _peer_cand_layout():
    k = PEER_TOPK
    ab = [(0, b) for b in range(16)] + [(1, b) for b in range(8)]
    ab += [(2, b) if b < 5 else None for b in range(8)]
    ab += [(3, 0), (3, 1), (3, 2), (3, 3), (4, 0), (4, 1), (4, 2), None]
    ab += [(5, 0), (5, 1), (6, 0), (6, 1), (7, 0), (7, 1), None, None]
    ab += [(a, 0) for a in range(8, 16)]
    assert all(p is None or (p[0] + 1) * (p[1] + 1) <= k for p in ab)
    assert sum(p is not None for p in ab) == sum(k // (a + 1) for a in range(k))
    pos = np.array([PEER_POS_INVALID if p is None else p[0] * k + p[1] for p in ab], np.float32)
    return np.broadcast_to(pos[:, None], (pos.size, LANES)).copy()


PEER_POS_INVALID = 1024.0


def _extract_max(s, ids, big):
    m = jnp.max(s, axis=0, keepdims=True)
    ix = jnp.min(jnp.where(s == m, ids, big), axis=0, keepdims=True)
    return m, ix


def _topk_rows(s, ids, big, k):
    vals, sel = [], []
    for _ in range(k):
        m, ix = _extract_max(s, ids, big)
        vals.append(m)
        sel.append(ix)
        s = jnp.where(ids == ix, -jnp.inf, s)
    return jnp.concatenate(vals, axis=0), jnp.concatenate(sel, axis=0)


def _topk_keys(st, k):
    n = st.shape[0]
    a = jnp.concatenate([st[r:r + 8] for r in range(0, n, 16)], axis=0)
    b = jnp.concatenate([st[r + 8:r + 16] for r in range(0, n, 16)], axis=0)
    r = lax.broadcasted_iota(jnp.int32, a.shape, 0)
    ida = (((r >> 3) << 4) + (r & 7)).astype(F32)
    idb = ida + 8.0
    a_wins = a >= b
    win, idw = jnp.maximum(a, b), jnp.where(a_wins, ida, idb)
    los, idl = jnp.minimum(a, b), jnp.where(a_wins, idb, ida)
    vals, sel = [], []
    for _ in range(k):
        m, ix = _extract_max(win, idw, float(n))
        vals.append(m)
        sel.append(ix)
        hit = idw == ix
        win = jnp.where(hit, los, win)
        idw = jnp.where(hit, idl, idw)
        los = jnp.where(hit, -jnp.inf, los)
    return jnp.concatenate(vals, axis=0), jnp.concatenate(sel, axis=0)


def _route_head(q_ref, sk_ref, pos, h, tm):
    k = PEER_TOPK
    nk = PEER_NKEYS
    sub = lax.broadcasted_iota(jnp.int32, (8, tm), 0)

    def bc(x, r):
        return jnp.broadcast_to(x[r:r + 1], (8, tm))

    tops = []
    for p in range(2):
        hp = 2 * h + p
        qh = q_ref[:, pl.ds(pl.multiple_of(hp * nk, nk), nk)]
        st = _dot_nt(sk_ref[hp], qh)
        tops.append(_topk_keys(st, k))
    (s0, i0), (s1, i1) = tops
    lo1 = s1[0:8]
    cand = jnp.concatenate([
        bc(s0, 0) + lo1,
        bc(s0, 0) + s1[8:16],
        bc(s0, 1) + lo1,
        bc(s0, 2) + lo1,
        jnp.where(sub < 4, bc(s0, 3), bc(s0, 4)) + jnp.where(sub < 4, lo1, pltpu.roll(lo1, 4, 0)),
        jnp.where(sub < 2, bc(s0, 5), jnp.where(sub < 4, bc(s0, 6), bc(s0, 7)))
        + jnp.where((sub & 1) == 0, bc(s1, 0), bc(s1, 1)),
        s0[8:16] + bc(s1, 0)], axis=0)
    cand = jnp.where(pos < PEER_POS_INVALID, cand, -jnp.inf)
    best, bpos = _topk_rows(cand, pos, 2.0 * PEER_POS_INVALID, k)
    bpos = bpos.astype(jnp.int32)
    ra = bpos >> 4
    rb = bpos & (k - 1)
    isel = jnp.zeros((k, tm), F32)
    jsel = jnp.zeros((k, tm), F32)
    for a in range(k):
        isel = jnp.where(ra == a, i0[a:a + 1], isel)
        jsel = jnp.where(rb == a, i1[a:a + 1], jsel)
    e = jnp.exp(best - best[0:1])
    gates = e / jnp.sum(e, axis=0, keepdims=True)
    return isel, jsel, gates


W3_GROUP = 8
W3_BATCH = 2 * W3_GROUP
ROUTE_HEADS_PER_TRIP = 2


def _peer_route_weights_body(q_ref, sk_ref, pos_ref, o_ref, ri_ref, rj_ref, rg_ref, pt_ref,
                             w_ref, *, tm):
    k = PEER_TOPK
    nk = PEER_NKEYS
    ngroup = tm // LANES
    nbatch = tm // W3_BATCH
    per_head = nbatch // PEER_HEADS

    @pl.when(pl.program_id(0) == 0)
    def _():
        ri_ref[...] = jnp.zeros_like(ri_ref)
        rj_ref[...] = jnp.zeros_like(rj_ref)
        rg_ref[...] = jnp.zeros_like(rg_ref)

    for a, r_ref in enumerate((ri_ref, rj_ref, rg_ref)):
        for c in range(ngroup):
            pt_ref[a, c * LANES:(c + 1) * LANES, :] = r_ref[:, c * LANES:(c + 1) * LANES].T
    pos = jnp.concatenate([pos_ref[...]] * ngroup, axis=1)
    sub = lax.broadcasted_iota(jnp.int32, (nk, LANES), 0).astype(F32)

    def build(t):
        slot = lax.rem(t, 2 * ROUTE_HEADS_PER_TRIP * per_head)
        for half in range(2):
            tok0 = pl.multiple_of(t * W3_BATCH + half * W3_GROUP, W3_GROUP)
            it, jt, gt = (pt_ref[a, pl.ds(tok0, W3_GROUP), :] for a in range(3))
            for s in range(W3_GROUP):
                at = jnp.where(sub == it[s:s + 1], gt[s:s + 1], 0.0).astype(BF16)
                bt = jnp.where(sub == jt[s:s + 1], 1.0, 0.0).astype(BF16)
                w_ref[2 * slot + half, pl.ds(s, nk, stride=W3_GROUP), :] = _dot_nt(at, bt)

    def flush(t):
        slot = lax.rem(t, 2 * ROUTE_HEADS_PER_TRIP * per_head)
        r0 = pl.multiple_of(t * W3_BATCH, W3_BATCH)
        for i in range(nk):
            rows = jnp.concatenate([w_ref[2 * slot, i * W3_GROUP:(i + 1) * W3_GROUP, :],
                                    w_ref[2 * slot + 1, i * W3_GROUP:(i + 1) * W3_GROUP, :]], axis=0)
            o_ref[pl.ds(r0, W3_BATCH), i * nk:(i + 1) * nk] = rows.astype(o_ref.dtype)

    hpt = ROUTE_HEADS_PER_TRIP

    def do_heads(u, flush_prev):
        if flush_prev:
            for t in range(hpt * per_head):
                flush((u - 1) * hpt * per_head + t)
        for hh in range(hpt):
            h = u * hpt + hh
            isel, jsel, gates = _route_head(q_ref, sk_ref, pos, h, tm)
            rows = pl.ds(pl.multiple_of(h * k, k), k)
            ri_ref[rows, :] = isel
            rj_ref[rows, :] = jsel
            rg_ref[rows, :] = gates
            for t in range(per_head):
                build(h * per_head + t)

    def trip(u, carry):
        do_heads(u, True)
        return carry

    ntrip = PEER_HEADS // hpt
    do_heads(jnp.int32(0), False)
    lax.fori_loop(1, ntrip, trip, 0)
    for t in range(hpt * per_head):
        flush(jnp.int32((ntrip - 1) * hpt * per_head + t))


def peer_route_weights(q, sub_keys, tm=256):
    n = q.shape[0]
    tm = min(tm, n)
    nblk = n // tm
    ne = PEER_NKEYS * PEER_NKEYS
    npair = PEER_HEADS * PEER_TOPK
    pos = jnp.asarray(_peer_cand_layout())
    return pl.pallas_call(
        functools.partial(_peer_route_weights_body, tm=tm),
        out_shape=jax.ShapeDtypeStruct((n, ne), BF16),
        grid=(nblk + 1,),
        in_specs=[pl.BlockSpec((tm, q.shape[1]), lambda s: (jnp.minimum(s, nblk - 1), 0)),
                  pl.BlockSpec(sub_keys.shape, lambda s: (0, 0, 0)),
                  pl.BlockSpec(pos.shape, lambda s: (0, 0))],
        out_specs=pl.BlockSpec((tm, ne), lambda s: (jnp.maximum(s - 1, 0), 0)),
        scratch_shapes=[pltpu.VMEM((npair, tm), F32)] * 3 + [pltpu.VMEM((3, tm, npair), F32)]
        + [pltpu.VMEM((4 * ROUTE_HEADS_PER_TRIP * tm // (W3_BATCH * PEER_HEADS),
                       W3_GROUP * PEER_NKEYS, PEER_NKEYS), F32)],
        compiler_params=_cparams(("arbitrary",)),
        name="peer_route_weights",
    )(q, sub_keys, pos)


DENSE_SUBTILE = 1024


def _gelu(x):
    return 0.5 * x * (1.0 + lax.erf(x * (1.0 / math.sqrt(2.0))))


def _peer_dense_body(x_ref, ut_ref, v_ref, w_ref, h_ref, gn_ref, ho_ref, hn_ref, acc_ref, *, nj):
    j = pl.program_id(1)

    @pl.when(j == 0)
    def _():
        acc_ref[...] = jnp.zeros_like(acc_ref)

    x = x_ref[...]
    te = v_ref.shape[0]
    for s in range(te // min(te, DENSE_SUBTILE)):
        rows = slice(s * DENSE_SUBTILE, (s + 1) * DENSE_SUBTILE)
        hid = _dot(x, ut_ref[:, rows])
        a = (_gelu(hid) * w_ref[:, rows].astype(F32)).astype(BF16)
        acc_ref[...] += _dot(a, v_ref[rows, :])

    @pl.when(j == nj - 1)
    def _():
        hnew = h_ref[...] + acc_ref[...]
        ho_ref[...] = hnew
        hn_ref[...] = _rms(hnew, gn_ref[...]).astype(hn_ref.dtype)


def peer_dense(x, u, v, layer, w, h, gain_next, tm=1024, te=2048):
    n, d = x.shape
    ne = v.shape[1]
    tm = min(tm, n)
    te = min(te, ne)
    nj = ne // te
    rowspec_in = pl.BlockSpec((tm, d), lambda i, j: (i, 0), pipeline_mode=pl.Buffered(1))
    rowspec_out = pl.BlockSpec((tm, d), lambda i, j: (i, 0), pipeline_mode=pl.Buffered(1))
    return pl.pallas_call(
        functools.partial(_peer_dense_body, nj=nj),
        out_shape=[jax.ShapeDtypeStruct((n, d), F32), jax.ShapeDtypeStruct((n, d), BF16)],
        grid=(n // tm, nj),
        in_specs=[rowspec_in,
                  pl.BlockSpec((None, d, te), lambda i, j: (layer, 0, j)),
                  pl.BlockSpec((None, te, d), lambda i, j: (layer, j, 0)),
                  pl.BlockSpec((tm, te), lambda i, j: (i, j)),
                  rowspec_in,
                  pl.BlockSpec((1, d), lambda i, j: (0, 0))],
        out_specs=[rowspec_out, rowspec_out],
        scratch_shapes=[pltpu.VMEM((tm, d), F32)],
        compiler_params=_cparams(("parallel", "arbitrary")),
        name="peer_dense",
    )(x, u, v, w, h, gain_next.astype(F32).reshape(1, d))


def _fox_layer(hn, b, t, w_in, b_f, q_gain, k_gain):
    d = hn.shape[1]
    w = w_in.astype(BF16)
    qkvg = fox_proj(hn, w[:, 0:4 * d], q_gain, k_gain, FOX_HEAD_DIM ** -0.5, FOX_HEAD_DIM)
    lf = mm_logsig(hn, w[:, 4 * d:], b_f, 1.0)
    c = time_cumsum(lf.reshape(b, t, FOX_HEADS))
    c_rows = c.transpose(0, 2, 1).reshape(b, FOX_HEADS // 2, 2, t)
    o = fox_attention(qkvg.reshape(b, t, 4 * d), c_rows)
    return o.reshape(b * t, d)


def _gla_layer(hn, b, t, w_in, w_up, b_alpha, out_gain):
    d = hn.shape[1]
    gk = GLA_HEADS * GLA_DK
    w = w_in.astype(BF16)
    q = mm_plain(hn, w[:, 0:gk], F32, scale=GLA_DK ** -0.5)
    k = mm_plain(hn, w[:, gk:2 * gk], F32)
    v = mm_plain(hn, w[:, 2 * gk:2 * gk + d], BF16)
    r = mm_plain(hn, w[:, 2 * gk + d:2 * gk + 2 * d], BF16)
    low = mm_plain(hn, w[:, 2 * gk + 2 * d:], BF16)
    log_a = mm_logsig(low, w_up.astype(BF16), b_alpha, 1.0 / GLA_TAU)
    o = gated_linear_attention(q.reshape(b, t, gk), k.reshape(b, t, gk), log_a.reshape(b, t, gk),
                               v.reshape(b, t, d), r.reshape(b, t, d), out_gain,
                               GLA_HEADS, GLA_DK, GLA_DV)
    return o.reshape(b * t, d)


def _hgrn_layer(hn, b, t, layer, w_in, lb_logits, out_gain):
    d = hn.shape[1]
    w = w_in.astype(BF16)
    q = mm_plain(hn, w[:, 0:d], F32, scale=HGRN_DK ** -0.5)
    k, log_g = mm_hgrn_gate(hn, w[:, d:2 * d], lb_logits, layer)
    v = mm_plain(hn, w[:, 2 * d:3 * d], BF16)
    r = mm_plain(hn, w[:, 3 * d:4 * d], BF16)
    shp = (b, t, d)
    o = gated_linear_attention(q.reshape(shp), k.reshape(shp), log_g.reshape(shp),
                               v.reshape(shp), r.reshape(shp), out_gain,
                               HGRN_HEADS, HGRN_DK, HGRN_DV)
    return o.reshape(b * t, d)


def _peer_layer(h, hn, w_q, sub_keys, u, v, layer, gain_next):
    q = mm_plain(hn, w_q.astype(BF16), BF16)
    sk = sub_keys.astype(BF16).reshape(2 * PEER_HEADS, PEER_NKEYS, -1)
    w = peer_route_weights(q, sk)
    return peer_dense(hn, u, v, layer, w, h, gain_next)


def kernel(x, norm_mix, norm_ffn, fox_w_in, fox_b_f, fox_q_gain, fox_k_gain, fox_w_out,
           gla_w_in, gla_w_up, gla_b_alpha, gla_out_gain, gla_w_out,
           hgrn_w_in, hgrn_lb_logits, hgrn_out_gain, hgrn_w_out,
           peer_w_q, peer_sub_keys, peer_u, peer_v):
    b, t, d = x.shape
    depth = norm_mix.shape[0]
    h = x.reshape(b * t, d)
    hn = rmsnorm(h, norm_mix[0])
    u_all, v_all = peer_u.astype(BF16).transpose(0, 2, 1), peer_v.astype(BF16)
    for i in range(depth):
        m, j = i % N_MIXERS, i // N_MIXERS
        if m == 0:
            o = _fox_layer(hn, b, t, fox_w_in[j], fox_b_f[j], fox_q_gain[j], fox_k_gain[j])
            w_out = fox_w_out[j]
        elif m == 1:
            o = _gla_layer(hn, b, t, gla_w_in[j], gla_w_up[j], gla_b_alpha[j], gla_out_gain[j])
            w_out = gla_w_out[j]
        else:
            o = _hgrn_layer(hn, b, t, i, hgrn_w_in[j], hgrn_lb_logits, hgrn_out_gain[j])
            w_out = hgrn_w_out[j]
        h, hn = mm_res_norm(o, w_out.astype(BF16), h, norm_ffn[i])
        gain_next = norm_mix[(i + 1) % depth]
        h, hn = _peer_layer(h, hn, peer_w_q[i], peer_sub_keys[i], u_all, v_all, i, gain_next)
    return h.reshape(b, t, d)
```

```python
import functools
import math

import numpy as np
import jax
import jax.numpy as jnp
from jax import lax
from jax.experimental import pallas as pl
from jax.experimental.pallas import tpu as pltpu

F32 = jnp.float32
BF16 = jnp.bfloat16
EPS = 1e-6

DEPTH = 4
N_MIXERS = 3
CHUNK = 128
FOX_HEADS = 16
FOX_HEAD_DIM = 64
GLA_HEADS = 4
GLA_DK = 128
GLA_DV = 256
GLA_TAU = 16.0
HGRN_HEADS = 8
HGRN_DK = 128
HGRN_DV = 128
PEER_HEADS = 8
PEER_NKEYS = 128
PEER_TOPK = 16
LANES = 128
VMEM_LIMIT = 56 * 1024 * 1024


def _cparams(sem):
    return pltpu.CompilerParams(dimension_semantics=sem, vmem_limit_bytes=VMEM_LIMIT)


def _dot(a, b):
    return jnp.dot(a, b, preferred_element_type=F32)


def _dot_nt(a, b):
    return lax.dot_general(a, b, (((1,), (1,)), ((), ())), preferred_element_type=F32)


def _dot_tn(a, b):
    return lax.dot_general(a, b, (((0,), (0,)), ((), ())), preferred_element_type=F32)


def _split2(x):
    hi = x.astype(BF16)
    lo = (x - hi.astype(F32)).astype(BF16)
    return hi, lo


def _split3(x):
    hi = x.astype(BF16)
    r = x - hi.astype(F32)
    mid = r.astype(BF16)
    lo = (r - mid.astype(F32)).astype(BF16)
    return hi, mid, lo


def _log_sigmoid(x):
    return jnp.minimum(x, 0.0) - jnp.log1p(jnp.exp(-jnp.abs(x)))


def _sigmoid(x):
    return 1.0 / (1.0 + jnp.exp(-x))


def _rms(x, gain):
    ms = jnp.mean(x * x, axis=-1, keepdims=True)
    return x * lax.rsqrt(ms + EPS) * gain


def _rmsnorm_body(x_ref, g_ref, o_ref):
    o_ref[...] = _rms(x_ref[...], g_ref[...]).astype(o_ref.dtype)


def rmsnorm(x, gain, tm=512):
    n, d = x.shape
    tm = min(tm, n)
    return pl.pallas_call(
        _rmsnorm_body,
        out_shape=jax.ShapeDtypeStruct((n, d), BF16),
        grid=(n // tm,),
        in_specs=[pl.BlockSpec((tm, d), lambda i: (i, 0)),
                  pl.BlockSpec((1, d), lambda i: (0, 0))],
        out_specs=pl.BlockSpec((tm, d), lambda i: (i, 0)),
        compiler_params=_cparams(("parallel",)),
        name="rmsnorm",
    )(x, gain.reshape(1, d))


def _mm_call(body, x, w, extras, extra_specs, out_dtypes, tm, tn, name):
    n, k = x.shape
    m = w.shape[1]
    tm = min(tm, n)
    tn = min(tn, m)
    outs = [jax.ShapeDtypeStruct((n, m), dt) for dt in out_dtypes]
    return pl.pallas_call(
        body,
        out_shape=outs,
        grid=(n // tm, m // tn),
        in_specs=[pl.BlockSpec((tm, k), lambda i, j: (i, 0)),
                  pl.BlockSpec((k, tn), lambda i, j: (0, j))] + extra_specs,
        out_specs=[pl.BlockSpec((tm, tn), lambda i, j: (i, j)) for _ in outs],
        compiler_params=_cparams(("parallel", "arbitrary")),
        name=name,
    )(x, w, *extras)


def _mm_plain_body(x_ref, w_ref, o_ref, *, scale):
    acc = _dot(x_ref[...], w_ref[...])
    if scale != 1.0:
        acc = acc * scale
    o_ref[...] = acc.astype(o_ref.dtype)


def mm_plain(x, w, out_dtype, scale=1.0, tm=1024, tn=1024):
    return _mm_call(functools.partial(_mm_plain_body, scale=scale), x, w, [], [],
                    [out_dtype], tm, tn, "mm_plain")[0]


MXU_DIM = 256


def _fox_proj_body(x_ref, w_ref, bd_ref, g_ref, o_ref, *, hd):
    j = pl.program_id(1)
    acc = _dot(x_ref[...], w_ref[...])

    @pl.when(j < 2)
    def _():
        for c in range(acc.shape[1] // MXU_DIM):
            cols = slice(c * MXU_DIM, (c + 1) * MXU_DIM)
            a = acc[:, cols]
            ms = _dot((a * a).astype(BF16), bd_ref[...]) * (1.0 / hd)
            o_ref[:, cols] = (a * lax.rsqrt(ms + EPS) * g_ref[0, :, cols]).astype(o_ref.dtype)

    @pl.when(j >= 2)
    def _():
        o_ref[...] = acc.astype(o_ref.dtype)


def fox_proj(x, w, q_gain, k_gain, scale, hd, tm=1024):
    n, d = x.shape
    tm = min(tm, n)
    blk = np.arange(MXU_DIM) // hd
    bd = jnp.asarray(blk[:, None] == blk[None, :], BF16)
    g = jnp.stack([jnp.tile(q_gain.astype(F32) * scale, d // hd),
                   jnp.tile(k_gain.astype(F32), d // hd)]).reshape(2, 1, d)
    return pl.pallas_call(
        functools.partial(_fox_proj_body, hd=hd),
        out_shape=jax.ShapeDtypeStruct((n, 4 * d), BF16),
        grid=(n // tm, 4),
        in_specs=[pl.BlockSpec((tm, d), lambda i, j: (i, 0)),
                  pl.BlockSpec((d, d), lambda i, j: (0, j)),
                  pl.BlockSpec((MXU_DIM, MXU_DIM), lambda i, j: (0, 0)),
                  pl.BlockSpec((1, 1, d), lambda i, j: (jnp.minimum(j, 1), 0, 0))],
        out_specs=pl.BlockSpec((tm, d), lambda i, j: (i, j)),
        compiler_params=_cparams(("parallel", "arbitrary")),
        name="fox_proj",
    )(x, w, bd, g)


def _mm_logsig_body(x_ref, w_ref, b_ref, o_ref, *, mult):
    acc = _dot(x_ref[...], w_ref[...]) + b_ref[...]
    o_ref[...] = (_log_sigmoid(acc) * mult).astype(o_ref.dtype)


def mm_logsig(x, w, bias, mult, tm=512, tn=512):
    m = w.shape[1]
    tn = min(tn, m)
    return _mm_call(functools.partial(_mm_logsig_body, mult=mult), x, w,
                    [bias.astype(F32).reshape(1, m)],
                    [pl.BlockSpec((1, tn), lambda i, j: (0, j))],
                    [F32], tm, tn, "mm_logsig")[0]


def _mm_hgrn_gate_body(x_ref, w_ref, lbl_ref, k_ref, lg_ref, *, layer):
    acc = _dot(x_ref[...], w_ref[...])
    logits = lbl_ref[...]
    e = jnp.exp(logits - jnp.max(logits, axis=0, keepdims=True))
    p = e / jnp.sum(e, axis=0, keepdims=True)
    lb = jnp.zeros_like(p[0:1])
    for u in range(1, layer + 1):
        lb = lb + p[u:u + 1]
    gate = lb + (1.0 - lb) * _sigmoid(acc)
    k_ref[...] = (1.0 - gate).astype(k_ref.dtype)
    lg_ref[...] = jnp.log(gate).astype(lg_ref.dtype)


def mm_hgrn_gate(x, w, lb_logits, layer, tm=512, tn=512):
    m = w.shape[1]
    tn = min(tn, m)
    depth = lb_logits.shape[0]
    return _mm_call(functools.partial(_mm_hgrn_gate_body, layer=layer), x, w,
                    [lb_logits.astype(F32)],
                    [pl.BlockSpec((depth, tn), lambda i, j: (0, j))],
                    [F32, F32], tm, tn, "mm_hgrn_gate")


def _mm_res_norm_body(x_ref, w_ref, h_ref, g_ref, ho_ref, hn_ref):
    hnew = h_ref[...] + _dot(x_ref[...], w_ref[...])
    ho_ref[...] = hnew
    hn_ref[...] = _rms(hnew, g_ref[...]).astype(hn_ref.dtype)


def mm_res_norm(x, w, h, gain, tm=512):
    n, d = h.shape
    return _mm_call(_mm_res_norm_body, x, w, [h, gain.astype(F32).reshape(1, d)],
                    [pl.BlockSpec((min(tm, n), d), lambda i, j: (i, 0)),
                     pl.BlockSpec((1, d), lambda i, j: (0, 0))],
                    [F32, BF16], tm, d, "mm_res_norm")


CUM_BLOCK = 256


def _cumsum_body(x_ref, o_ref, *, t):
    nb = t // CUM_BLOCK
    r = lax.broadcasted_iota(jnp.int32, (CUM_BLOCK, CUM_BLOCK), 0)
    c = lax.broadcasted_iota(jnp.int32, (CUM_BLOCK, CUM_BLOCK), 1)
    tril = jnp.where(r >= c, 1.0, 0.0).astype(BF16)
    carry = jnp.zeros((1, x_ref.shape[-1]), F32)
    for b in range(nb):
        x = x_ref[0, b * CUM_BLOCK:(b + 1) * CUM_BLOCK, :]
        hi, mid, lo = _split3(x)
        cs = _dot(tril, hi) + _dot(tril, mid) + _dot(tril, lo) + carry
        o_ref[0, b * CUM_BLOCK:(b + 1) * CUM_BLOCK, :] = cs
        carry = cs[CUM_BLOCK - 1:CUM_BLOCK, :]


def time_cumsum(x):
    b, t, h = x.shape
    return pl.pallas_call(
        functools.partial(_cumsum_body, t=t),
        out_shape=jax.ShapeDtypeStruct((b, t, h), F32),
        grid=(b,),
        in_specs=[pl.BlockSpec((1, t, h), lambda i: (i, 0, 0))],
        out_specs=pl.BlockSpec((1, t, h), lambda i: (i, 0, 0)),
        compiler_params=_cparams(("parallel",)),
        name="time_cumsum",
    )(x)


def _fox_body(q_ref, k_ref, v_ref, g_ref, c_ref, o_ref, s_ref, m_ref, l_ref, acc_ref, *, tq):
    qi = pl.program_id(2)
    q = q_ref[0]
    lane = lax.broadcasted_iota(jnp.int32, (tq, LANES), 1)
    row = lax.broadcasted_iota(jnp.int32, (tq, tq), 0)
    col = lax.broadcasted_iota(jnp.int32, (tq, tq), 1)
    zero = jnp.zeros_like(q)
    qs = [jnp.where(lane < FOX_HEAD_DIM, q, zero), jnp.where(lane >= FOX_HEAD_DIM, q, zero)]
    nfold = tq // LANES

    def fold(x, op):
        r = x[:, 0:LANES]
        for f in range(1, nfold):
            r = op(r, x[:, f * LANES:(f + 1) * LANES])
        return r

    def scores(j, off):
        ks = k_ref[0, pl.ds(off, tq), :]
        return _dot_nt(qs[j], ks) - c_ref[0, 0, pl.ds(j, 1), pl.ds(off, tq)]

    m_ref[...] = jnp.full(m_ref.shape, -jnp.inf, F32)
    l_ref[...] = jnp.zeros(l_ref.shape, F32)
    acc_ref[...] = jnp.zeros(acc_ref.shape, F32)

    def pass1(kb, carry):
        off = pl.multiple_of(kb * tq, tq)
        for j in range(2):
            s = scores(j, off)
            s_ref[j, :, pl.ds(off, tq)] = s
            m_ref[j] = jnp.maximum(m_ref[j], fold(s, jnp.maximum))
        return carry

    lax.fori_loop(0, qi, pass1, 0)
    offd = pl.multiple_of(qi * tq, tq)
    sd, ms = [], []
    for j in range(2):
        s = jnp.where(row >= col, scores(j, offd), -jnp.inf)
        sd.append(s)
        ms.append(jnp.max(jnp.maximum(m_ref[j], fold(s, jnp.maximum)), axis=-1, keepdims=True))

    def pass2(kb, carry):
        off = pl.multiple_of(kb * tq, tq)
        vs = v_ref[0, pl.ds(off, tq), :]
        for j in range(2):
            p = jnp.exp(s_ref[j, :, pl.ds(off, tq)] - ms[j])
            l_ref[j] += fold(p, jnp.add)
            acc_ref[j] += _dot(p.astype(BF16), vs)
        return carry

    lax.fori_loop(0, qi, pass2, 0)
    vsd = v_ref[0, pl.ds(offd, tq), :]
    outs = []
    for j in range(2):
        p = jnp.exp(sd[j] - ms[j])
        l = jnp.sum(l_ref[j] + fold(p, jnp.add), axis=-1, keepdims=True)
        outs.append((acc_ref[j] + _dot(p.astype(BF16), vsd)) / l)
    o = jnp.where(lane < FOX_HEAD_DIM, outs[0], outs[1])
    o = o * _sigmoid(g_ref[0].astype(F32))
    o_ref[0] = o.astype(o_ref.dtype)


def fox_attention(qkvg, c_rows, tq=512):
    b, t, d4 = qkvg.shape
    d = d4 // 4
    tq = min(tq, t)
    npair = d // LANES
    qspec = pl.BlockSpec((1, tq, LANES), lambda bi, p, i: (bi, i, p))
    return pl.pallas_call(
        functools.partial(_fox_body, tq=tq),
        out_shape=jax.ShapeDtypeStruct((b, t, d), BF16),
        grid=(b, npair, t // tq),
        in_specs=[qspec,
                  pl.BlockSpec((1, t, LANES), lambda bi, p, i: (bi, 0, npair + p)),
                  pl.BlockSpec((1, t, LANES), lambda bi, p, i: (bi, 0, 2 * npair + p)),
                  pl.BlockSpec((1, tq, LANES), lambda bi, p, i: (bi, i, 3 * npair + p)),
                  pl.BlockSpec((1, 1, 2, t), lambda bi, p, i: (bi, p, 0, 0))],
        out_specs=qspec,
        scratch_shapes=[pltpu.VMEM((2, tq, t), F32), pltpu.VMEM((2, tq, LANES), F32),
                        pltpu.VMEM((2, tq, LANES), F32), pltpu.VMEM((2, tq, LANES), F32)],
        compiler_params=_cparams(("parallel", "parallel", "arbitrary")),
        name="fox_attention",
    )(qkvg, qkvg, qkvg, qkvg, c_rows)


_LEVELS = tuple(CHUNK >> s for s in range(CHUNK.bit_length() - 1))


def _gla_consts():
    c = CHUNK
    t = np.arange(c)
    u = t[None, :]
    blocks = [u <= t[:, None]]
    masks = [np.eye(c, dtype=bool)]
    for lv in _LEVELS:
        half = lv // 2
        blk, pos = t // lv, t % lv
        r = blk * lv + half - 1
        upper = pos >= half
        qrow = upper[:, None] & (u > r[:, None]) & (u <= t[:, None])
        krow = (~upper)[:, None] & (u > t[:, None]) & (u <= r[:, None])
        blocks.append(qrow | krow)
        masks.append((blk[:, None] == blk[None, :]) & upper[:, None] & (~upper)[None, :])
    blocks.append(u > t[:, None])
    mall = np.concatenate(blocks, axis=0).astype(np.float32)
    mask = np.stack(masks).astype(np.float32)
    return jnp.asarray(mall, BF16), jnp.asarray(mask, F32)


def _gla_body(q_ref, k_ref, g_ref, v_ref, r_ref, gain_ref, mall_ref, mask_ref,
              o_ref, st_ref, *, nchunk, hg, dk, dv):
    c = CHUNK
    nl = len(_LEVELS)

    @pl.when(pl.program_id(2) == 0)
    def _():
        st_ref[...] = jnp.zeros_like(st_ref)

    def chunk(ci, carry):
        sl = pl.ds(pl.multiple_of(ci * c, c), c)
        g_hi, g_lo = _split2(g_ref[0, sl, :])
        mall = mall_ref[...]
        xall = jnp.exp(_dot(mall, g_hi) + _dot(mall, g_lo))
        for hh in range(hg):
            ks, vs = slice(hh * dk, (hh + 1) * dk), slice(hh * dv, (hh + 1) * dv)
            q = q_ref[0, sl, ks]
            k = k_ref[0, sl, ks]
            v = v_ref[0, sl, vs]
            x = xall[:, ks]
            scores = mask_ref[0] * _dot_nt(q.astype(BF16), k.astype(BF16))
            for lv in range(nl):
                z = x[(1 + lv) * c:(2 + lv) * c]
                scores = scores + mask_ref[1 + lv] * _dot_nt((q * z).astype(BF16), (k * z).astype(BF16))
            st = st_ref[hh]
            o = _dot(scores.astype(BF16), v) + _dot_nt((q * x[0:c]).astype(BF16), st.astype(BF16))
            kt = (k * x[(1 + nl) * c:(2 + nl) * c]).astype(BF16)
            st_ref[hh] = st * x[c - 1:c, :] + _dot_tn(v, kt)
            r = r_ref[0, sl, vs].astype(F32)
            y = _rms(o, gain_ref[hh]) * (r * _sigmoid(r))
            o_ref[0, sl, vs] = y.astype(o_ref.dtype)
        return carry

    lax.fori_loop(0, nchunk, chunk, 0, unroll=2)


def gated_linear_attention(q, k, g, v, r, gain, heads, dk, dv, hg=4, ts=512):
    b, t, _ = q.shape
    ts = min(ts, t)
    mall, mask = _gla_consts()
    qspec = pl.BlockSpec((1, ts, hg * dk), lambda bi, h, i: (bi, i, h))
    vspec = pl.BlockSpec((1, ts, hg * dv), lambda bi, h, i: (bi, i, h))
    return pl.pallas_call(
        functools.partial(_gla_body, nchunk=ts // CHUNK, hg=hg, dk=dk, dv=dv),
        out_shape=jax.ShapeDtypeStruct((b, t, heads * dv), BF16),
        grid=(b, heads // hg, t // ts),
        in_specs=[qspec, qspec, qspec, vspec, vspec,
                  pl.BlockSpec((hg, 1, dv), lambda bi, h, i: (h, 0, 0)),
                  pl.BlockSpec(mall.shape, lambda bi, h, i: (0, 0)),
                  pl.BlockSpec(mask.shape, lambda bi, h, i: (0, 0, 0))],
        out_specs=vspec,
        scratch_shapes=[pltpu.VMEM((hg, dv, dk), F32)],
        compiler_params=_cparams(("parallel", "parallel", "arbitrary")),
        name="gated_linear_attention",
    )(q, k, g, v, r, gain.astype(F32).reshape(heads, 1, dv), mall, mask)


def _peer_cand_layout():
    k = PEER_TOPK
    ab = [(0, b) for b in range(16)] + [(1, b) for b in range(8)]
    ab += [(2, b) if b < 5 else None for b in range(8)]
    ab += [(3, 0), (3, 1), (3, 2), (3, 3), (4, 0), (4, 1), (4, 2), None]
    ab += [(5, 0), (5, 1), (6, 0), (6, 1), (7, 0), (7, 1), None, None]
    ab += [(a, 0) for a in range(8, 16)]
    assert all(p is None or (p[0] + 1) * (p[1] + 1) <= k for p in ab)
    assert sum(p is not None for p in ab) == sum(k // (a + 1) for a in range(k))
    pos = np.array([PEER_POS_INVALID if p is None else p[0] * k + p[1] for p in ab], np.float32)
    return np.broadcast_to(pos[:, None], (pos.size, LANES)).copy()


PEER_POS_INVALID = 1024.0


def _extract_max(s, ids, big):
    m = jnp.max(s, axis=0, keepdims=True)
    ix = jnp.min(jnp.where(s == m, ids, big), axis=0, keepdims=True)
    return m, ix


def _topk_rows(s, ids, big, k):
    vals, sel = [], []
    for _ in range(k):
        m, ix = _extract_max(s, ids, big)
        vals.append(m)
        sel.append(ix)
        s = jnp.where(ids == ix, -jnp.inf, s)
    return jnp.concatenate(vals, axis=0), jnp.concatenate(sel, axis=0)


def _topk_keys(st, k):
    n = st.shape[0]
    a = jnp.concatenate([st[r:r + 8] for r in range(0, n, 16)], axis=0)
    b = jnp.concatenate([st[r + 8:r + 16] for r in range(0, n, 16)], axis=0)
    r = lax.broadcasted_iota(jnp.int32, a.shape, 0)
    ida = (((r >> 3) << 4) + (r & 7)).astype(F32)
    idb = ida + 8.0
    a_wins = a >= b
    win, idw = jnp.maximum(a, b), jnp.where(a_wins, ida, idb)
    los, idl = jnp.minimum(a, b), jnp.where(a_wins, idb, ida)
    vals, sel = [], []
    for _ in range(k):
        m, ix = _extract_max(win, idw, float(n))
        vals.append(m)
        sel.append(ix)
        hit = idw == ix
        win = jnp.where(hit, los, win)
        idw = jnp.where(hit, idl, idw)
        los = jnp.where(hit, -jnp.inf, los)
    return jnp.concatenate(vals, axis=0), jnp.concatenate(sel, axis=0)


def _route_head(q_ref, sk_ref, pos, h, tm):
    k = PEER_TOPK
    nk = PEER_NKEYS
    sub = lax.broadcasted_iota(jnp.int32, (8, tm), 0)

    def bc(x, r):
        return jnp.broadcast_to(x[r:r + 1], (8, tm))

    tops = []
    for p in range(2):
        hp = 2 * h + p
        qh = q_ref[:, pl.ds(pl.multiple_of(hp * nk, nk), nk)]
        st = _dot_nt(sk_ref[hp], qh)
        tops.append(_topk_keys(st, k))
    (s0, i0), (s1, i1) = tops
    lo1 = s1[0:8]
    cand = jnp.concatenate([
        bc(s0, 0) + lo1,
        bc(s0, 0) + s1[8:16],
        bc(s0, 1) + lo1,
        bc(s0, 2) + lo1,
        jnp.where(sub < 4, bc(s0, 3), bc(s0, 4)) + jnp.where(sub < 4, lo1, pltpu.roll(lo1, 4, 0)),
        jnp.where(sub < 2, bc(s0, 5), jnp.where(sub < 4, bc(s0, 6), bc(s0, 7)))
        + jnp.where((sub & 1) == 0, bc(s1, 0), bc(s1, 1)),
        s0[8:16] + bc(s1, 0)], axis=0)
    cand = jnp.where(pos < PEER_POS_INVALID, cand, -jnp.inf)
    best, bpos = _topk_rows(cand, pos, 2.0 * PEER_POS_INVALID, k)
    bpos = bpos.astype(jnp.int32)
    ra = bpos >> 4
    rb = bpos & (k - 1)
    isel = jnp.zeros((k, tm), F32)
    jsel = jnp.zeros((k, tm), F32)
    for a in range(k):
        isel = jnp.where(ra == a, i0[a:a + 1], isel)
        jsel = jnp.where(rb == a, i1[a:a + 1], jsel)
    e = jnp.exp(best - best[0:1])
    gates = e / jnp.sum(e, axis=0, keepdims=True)
    return isel, jsel, gates


W3_GROUP = 8
W3_BATCH = 2 * W3_GROUP
ROUTE_HEADS_PER_TRIP = 2


def _peer_route_weights_body(x_ref, xn_ref, wq_ref, sk_ref, pos_ref, o_ref, q_ref, ri_ref, rj_ref,
                             rg_ref, pt_ref, w_ref, *, tm):
    k = PEER_TOPK
    nk = PEER_NKEYS
    ngroup = tm // LANES
    nbatch = tm // W3_BATCH
    per_head = nbatch // PEER_HEADS

    step = pl.program_id(0)
    cur = lax.rem(step, 2)
    q_cur = q_ref.at[cur]

    @pl.when(step == 0)
    def _():
        q_ref[0] = _dot(x_ref[...], wq_ref[...]).astype(q_ref.dtype)
        ri_ref[...] = jnp.zeros_like(ri_ref)
        rj_ref[...] = jnp.zeros_like(rj_ref)
        rg_ref[...] = jnp.zeros_like(rg_ref)

    for a, r_ref in enumerate((ri_ref, rj_ref, rg_ref)):
        for c in range(ngroup):
            pt_ref[a, c * LANES:(c + 1) * LANES, :] = r_ref[:, c * LANES:(c + 1) * LANES].T
    pos = jnp.concatenate([pos_ref[...]] * ngroup, axis=1)
    sub = lax.broadcasted_iota(jnp.int32, (nk, LANES), 0).astype(F32)

    def build(t):
        slot = lax.rem(t, 2 * ROUTE_HEADS_PER_TRIP * per_head)
        for half in range(2):
            tok0 = pl.multiple_of(t * W3_BATCH + half * W3_GROUP, W3_GROUP)
            it, jt, gt = (pt_ref[a, pl.ds(tok0, W3_GROUP), :] for a in range(3))
            for s in range(W3_GROUP):
                at = jnp.where(sub == it[s:s + 1], gt[s:s + 1], 0.0).astype(BF16)
                bt = jnp.where(sub == jt[s:s + 1], 1.0, 0.0).astype(BF16)
                w_ref[2 * slot + half, pl.ds(s, nk, stride=W3_GROUP), :] = _dot_nt(at, bt)

    def flush(t):
        slot = lax.rem(t, 2 * ROUTE_HEADS_PER_TRIP * per_head)
        r0 = pl.multiple_of(t * W3_BATCH, W3_BATCH)
        for i in range(nk):
            rows = jnp.concatenate([w_ref[2 * slot, i * W3_GROUP:(i + 1) * W3_GROUP, :],
                                    w_ref[2 * slot + 1, i * W3_GROUP:(i + 1) * W3_GROUP, :]], axis=0)
            o_ref[pl.ds(r0, W3_BATCH), i * nk:(i + 1) * nk] = rows.astype(o_ref.dtype)

    hpt = ROUTE_HEADS_PER_TRIP
    ntrip = PEER_HEADS // hpt
    qcols = q_ref.shape[2] // ntrip

    def do_heads(u, flush_prev):
        if flush_prev:
            for t in range(hpt * per_head):
                flush((u - 1) * hpt * per_head + t)
        for hh in range(hpt):
            h = u * hpt + hh
            isel, jsel, gates = _route_head(q_cur, sk_ref, pos, h, tm)
            rows = pl.ds(pl.multiple_of(h * k, k), k)
            ri_ref[rows, :] = isel
            rj_ref[rows, :] = jsel
            rg_ref[rows, :] = gates
            for t in range(per_head):
                build(h * per_head + t)
        cols = pl.ds(pl.multiple_of(u * qcols, qcols), qcols)
        q_ref[1 - cur, :, cols] = _dot(xn_ref[...], wq_ref[:, cols]).astype(q_ref.dtype)

    def trip(u, carry):
        do_heads(u, True)
        return carry

    do_heads(jnp.int32(0), False)
    lax.fori_loop(1, ntrip, trip, 0)
    for t in range(hpt * per_head):
        flush(jnp.int32((ntrip - 1) * hpt * per_head + t))


def peer_route_weights(x, w_q, sub_keys, tm=256):
    n, d = x.shape
    dq = w_q.shape[1]
    tm = min(tm, n)
    nblk = n // tm
    ne = PEER_NKEYS * PEER_NKEYS
    npair = PEER_HEADS * PEER_TOPK
    pos = jnp.asarray(_peer_cand_layout())
    return pl.pallas_call(
        functools.partial(_peer_route_weights_body, tm=tm),
        out_shape=jax.ShapeDtypeStruct((n, ne), BF16),
        grid=(nblk + 1,),
        in_specs=[pl.BlockSpec((tm, d), lambda s: (jnp.minimum(s, nblk - 1), 0)),
                  pl.BlockSpec((tm, d), lambda s: (jnp.minimum(s + 1, nblk - 1), 0)),
                  pl.BlockSpec((d, dq), lambda s: (0, 0), pipeline_mode=pl.Buffered(1)),
                  pl.BlockSpec(sub_keys.shape, lambda s: (0, 0, 0)),
                  pl.BlockSpec(pos.shape, lambda s: (0, 0))],
        out_specs=pl.BlockSpec((tm, ne), lambda s: (jnp.maximum(s - 1, 0), 0)),
        scratch_shapes=[pltpu.VMEM((2, tm, dq), BF16)]
        + [pltpu.VMEM((npair, tm), F32)] * 3 + [pltpu.VMEM((3, tm, npair), F32)]
        + [pltpu.VMEM((4 * ROUTE_HEADS_PER_TRIP * tm // (W3_BATCH * PEER_HEADS),
                       W3_GROUP * PEER_NKEYS, PEER_NKEYS), F32)],
        compiler_params=_cparams(("arbitrary",)),
        name="peer_route_weights",
    )(x, x, w_q, sub_keys, pos)


DENSE_SUBTILE = 1024


def _gelu(x):
    return 0.5 * x * (1.0 + lax.erf(x * (1.0 / math.sqrt(2.0))))


def _peer_dense_body(x_ref, u_ref, v_ref, w_ref, h_ref, gn_ref, ho_ref, hn_ref, acc_ref, *, nj):
    j = pl.program_id(1)

    @pl.when(j == 0)
    def _():
        acc_ref[...] = jnp.zeros_like(acc_ref)

    x = x_ref[...]
    te = u_ref.shape[0]
    for s in range(te // min(te, DENSE_SUBTILE)):
        rows = slice(s * DENSE_SUBTILE, (s + 1) * DENSE_SUBTILE)
        hid = _dot_nt(x, u_ref[rows, :])
        a = (_gelu(hid) * w_ref[:, rows].astype(F32)).astype(BF16)
        acc_ref[...] += _dot(a, v_ref[rows, :])

    @pl.when(j == nj - 1)
    def _():
        hnew = h_ref[...] + acc_ref[...]
        ho_ref[...] = hnew
        hn_ref[...] = _rms(hnew, gn_ref[...]).astype(hn_ref.dtype)


def peer_dense(x, u, v, layer, w, h, gain_next, tm=1024, te=2048):
    n, d = x.shape
    ne = u.shape[1]
    tm = min(tm, n)
    te = min(te, ne)
    nj = ne // te
    rowspec_in = pl.BlockSpec((tm, d), lambda i, j: (i, 0), pipeline_mode=pl.Buffered(1))
    rowspec_out = pl.BlockSpec((tm, d), lambda i, j: (i, 0), pipeline_mode=pl.Buffered(1))
    return pl.pallas_call(
        functools.partial(_peer_dense_body, nj=nj),
        out_shape=[jax.ShapeDtypeStruct((n, d), F32), jax.ShapeDtypeStruct((n, d), BF16)],
        grid=(n // tm, nj),
        in_specs=[rowspec_in,
                  pl.BlockSpec((None, te, d), lambda i, j: (layer, j, 0)),
                  pl.BlockSpec((None, te, d), lambda i, j: (layer, j, 0)),
                  pl.BlockSpec((tm, te), lambda i, j: (i, j)),
                  rowspec_in,
                  pl.BlockSpec((1, d), lambda i, j: (0, 0))],
        out_specs=[rowspec_out, rowspec_out],
        scratch_shapes=[pltpu.VMEM((tm, d), F32)],
        compiler_params=_cparams(("parallel", "arbitrary")),
        name="peer_dense",
    )(x, u, v, w, h, gain_next.astype(F32).reshape(1, d))


def _fox_layer(hn, b, t, w_in, b_f, q_gain, k_gain):
    d = hn.shape[1]
    w = w_in.astype(BF16)
    qkvg = fox_proj(hn, w[:, 0:4 * d], q_gain, k_gain, FOX_HEAD_DIM ** -0.5, FOX_HEAD_DIM)
    lf = mm_logsig(hn, w[:, 4 * d:], b_f, 1.0)
    c = time_cumsum(lf.reshape(b, t, FOX_HEADS))
    c_rows = c.transpose(0, 2, 1).reshape(b, FOX_HEADS // 2, 2, t)
    o = fox_attention(qkvg.reshape(b, t, 4 * d), c_rows)
    return o.reshape(b * t, d)


def _gla_layer(hn, b, t, w_in, w_up, b_alpha, out_gain):
    d = hn.shape[1]
    gk = GLA_HEADS * GLA_DK
    w = w_in.astype(BF16)
    q = mm_plain(hn, w[:, 0:gk], F32, scale=GLA_DK ** -0.5)
    k = mm_plain(hn, w[:, gk:2 * gk], F32)
    v = mm_plain(hn, w[:, 2 * gk:2 * gk + d], BF16)
    r = mm_plain(hn, w[:, 2 * gk + d:2 * gk + 2 * d], BF16)
    low = mm_plain(hn, w[:, 2 * gk + 2 * d:], BF16)
    log_a = mm_logsig(low, w_up.astype(BF16), b_alpha, 1.0 / GLA_TAU)
    o = gated_linear_attention(q.reshape(b, t, gk), k.reshape(b, t, gk), log_a.reshape(b, t, gk),
                               v.reshape(b, t, d), r.reshape(b, t, d), out_gain,
                               GLA_HEADS, GLA_DK, GLA_DV)
    return o.reshape(b * t, d)


def _hgrn_layer(hn, b, t, layer, w_in, lb_logits, out_gain):
    d = hn.shape[1]
    w = w_in.astype(BF16)
    q = mm_plain(hn, w[:, 0:d], F32, scale=HGRN_DK ** -0.5)
    k, log_g = mm_hgrn_gate(hn, w[:, d:2 * d], lb_logits, layer)
    v = mm_plain(hn, w[:, 2 * d:3 * d], BF16)
    r = mm_plain(hn, w[:, 3 * d:4 * d], BF16)
    shp = (b, t, d)
    o = gated_linear_attention(q.reshape(shp), k.reshape(shp), log_g.reshape(shp),
                               v.reshape(shp), r.reshape(shp), out_gain,
                               HGRN_HEADS, HGRN_DK, HGRN_DV)
    return o.reshape(b * t, d)


def _peer_layer(h, hn, w_q, sub_keys, u, v, layer, gain_next):
    sk = sub_keys.astype(BF16).reshape(2 * PEER_HEADS, PEER_NKEYS, -1)
    w = peer_route_weights(hn, w_q.astype(BF16), sk)
    return peer_dense(hn, u, v, layer, w, h, gain_next)


def kernel(x, norm_mix, norm_ffn, fox_w_in, fox_b_f, fox_q_gain, fox_k_gain, fox_w_out,
           gla_w_in, gla_w_up, gla_b_alpha, gla_out_gain, gla_w_out,
           hgrn_w_in, hgrn_lb_logits, hgrn_out_gain, hgrn_w_out,
           peer_w_q, peer_sub_keys, peer_u, peer_v):
    b, t, d = x.shape
    depth = norm_mix.shape[0]
    h = x.reshape(b * t, d)
    hn = rmsnorm(h, norm_mix[0])
    u_all, v_all = peer_u.astype(BF16), peer_v.astype(BF16)
    for i in range(depth):
        m, j = i % N_MIXERS, i // N_MIXERS
        if m == 0:
            o = _fox_layer(hn, b, t, fox_w_in[j], fox_b_f[j], fox_q_gain[j], fox_k_gain[j])
            w_out = fox_w_out[j]
        elif m == 1:
            o = _gla_layer(hn, b, t, gla_w_in[j], gla_w_up[j], gla_b_alpha[j], gla_out_gain[j])
            w_out = gla_w_out[j]
        else:
            o = _hgrn_layer(hn, b, t, i, hgrn_w_in[j], hgrn_lb_logits, hgrn_out_gain[j])
            w_out = hgrn_w_out[j]
        h, hn = mm_res_norm(o, w_out.astype(BF16), h, norm_ffn[i])
        gain_next = norm_mix[(i + 1) % depth]
        h, hn = _peer_layer(h, hn, peer_w_q[i], peer_sub_keys[i], u_all, v_all, i, gain_next)
    return h.reshape(b, t, d)
```

```python
import functools
import math

import numpy as np
import jax
import jax.numpy as jnp
from jax import lax
from jax.experimental import pallas as pl
from jax.experimental.pallas import tpu as pltpu

F32 = jnp.float32
BF16 = jnp.bfloat16
EPS = 1e-6

DEPTH = 4
N_MIXERS = 3
CHUNK = 128
FOX_HEADS = 16
FOX_HEAD_DIM = 64
GLA_HEADS = 4
GLA_DK = 128
GLA_DV = 256
GLA_TAU = 16.0
HGRN_HEADS = 8
HGRN_DK = 128
HGRN_DV = 128
PEER_HEADS = 8
PEER_NKEYS = 128
PEER_TOPK = 16
LANES = 128
VMEM_LIMIT = 56 * 1024 * 1024


def _cparams(sem):
    return pltpu.CompilerParams(dimension_semantics=sem, vmem_limit_bytes=VMEM_LIMIT)


def _dot(a, b):
    return jnp.dot(a, b, preferred_element_type=F32)


def _dot_nt(a, b):
    return lax.dot_general(a, b, (((1,), (1,)), ((), ())), preferred_element_type=F32)


def _dot_tn(a, b):
    return lax.dot_general(a, b, (((0,), (0,)), ((), ())), preferred_element_type=F32)


def _split2(x):
    hi = x.astype(BF16)
    lo = (x - hi.astype(F32)).astype(BF16)
    return hi, lo


def _split3(x):
    hi = x.astype(BF16)
    r = x - hi.astype(F32)
    mid = r.astype(BF16)
    lo = (r - mid.astype(F32)).astype(BF16)
    return hi, mid, lo


def _log_sigmoid(x):
    return jnp.minimum(x, 0.0) - jnp.log1p(jnp.exp(-jnp.abs(x)))


def _sigmoid(x):
    return 1.0 / (1.0 + jnp.exp(-x))


def _rms(x, gain):
    ms = jnp.mean(x * x, axis=-1, keepdims=True)
    return x * lax.rsqrt(ms + EPS) * gain


def _rmsnorm_body(x_ref, g_ref, o_ref):
    o_ref[...] = _rms(x_ref[...], g_ref[...]).astype(o_ref.dtype)


def rmsnorm(x, gain, tm=512):
    n, d = x.shape
    tm = min(tm, n)
    return pl.pallas_call(
        _rmsnorm_body,
        out_shape=jax.ShapeDtypeStruct((n, d), BF16),
        grid=(n // tm,),
        in_specs=[pl.BlockSpec((tm, d), lambda i: (i, 0)),
                  pl.BlockSpec((1, d), lambda i: (0, 0))],
        out_specs=pl.BlockSpec((tm, d), lambda i: (i, 0)),
        compiler_params=_cparams(("parallel",)),
        name="rmsnorm",
    )(x, gain.reshape(1, d))


def _mm_call(body, x, w, extras, extra_specs, out_dtypes, tm, tn, name):
    n, k = x.shape
    m = w.shape[1]
    tm = min(tm, n)
    tn = min(tn, m)
    outs = [jax.ShapeDtypeStruct((n, m), dt) for dt in out_dtypes]
    return pl.pallas_call(
        body,
        out_shape=outs,
        grid=(n // tm, m // tn),
        in_specs=[pl.BlockSpec((tm, k), lambda i, j: (i, 0)),
                  pl.BlockSpec((k, tn), lambda i, j: (0, j))] + extra_specs,
        out_specs=[pl.BlockSpec((tm, tn), lambda i, j: (i, j)) for _ in outs],
        compiler_params=_cparams(("parallel", "arbitrary")),
        name=name,
    )(x, w, *extras)


def _mm_plain_body(x_ref, w_ref, o_ref, *, scale):
    acc = _dot(x_ref[...], w_ref[...])
    if scale != 1.0:
        acc = acc * scale
    o_ref[...] = acc.astype(o_ref.dtype)


def mm_plain(x, w, out_dtype, scale=1.0, tm=1024, tn=1024):
    return _mm_call(functools.partial(_mm_plain_body, scale=scale), x, w, [], [],
                    [out_dtype], tm, tn, "mm_plain")[0]


MXU_DIM = 256


def _fox_proj_body(x_ref, w_ref, bd_ref, g_ref, o_ref, *, hd):
    j = pl.program_id(1)
    acc = _dot(x_ref[...], w_ref[...])

    @pl.when(j < 2)
    def _():
        for c in range(acc.shape[1] // MXU_DIM):
            cols = slice(c * MXU_DIM, (c + 1) * MXU_DIM)
            a = acc[:, cols]
            ms = _dot((a * a).astype(BF16), bd_ref[...]) * (1.0 / hd)
            o_ref[:, cols] = (a * lax.rsqrt(ms + EPS) * g_ref[0, :, cols]).astype(o_ref.dtype)

    @pl.when(j >= 2)
    def _():
        o_ref[...] = acc.astype(o_ref.dtype)


def fox_proj(x, w, q_gain, k_gain, scale, hd, tm=1024):
    n, d = x.shape
    tm = min(tm, n)
    blk = np.arange(MXU_DIM) // hd
    bd = jnp.asarray(blk[:, None] == blk[None, :], BF16)
    g = jnp.stack([jnp.tile(q_gain.astype(F32) * scale, d // hd),
                   jnp.tile(k_gain.astype(F32), d // hd)]).reshape(2, 1, d)
    return pl.pallas_call(
        functools.partial(_fox_proj_body, hd=hd),
        out_shape=jax.ShapeDtypeStruct((n, 4 * d), BF16),
        grid=(n // tm, 4),
        in_specs=[pl.BlockSpec((tm, d), lambda i, j: (i, 0)),
                  pl.BlockSpec((d, d), lambda i, j: (0, j)),
                  pl.BlockSpec((MXU_DIM, MXU_DIM), lambda i, j: (0, 0)),
                  pl.BlockSpec((1, 1, d), lambda i, j: (jnp.minimum(j, 1), 0, 0))],
        out_specs=pl.BlockSpec((tm, d), lambda i, j: (i, j)),
        compiler_params=_cparams(("parallel", "arbitrary")),
        name="fox_proj",
    )(x, w, bd, g)


def _mm_logsig_body(x_ref, w_ref, b_ref, o_ref, *, mult):
    acc = _dot(x_ref[...], w_ref[...]) + b_ref[...]
    o_ref[...] = (_log_sigmoid(acc) * mult).astype(o_ref.dtype)


def mm_logsig(x, w, bias, mult, tm=512, tn=512):
    m = w.shape[1]
    tn = min(tn, m)
    return _mm_call(functools.partial(_mm_logsig_body, mult=mult), x, w,
                    [bias.astype(F32).reshape(1, m)],
                    [pl.BlockSpec((1, tn), lambda i, j: (0, j))],
                    [F32], tm, tn, "mm_logsig")[0]


def _mm_hgrn_gate_body(x_ref, w_ref, lbl_ref, k_ref, lg_ref, *, layer):
    acc = _dot(x_ref[...], w_ref[...])
    logits = lbl_ref[...]
    e = jnp.exp(logits - jnp.max(logits, axis=0, keepdims=True))
    p = e / jnp.sum(e, axis=0, keepdims=True)
    lb = jnp.zeros_like(p[0:1])
    for u in range(1, layer + 1):
        lb = lb + p[u:u + 1]
    gate = lb + (1.0 - lb) * _sigmoid(acc)
    k_ref[...] = (1.0 - gate).astype(k_ref.dtype)
    lg_ref[...] = jnp.log(gate).astype(lg_ref.dtype)


def mm_hgrn_gate(x, w, lb_logits, layer, tm=512, tn=512):
    m = w.shape[1]
    tn = min(tn, m)
    depth = lb_logits.shape[0]
    return _mm_call(functools.partial(_mm_hgrn_gate_body, layer=layer), x, w,
                    [lb_logits.astype(F32)],
                    [pl.BlockSpec((depth, tn), lambda i, j: (0, j))],
                    [F32, F32], tm, tn, "mm_hgrn_gate")


def _mm_res_norm_body(x_ref, w_ref, h_ref, g_ref, ho_ref, hn_ref):
    hnew = h_ref[...] + _dot(x_ref[...], w_ref[...])
    ho_ref[...] = hnew
    hn_ref[...] = _rms(hnew, g_ref[...]).astype(hn_ref.dtype)


def mm_res_norm(x, w, h, gain, tm=512):
    n, d = h.shape
    return _mm_call(_mm_res_norm_body, x, w, [h, gain.astype(F32).reshape(1, d)],
                    [pl.BlockSpec((min(tm, n), d), lambda i, j: (i, 0)),
                     pl.BlockSpec((1, d), lambda i, j: (0, 0))],
                    [F32, BF16], tm, d, "mm_res_norm")


CUM_BLOCK = 256


def _cumsum_body(x_ref, o_ref, *, t):
    nb = t // CUM_BLOCK
    r = lax.broadcasted_iota(jnp.int32, (CUM_BLOCK, CUM_BLOCK), 0)
    c = lax.broadcasted_iota(jnp.int32, (CUM_BLOCK, CUM_BLOCK), 1)
    tril = jnp.where(r >= c, 1.0, 0.0).astype(BF16)
    carry = jnp.zeros((1, x_ref.shape[-1]), F32)
    for b in range(nb):
        x = x_ref[0, b * CUM_BLOCK:(b + 1) * CUM_BLOCK, :]
        hi, mid, lo = _split3(x)
        cs = _dot(tril, hi) + _dot(tril, mid) + _dot(tril, lo) + carry
        o_ref[0, b * CUM_BLOCK:(b + 1) * CUM_BLOCK, :] = cs
        carry = cs[CUM_BLOCK - 1:CUM_BLOCK, :]


def time_cumsum(x):
    b, t, h = x.shape
    return pl.pallas_call(
        functools.partial(_cumsum_body, t=t),
        out_shape=jax.ShapeDtypeStruct((b, t, h), F32),
        grid=(b,),
        in_specs=[pl.BlockSpec((1, t, h), lambda i: (i, 0, 0))],
        out_specs=pl.BlockSpec((1, t, h), lambda i: (i, 0, 0)),
        compiler_params=_cparams(("parallel",)),
        name="time_cumsum",
    )(x)


def _fox_body(q_ref, k_ref, v_ref, g_ref, c_ref, o_ref, s_ref, m_ref, l_ref, acc_ref, *, tq):
    qi = pl.program_id(2)
    q = q_ref[0]
    lane = lax.broadcasted_iota(jnp.int32, (tq, LANES), 1)
    row = lax.broadcasted_iota(jnp.int32, (tq, tq), 0)
    col = lax.broadcasted_iota(jnp.int32, (tq, tq), 1)
    zero = jnp.zeros_like(q)
    qs = [jnp.where(lane < FOX_HEAD_DIM, q, zero), jnp.where(lane >= FOX_HEAD_DIM, q, zero)]
    nfold = tq // LANES

    def fold(x, op):
        r = x[:, 0:LANES]
        for f in range(1, nfold):
            r = op(r, x[:, f * LANES:(f + 1) * LANES])
        return r

    def scores(j, off):
        ks = k_ref[0, pl.ds(off, tq), :]
        return _dot_nt(qs[j], ks) - c_ref[0, 0, pl.ds(j, 1), pl.ds(off, tq)]

    m_ref[...] = jnp.full(m_ref.shape, -jnp.inf, F32)
    l_ref[...] = jnp.zeros(l_ref.shape, F32)
    acc_ref[...] = jnp.zeros(acc_ref.shape, F32)

    def pass1(kb, carry):
        off = pl.multiple_of(kb * tq, tq)
        for j in range(2):
            s = scores(j, off)
            s_ref[j, :, pl.ds(off, tq)] = s
            m_ref[j] = jnp.maximum(m_ref[j], fold(s, jnp.maximum))
        return carry

    lax.fori_loop(0, qi, pass1, 0)
    offd = pl.multiple_of(qi * tq, tq)
    sd, ms = [], []
    for j in range(2):
        s = jnp.where(row >= col, scores(j, offd), -jnp.inf)
        sd.append(s)
        ms.append(jnp.max(jnp.maximum(m_ref[j], fold(s, jnp.maximum)), axis=-1, keepdims=True))

    def pass2(kb, carry):
        off = pl.multiple_of(kb * tq, tq)
        vs = v_ref[0, pl.ds(off, tq), :]
        for j in range(2):
            p = jnp.exp(s_ref[j, :, pl.ds(off, tq)] - ms[j])
            l_ref[j] += fold(p, jnp.add)
            acc_ref[j] += _dot(p.astype(BF16), vs)
        return carry

    lax.fori_loop(0, qi, pass2, 0)
    vsd = v_ref[0, pl.ds(offd, tq), :]
    outs = []
    for j in range(2):
        p = jnp.exp(sd[j] - ms[j])
        l = jnp.sum(l_ref[j] + fold(p, jnp.add), axis=-1, keepdims=True)
        outs.append((acc_ref[j] + _dot(p.astype(BF16), vsd)) / l)
    o = jnp.where(lane < FOX_HEAD_DIM, outs[0], outs[1])
    o = o * _sigmoid(g_ref[0].astype(F32))
    o_ref[0] = o.astype(o_ref.dtype)


def fox_attention(qkvg, c_rows, tq=512):
    b, t, d4 = qkvg.shape
    d = d4 // 4
    tq = min(tq, t)
    npair = d // LANES
    qspec = pl.BlockSpec((1, tq, LANES), lambda bi, p, i: (bi, i, p))
    return pl.pallas_call(
        functools.partial(_fox_body, tq=tq),
        out_shape=jax.ShapeDtypeStruct((b, t, d), BF16),
        grid=(b, npair, t // tq),
        in_specs=[qspec,
                  pl.BlockSpec((1, t, LANES), lambda bi, p, i: (bi, 0, npair + p)),
                  pl.BlockSpec((1, t, LANES), lambda bi, p, i: (bi, 0, 2 * npair + p)),
                  pl.BlockSpec((1, tq, LANES), lambda bi, p, i: (bi, i, 3 * npair + p)),
                  pl.BlockSpec((1, 1, 2, t), lambda bi, p, i: (bi, p, 0, 0))],
        out_specs=qspec,
        scratch_shapes=[pltpu.VMEM((2, tq, t), F32), pltpu.VMEM((2, tq, LANES), F32),
                        pltpu.VMEM((2, tq, LANES), F32), pltpu.VMEM((2, tq, LANES), F32)],
        compiler_params=_cparams(("parallel", "parallel", "arbitrary")),
        name="fox_attention",
    )(qkvg, qkvg, qkvg, qkvg, c_rows)


_LEVELS = tuple(CHUNK >> s for s in range(CHUNK.bit_length() - 1))


def _gla_consts():
    c = CHUNK
    t = np.arange(c)
    u = t[None, :]
    blocks = [u <= t[:, None]]
    masks = [np.eye(c, dtype=bool)]
    for lv in _LEVELS:
        half = lv // 2
        blk, pos = t // lv, t % lv
        r = blk * lv + half - 1
        upper = pos >= half
        qrow = upper[:, None] & (u > r[:, None]) & (u <= t[:, None])
        krow = (~upper)[:, None] & (u > t[:, None]) & (u <= r[:, None])
        blocks.append(qrow | krow)
        masks.append((blk[:, None] == blk[None, :]) & upper[:, None] & (~upper)[None, :])
    blocks.append(u > t[:, None])
    mall = np.concatenate(blocks, axis=0).astype(np.float32)
    mask = np.stack(masks).astype(np.float32)
    mall = np.concatenate([mall, mall], axis=1)
    return jnp.asarray(mall, BF16), jnp.asarray(mask, F32)


def _gla_body(q_ref, k_ref, g_ref, v_ref, r_ref, gain_ref, mall_ref, mask_ref,
              o_ref, st_ref, *, nchunk, hg, dk, dv):
    c = CHUNK
    nl = len(_LEVELS)

    @pl.when(pl.program_id(2) == 0)
    def _():
        st_ref[...] = jnp.zeros_like(st_ref)

    def chunk(ci, carry):
        sl = pl.ds(pl.multiple_of(ci * c, c), c)
        g2 = jnp.concatenate(_split2(g_ref[0, sl, :]), axis=0)
        xall = jnp.exp(_dot(mall_ref[...], g2))
        for hh in range(hg):
            ks, vs = slice(hh * dk, (hh + 1) * dk), slice(hh * dv, (hh + 1) * dv)
            q = q_ref[0, sl, ks]
            k = k_ref[0, sl, ks]
            v = v_ref[0, sl, vs]
            x = xall[:, ks]
            scores = mask_ref[0] * _dot_nt(q.astype(BF16), k.astype(BF16))
            for lv in range(nl):
                z = x[(1 + lv) * c:(2 + lv) * c]
                scores = scores + mask_ref[1 + lv] * _dot_nt((q * z).astype(BF16), (k * z).astype(BF16))
            st = st_ref[hh]
            o = _dot(scores.astype(BF16), v) + _dot_nt((q * x[0:c]).astype(BF16), st.astype(BF16))
            kt = (k * x[(1 + nl) * c:(2 + nl) * c]).astype(BF16)
            st_ref[hh] = st * x[c - 1:c, :] + _dot_tn(v, kt)
            r = r_ref[0, sl, vs].astype(F32)
            y = _rms(o, gain_ref[hh]) * (r * _sigmoid(r))
            o_ref[0, sl, vs] = y.astype(o_ref.dtype)
        return carry

    lax.fori_loop(0, nchunk, chunk, 0, unroll=2)


def gated_linear_attention(q, k, g, v, r, gain, heads, dk, dv, hg=4, ts=512):
    b, t, _ = q.shape
    ts = min(ts, t)
    mall, mask = _gla_consts()
    qspec = pl.BlockSpec((1, ts, hg * dk), lambda bi, h, i: (bi, i, h))
    vspec = pl.BlockSpec((1, ts, hg * dv), lambda bi, h, i: (bi, i, h))
    return pl.pallas_call(
        functools.partial(_gla_body, nchunk=ts // CHUNK, hg=hg, dk=dk, dv=dv),
        out_shape=jax.ShapeDtypeStruct((b, t, heads * dv), BF16),
        grid=(b, heads // hg, t // ts),
        in_specs=[qspec, qspec, qspec, vspec, vspec,
                  pl.BlockSpec((hg, 1, dv), lambda bi, h, i: (h, 0, 0)),
                  pl.BlockSpec(mall.shape, lambda bi, h, i: (0, 0)),
                  pl.BlockSpec(mask.shape, lambda bi, h, i: (0, 0, 0))],
        out_specs=vspec,
        scratch_shapes=[pltpu.VMEM((hg, dv, dk), F32)],
        compiler_params=_cparams(("parallel", "parallel", "arbitrary")),
        name="gated_linear_attention",
    )(q, k, g, v, r, gain.astype(F32).reshape(heads, 1, dv), mall, mask)


def _peer_cand_layout():
    k = PEER_TOPK
    ab = [(0, b) for b in range(16)] + [(1, b) for b in range(8)]
    ab += [(2, b) if b < 5 else None for b in range(8)]
    ab += [(3, 0), (3, 1), (3, 2), (3, 3), (4, 0), (4, 1), (4, 2), None]
    ab += [(5, 0), (5, 1), (6, 0), (6, 1), (7, 0), (7, 1), None, None]
    ab += [(a, 0) for a in range(8, 16)]
    assert all(p is None or (p[0] + 1) * (p[1] + 1) <= k for p in ab)
    assert sum(p is not None for p in ab) == sum(k // (a + 1) for a in range(k))
    pos = np.array([PEER_POS_INVALID if p is None else p[0] * k + p[1] for p in ab], np.float32)
    tiles = pos.reshape(7, 8)
    first = tiles[0::2].reshape(-1)
    second = np.concatenate([tiles[1::2].reshape(-1), np.full(8, PEER_POS_INVALID, np.float32)])
    assert np.all((first < second) | (first == PEER_POS_INVALID))
    pos = np.concatenate([first, second])
    return np.broadcast_to(pos[:, None], (pos.size, LANES)).copy()


PEER_POS_INVALID = 1024.0


def _extract_max(s, ids, big):
    m = jnp.max(s, axis=0, keepdims=True)
    ix = jnp.min(jnp.where(s == m, ids, big), axis=0, keepdims=True)
    return m, ix


def _topk_pairs(a, b, ida, idb, big, k):
    a_wins = a >= b
    win, idw = jnp.maximum(a, b), jnp.where(a_wins, ida, idb)
    los, idl = jnp.minimum(a, b), jnp.where(a_wins, idb, ida)
    vals, sel = [], []
    for _ in range(k):
        m, ix = _extract_max(win, idw, big)
        vals.append(m)
        sel.append(ix)
        hit = idw == ix
        win = jnp.where(hit, los, win)
        idw = jnp.where(hit, idl, idw)
        los = jnp.where(hit, -jnp.inf, los)
    return jnp.concatenate(vals, axis=0), jnp.concatenate(sel, axis=0)


def _topk_keys(st, k):
    n = st.shape[0]
    a = jnp.concatenate([st[r:r + 8] for r in range(0, n, 16)], axis=0)
    b = jnp.concatenate([st[r + 8:r + 16] for r in range(0, n, 16)], axis=0)
    r = lax.broadcasted_iota(jnp.int32, a.shape, 0)
    ida = (((r >> 3) << 4) + (r & 7)).astype(F32)
    return _topk_pairs(a, b, ida, ida + 8.0, float(n), k)


def _route_head(q_ref, sk_ref, pos, h, tm):
    k = PEER_TOPK
    nk = PEER_NKEYS
    sub = lax.broadcasted_iota(jnp.int32, (8, tm), 0)

    def bc(x, r):
        return jnp.broadcast_to(x[r:r + 1], (8, tm))

    tops = []
    for p in range(2):
        hp = 2 * h + p
        qh = q_ref[:, pl.ds(pl.multiple_of(hp * nk, nk), nk)]
        st = _dot_nt(sk_ref[hp], qh)
        tops.append(_topk_keys(st, k))
    (s0, i0), (s1, i1) = tops
    lo1 = s1[0:8]
    tiles = [
        bc(s0, 0) + lo1,
        bc(s0, 0) + s1[8:16],
        bc(s0, 1) + lo1,
        bc(s0, 2) + lo1,
        jnp.where(sub < 4, bc(s0, 3), bc(s0, 4)) + jnp.where(sub < 4, lo1, pltpu.roll(lo1, 4, 0)),
        jnp.where(sub < 2, bc(s0, 5), jnp.where(sub < 4, bc(s0, 6), bc(s0, 7)))
        + jnp.where((sub & 1) == 0, bc(s1, 0), bc(s1, 1)),
        s0[8:16] + bc(s1, 0),
        jnp.full((8, tm), -jnp.inf, F32)]
    half = pos.shape[0] // 2
    pos_a, pos_b = pos[:half], pos[half:]
    ca = jnp.where(pos_a < PEER_POS_INVALID, jnp.concatenate(tiles[0::2], axis=0), -jnp.inf)
    cb = jnp.where(pos_b < PEER_POS_INVALID, jnp.concatenate(tiles[1::2], axis=0), -jnp.inf)
    best, bpos = _topk_pairs(ca, cb, pos_a, pos_b, 2.0 * PEER_POS_INVALID, k)
    bpos = bpos.astype(jnp.int32)
    ra = bpos >> 4
    rb = bpos & (k - 1)
    isel = jnp.zeros((k, tm), F32)
    jsel = jnp.zeros((k, tm), F32)
    for a in range(k):
        isel = jnp.where(ra == a, i0[a:a + 1], isel)
        jsel = jnp.where(rb == a, i1[a:a + 1], jsel)
    e = jnp.exp(best - best[0:1])
    gates = e / jnp.sum(e, axis=0, keepdims=True)
    return isel, jsel, gates


W3_GROUP = 8
W3_BATCH = 2 * W3_GROUP
ROUTE_HEADS_PER_TRIP = 2


def _peer_route_weights_body(x_ref, xn_ref, wq_ref, sk_ref, pos_ref, o_ref, q_ref, ri_ref, rj_ref,
                             rg_ref, pt_ref, w_ref, *, tm):
    k = PEER_TOPK
    nk = PEER_NKEYS
    ngroup = tm // LANES
    nbatch = tm // W3_BATCH
    per_head = nbatch // PEER_HEADS

    step = pl.program_id(0)
    cur = lax.rem(step, 2)
    q_cur = q_ref.at[cur]

    @pl.when(step == 0)
    def _():
        q_ref[0] = _dot(x_ref[...], wq_ref[...]).astype(q_ref.dtype)
        ri_ref[...] = jnp.zeros_like(ri_ref)
        rj_ref[...] = jnp.zeros_like(rj_ref)
        rg_ref[...] = jnp.zeros_like(rg_ref)

    for a, r_ref in enumerate((ri_ref, rj_ref, rg_ref)):
        for c in range(ngroup):
            pt_ref[a, c * LANES:(c + 1) * LANES, :] = r_ref[:, c * LANES:(c + 1) * LANES].T
    pos = jnp.concatenate([pos_ref[...]] * ngroup, axis=1)
    sub = lax.broadcasted_iota(jnp.int32, (nk, LANES), 0).astype(F32)

    def build(t):
        slot = lax.rem(t, 2 * ROUTE_HEADS_PER_TRIP * per_head)
        for half in range(2):
            tok0 = pl.multiple_of(t * W3_BATCH + half * W3_GROUP, W3_GROUP)
            it, jt, gt = (pt_ref[a, pl.ds(tok0, W3_GROUP), :] for a in range(3))
            for s in range(W3_GROUP):
                at = jnp.where(sub == it[s:s + 1], gt[s:s + 1], 0.0).astype(BF16)
                bt = jnp.where(sub == jt[s:s + 1], 1.0, 0.0).astype(BF16)
                w_ref[2 * slot + half, pl.ds(s, nk, stride=W3_GROUP), :] = _dot_nt(at, bt)

    def flush(t):
        slot = lax.rem(t, 2 * ROUTE_HEADS_PER_TRIP * per_head)
        r0 = pl.multiple_of(t * W3_BATCH, W3_BATCH)
        for i in range(nk):
            rows = jnp.concatenate([w_ref[2 * slot, i * W3_GROUP:(i + 1) * W3_GROUP, :],
                                    w_ref[2 * slot + 1, i * W3_GROUP:(i + 1) * W3_GROUP, :]], axis=0)
            o_ref[pl.ds(r0, W3_BATCH), i * nk:(i + 1) * nk] = rows.astype(o_ref.dtype)

    hpt = ROUTE_HEADS_PER_TRIP
    ntrip = PEER_HEADS // hpt
    qcols = q_ref.shape[2] // ntrip

    def do_heads(u, flush_prev):
        if flush_prev:
            for t in range(hpt * per_head):
                flush((u - 1) * hpt * per_head + t)
        for hh in range(hpt):
            h = u * hpt + hh
            isel, jsel, gates = _route_head(q_cur, sk_ref, pos, h, tm)
            rows = pl.ds(pl.multiple_of(h * k, k), k)
            ri_ref[rows, :] = isel
            rj_ref[rows, :] = jsel
            rg_ref[rows, :] = gates
            for t in range(per_head):
                build(h * per_head + t)
        cols = pl.ds(pl.multiple_of(u * qcols, qcols), qcols)
        q_ref[1 - cur, :, cols] = _dot(xn_ref[...], wq_ref[:, cols]).astype(q_ref.dtype)

    def trip(u, carry):
        do_heads(u, True)
        return carry

    do_heads(jnp.int32(0), False)
    lax.fori_loop(1, ntrip, trip, 0)
    for t in range(hpt * per_head):
        flush(jnp.int32((ntrip - 1) * hpt * per_head + t))


def peer_route_weights(x, w_q, sub_keys, tm=256):
    n, d = x.shape
    dq = w_q.shape[1]
    tm = min(tm, n)
    nblk = n // tm
    ne = PEER_NKEYS * PEER_NKEYS
    npair = PEER_HEADS * PEER_TOPK
    pos = jnp.asarray(_peer_cand_layout())
    return pl.pallas_call(
        functools.partial(_peer_route_weights_body, tm=tm),
        out_shape=jax.ShapeDtypeStruct((n, ne), BF16),
        grid=(nblk + 1,),
        in_specs=[pl.BlockSpec((tm, d), lambda s: (jnp.minimum(s, nblk - 1), 0)),
                  pl.BlockSpec((tm, d), lambda s: (jnp.minimum(s + 1, nblk - 1), 0)),
                  pl.BlockSpec((d, dq), lambda s: (0, 0), pipeline_mode=pl.Buffered(1)),
                  pl.BlockSpec(sub_keys.shape, lambda s: (0, 0, 0)),
                  pl.BlockSpec(pos.shape, lambda s: (0, 0))],
        out_specs=pl.BlockSpec((tm, ne), lambda s: (jnp.maximum(s - 1, 0), 0)),
        scratch_shapes=[pltpu.VMEM((2, tm, dq), BF16)]
        + [pltpu.VMEM((npair, tm), F32)] * 3 + [pltpu.VMEM((3, tm, npair), F32)]
        + [pltpu.VMEM((4 * ROUTE_HEADS_PER_TRIP * tm // (W3_BATCH * PEER_HEADS),
                       W3_GROUP * PEER_NKEYS, PEER_NKEYS), F32)],
        compiler_params=_cparams(("arbitrary",)),
        name="peer_route_weights",
    )(x, x, w_q, sub_keys, pos)


DENSE_SUBTILE = 1024


def _gelu(x):
    return 0.5 * x * (1.0 + lax.erf(x * (1.0 / math.sqrt(2.0))))


def _peer_dense_body(x_ref, u_ref, v_ref, w_ref, h_ref, gn_ref, ho_ref, hn_ref, acc_ref, *, nj):
    j = pl.program_id(1)

    @pl.when(j == 0)
    def _():
        acc_ref[...] = jnp.zeros_like(acc_ref)

    x = x_ref[...]
    te = u_ref.shape[0]
    for s in range(te // min(te, DENSE_SUBTILE)):
        rows = slice(s * DENSE_SUBTILE, (s + 1) * DENSE_SUBTILE)
        hid = _dot_nt(x, u_ref[rows, :])
        a = (_gelu(hid) * w_ref[:, rows].astype(F32)).astype(BF16)
        acc_ref[...] += _dot(a, v_ref[rows, :])

    @pl.when(j == nj - 1)
    def _():
        hnew = h_ref[...] + acc_ref[...]
        ho_ref[...] = hnew
        hn_ref[...] = _rms(hnew, gn_ref[...]).astype(hn_ref.dtype)


def peer_dense(x, u, v, layer, w, h, gain_next, tm=1024, te=2048):
    n, d = x.shape
    ne = u.shape[1]
    tm = min(tm, n)
    te = min(te, ne)
    nj = ne // te
    rowspec_in = pl.BlockSpec((tm, d), lambda i, j: (i, 0), pipeline_mode=pl.Buffered(1))
    rowspec_out = pl.BlockSpec((tm, d), lambda i, j: (i, 0), pipeline_mode=pl.Buffered(1))
    return pl.pallas_call(
        functools.partial(_peer_dense_body, nj=nj),
        out_shape=[jax.ShapeDtypeStruct((n, d), F32), jax.ShapeDtypeStruct((n, d), BF16)],
        grid=(n // tm, nj),
        in_specs=[rowspec_in,
                  pl.BlockSpec((None, te, d), lambda i, j: (layer, j, 0)),
                  pl.BlockSpec((None, te, d), lambda i, j: (layer, j, 0)),
                  pl.BlockSpec((tm, te), lambda i, j: (i, j)),
                  rowspec_in,
                  pl.BlockSpec((1, d), lambda i, j: (0, 0))],
        out_specs=[rowspec_out, rowspec_out],
        scratch_shapes=[pltpu.VMEM((tm, d), F32)],
        compiler_params=_cparams(("parallel", "arbitrary")),
        name="peer_dense",
    )(x, u, v, w, h, gain_next.astype(F32).reshape(1, d))


def _fox_layer(hn, b, t, w_in, b_f, q_gain, k_gain):
    d = hn.shape[1]
    w = w_in.astype(BF16)
    qkvg = fox_proj(hn, w[:, 0:4 * d], q_gain, k_gain, FOX_HEAD_DIM ** -0.5, FOX_HEAD_DIM)
    lf = mm_logsig(hn, w[:, 4 * d:], b_f, 1.0)
    c = time_cumsum(lf.reshape(b, t, FOX_HEADS))
    c_rows = c.transpose(0, 2, 1).reshape(b, FOX_HEADS // 2, 2, t)
    o = fox_attention(qkvg.reshape(b, t, 4 * d), c_rows)
    return o.reshape(b * t, d)


def _gla_layer(hn, b, t, w_in, w_up, b_alpha, out_gain):
    d = hn.shape[1]
    gk = GLA_HEADS * GLA_DK
    w = w_in.astype(BF16)
    q = mm_plain(hn, w[:, 0:gk], F32, scale=GLA_DK ** -0.5)
    k = mm_plain(hn, w[:, gk:2 * gk], F32)
    v = mm_plain(hn, w[:, 2 * gk:2 * gk + d], BF16)
    r = mm_plain(hn, w[:, 2 * gk + d:2 * gk + 2 * d], BF16)
    low = mm_plain(hn, w[:, 2 * gk + 2 * d:], BF16)
    log_a = mm_logsig(low, w_up.astype(BF16), b_alpha, 1.0 / GLA_TAU)
    o = gated_linear_attention(q.reshape(b, t, gk), k.reshape(b, t, gk), log_a.reshape(b, t, gk),
                               v.reshape(b, t, d), r.reshape(b, t, d), out_gain,
                               GLA_HEADS, GLA_DK, GLA_DV)
    return o.reshape(b * t, d)


def _hgrn_layer(hn, b, t, layer, w_in, lb_logits, out_gain):
    d = hn.shape[1]
    w = w_in.astype(BF16)
    q = mm_plain(hn, w[:, 0:d], F32, scale=HGRN_DK ** -0.5)
    k, log_g = mm_hgrn_gate(hn, w[:, d:2 * d], lb_logits, layer)
    v = mm_plain(hn, w[:, 2 * d:3 * d], BF16)
    r = mm_plain(hn, w[:, 3 * d:4 * d], BF16)
    shp = (b, t, d)
    o = gated_linear_attention(q.reshape(shp), k.reshape(shp), log_g.reshape(shp),
                               v.reshape(shp), r.reshape(shp), out_gain,
                               HGRN_HEADS, HGRN_DK, HGRN_DV)
    return o.reshape(b * t, d)


def _peer_layer(h, hn, w_q, sub_keys, u, v, layer, gain_next):
    sk = sub_keys.astype(BF16).reshape(2 * PEER_HEADS, PEER_NKEYS, -1)
    w = peer_route_weights(hn, w_q.astype(BF16), sk)
    return peer_dense(hn, u, v, layer, w, h, gain_next)


def kernel(x, norm_mix, norm_ffn, fox_w_in, fox_b_f, fox_q_gain, fox_k_gain, fox_w_out,
           gla_w_in, gla_w_up, gla_b_alpha, gla_out_gain, gla_w_out,
           hgrn_w_in, hgrn_lb_logits, hgrn_out_gain, hgrn_w_out,
           peer_w_q, peer_sub_keys, peer_u, peer_v):
    b, t, d = x.shape
    depth = norm_mix.shape[0]
    h = x.reshape(b * t, d)
    hn = rmsnorm(h, norm_mix[0])
    u_all, v_all = peer_u.astype(BF16), peer_v.astype(BF16)
    for i in range(depth):
        m, j = i % N_MIXERS, i // N_MIXERS
        if m == 0:
            o = _fox_layer(hn, b, t, fox_w_in[j], fox_b_f[j], fox_q_gain[j], fox_k_gain[j])
            w_out = fox_w_out[j]
        elif m == 1:
            o = _gla_layer(hn, b, t, gla_w_in[j], gla_w_up[j], gla_b_alpha[j], gla_out_gain[j])
            w_out = gla_w_out[j]
        else:
            o = _hgrn_layer(hn, b, t, i, hgrn_w_in[j], hgrn_lb_logits, hgrn_out_gain[j])
            w_out = hgrn_w_out[j]
        h, hn = mm_res_norm(o, w_out.astype(BF16), h, norm_ffn[i])
        gain_next = norm_mix[(i + 1) % depth]
        h, hn = _peer_layer(h, hn, peer_w_q[i], peer_sub_keys[i], u_all, v_all, i, gain_next)
    return h.reshape(b, t, d)
```

```python
import functools
import math

import numpy as np
import jax
import jax.numpy as jnp
from jax import lax
from jax.experimental import pallas as pl
from jax.experimental.pallas import tpu as pltpu

F32 = jnp.float32
BF16 = jnp.bfloat16
EPS = 1e-6

DEPTH = 4
N_MIXERS = 3
CHUNK = 128
FOX_HEADS = 16
FOX_HEAD_DIM = 64
GLA_HEADS = 4
GLA_DK = 128
GLA_DV = 256
GLA_TAU = 16.0
HGRN_HEADS = 8
HGRN_DK = 128
HGRN_DV = 128
PEER_HEADS = 8
PEER_NKEYS = 128
PEER_TOPK = 16
LANES = 128
VMEM_LIMIT = 56 * 1024 * 1024


def _cparams(sem):
    return pltpu.CompilerParams(dimension_semantics=sem, vmem_limit_bytes=VMEM_LIMIT)


def _dot(a, b):
    return jnp.dot(a, b, preferred_element_type=F32)


def _dot_nt(a, b):
    return lax.dot_general(a, b, (((1,), (1,)), ((), ())), preferred_element_type=F32)


def _dot_tn(a, b):
    return lax.dot_general(a, b, (((0,), (0,)), ((), ())), preferred_element_type=F32)


def _split2(x):
    hi = x.astype(BF16)
    lo = (x - hi.astype(F32)).astype(BF16)
    return hi, lo


def _split3(x):
    hi = x.astype(BF16)
    r = x - hi.astype(F32)
    mid = r.astype(BF16)
    lo = (r - mid.astype(F32)).astype(BF16)
    return hi, mid, lo


def _log_sigmoid(x):
    return jnp.minimum(x, 0.0) - jnp.log1p(jnp.exp(-jnp.abs(x)))


def _sigmoid(x):
    return 1.0 / (1.0 + jnp.exp(-x))


def _rms(x, gain):
    ms = jnp.mean(x * x, axis=-1, keepdims=True)
    return x * lax.rsqrt(ms + EPS) * gain


def _rmsnorm_body(x_ref, g_ref, o_ref):
    o_ref[...] = _rms(x_ref[...], g_ref[...]).astype(o_ref.dtype)


def rmsnorm(x, gain, tm=512):
    n, d = x.shape
    tm = min(tm, n)
    return pl.pallas_call(
        _rmsnorm_body,
        out_shape=jax.ShapeDtypeStruct((n, d), BF16),
        grid=(n // tm,),
        in_specs=[pl.BlockSpec((tm, d), lambda i: (i, 0)),
                  pl.BlockSpec((1, d), lambda i: (0, 0))],
        out_specs=pl.BlockSpec((tm, d), lambda i: (i, 0)),
        compiler_params=_cparams(("parallel",)),
        name="rmsnorm",
    )(x, gain.reshape(1, d))


def _mm_call(body, x, w, extras, extra_specs, out_dtypes, tm, tn, name):
    n, k = x.shape
    m = w.shape[1]
    tm = min(tm, n)
    tn = min(tn, m)
    outs = [jax.ShapeDtypeStruct((n, m), dt) for dt in out_dtypes]
    return pl.pallas_call(
        body,
        out_shape=outs,
        grid=(n // tm, m // tn),
        in_specs=[pl.BlockSpec((tm, k), lambda i, j: (i, 0)),
                  pl.BlockSpec((k, tn), lambda i, j: (0, j))] + extra_specs,
        out_specs=[pl.BlockSpec((tm, tn), lambda i, j: (i, j)) for _ in outs],
        compiler_params=_cparams(("parallel", "arbitrary")),
        name=name,
    )(x, w, *extras)


def _mm_plain_body(x_ref, w_ref, o_ref, *, scale):
    acc = _dot(x_ref[...], w_ref[...])
    if scale != 1.0:
        acc = acc * scale
    o_ref[...] = acc.astype(o_ref.dtype)


def mm_plain(x, w, out_dtype, scale=1.0, tm=1024, tn=1024):
    return _mm_call(functools.partial(_mm_plain_body, scale=scale), x, w, [], [],
                    [out_dtype], tm, tn, "mm_plain")[0]


MXU_DIM = 256


def _fox_proj_body(x_ref, w_ref, bd_ref, g_ref, o_ref, *, hd):
    j = pl.program_id(1)
    acc = _dot(x_ref[...], w_ref[...])

    @pl.when(j < 2)
    def _():
        for c in range(acc.shape[1] // MXU_DIM):
            cols = slice(c * MXU_DIM, (c + 1) * MXU_DIM)
            a = acc[:, cols]
            ms = _dot((a * a).astype(BF16), bd_ref[...]) * (1.0 / hd)
            o_ref[:, cols] = (a * lax.rsqrt(ms + EPS) * g_ref[0, :, cols]).astype(o_ref.dtype)

    @pl.when(j >= 2)
    def _():
        o_ref[...] = acc.astype(o_ref.dtype)


def fox_proj(x, w, q_gain, k_gain, scale, hd, tm=1024):
    n, d = x.shape
    tm = min(tm, n)
    blk = np.arange(MXU_DIM) // hd
    bd = jnp.asarray(blk[:, None] == blk[None, :], BF16)
    g = jnp.stack([jnp.tile(q_gain.astype(F32) * scale, d // hd),
                   jnp.tile(k_gain.astype(F32), d // hd)]).reshape(2, 1, d)
    return pl.pallas_call(
        functools.partial(_fox_proj_body, hd=hd),
        out_shape=jax.ShapeDtypeStruct((n, 4 * d), BF16),
        grid=(n // tm, 4),
        in_specs=[pl.BlockSpec((tm, d), lambda i, j: (i, 0)),
                  pl.BlockSpec((d, d), lambda i, j: (0, j)),
                  pl.BlockSpec((MXU_DIM, MXU_DIM), lambda i, j: (0, 0)),
                  pl.BlockSpec((1, 1, d), lambda i, j: (jnp.minimum(j, 1), 0, 0))],
        out_specs=pl.BlockSpec((tm, d), lambda i, j: (i, j)),
        compiler_params=_cparams(("parallel", "arbitrary")),
        name="fox_proj",
    )(x, w, bd, g)


def _mm_logsig_body(x_ref, w_ref, b_ref, o_ref, *, mult):
    acc = _dot(x_ref[...], w_ref[...]) + b_ref[...]
    o_ref[...] = (_log_sigmoid(acc) * mult).astype(o_ref.dtype)


def mm_logsig(x, w, bias, mult, tm=512, tn=512):
    m = w.shape[1]
    tn = min(tn, m)
    return _mm_call(functools.partial(_mm_logsig_body, mult=mult), x, w,
                    [bias.astype(F32).reshape(1, m)],
                    [pl.BlockSpec((1, tn), lambda i, j: (0, j))],
                    [F32], tm, tn, "mm_logsig")[0]


def _mm_hgrn_gate_body(x_ref, w_ref, lbl_ref, k_ref, lg_ref, *, layer):
    acc = _dot(x_ref[...], w_ref[...])
    logits = lbl_ref[...]
    e = jnp.exp(logits - jnp.max(logits, axis=0, keepdims=True))
    p = e / jnp.sum(e, axis=0, keepdims=True)
    lb = jnp.zeros_like(p[0:1])
    for u in range(1, layer + 1):
        lb = lb + p[u:u + 1]
    gate = lb + (1.0 - lb) * _sigmoid(acc)
    k_ref[...] = (1.0 - gate).astype(k_ref.dtype)
    lg_ref[...] = jnp.log(gate).astype(lg_ref.dtype)


def mm_hgrn_gate(x, w, lb_logits, layer, tm=512, tn=512):
    m = w.shape[1]
    tn = min(tn, m)
    depth = lb_logits.shape[0]
    return _mm_call(functools.partial(_mm_hgrn_gate_body, layer=layer), x, w,
                    [lb_logits.astype(F32)],
                    [pl.BlockSpec((depth, tn), lambda i, j: (0, j))],
                    [F32, F32], tm, tn, "mm_hgrn_gate")


def _mm_res_norm_body(x_ref, w_ref, h_ref, g_ref, ho_ref, hn_ref):
    hnew = h_ref[...] + _dot(x_ref[...], w_ref[...])
    ho_ref[...] = hnew
    hn_ref[...] = _rms(hnew, g_ref[...]).astype(hn_ref.dtype)


def mm_res_norm(x, w, h, gain, tm=512):
    n, d = h.shape
    return _mm_call(_mm_res_norm_body, x, w, [h, gain.astype(F32).reshape(1, d)],
                    [pl.BlockSpec((min(tm, n), d), lambda i, j: (i, 0)),
                     pl.BlockSpec((1, d), lambda i, j: (0, 0))],
                    [F32, BF16], tm, d, "mm_res_norm")


CUM_BLOCK = 256


def _cumsum_body(x_ref, o_ref, *, t):
    nb = t // CUM_BLOCK
    r = lax.broadcasted_iota(jnp.int32, (CUM_BLOCK, CUM_BLOCK), 0)
    c = lax.broadcasted_iota(jnp.int32, (CUM_BLOCK, CUM_BLOCK), 1)
    tril = jnp.where(r >= c, 1.0, 0.0).astype(BF16)
    carry = jnp.zeros((1, x_ref.shape[-1]), F32)
    for b in range(nb):
        x = x_ref[0, b * CUM_BLOCK:(b + 1) * CUM_BLOCK, :]
        hi, mid, lo = _split3(x)
        cs = _dot(tril, hi) + _dot(tril, mid) + _dot(tril, lo) + carry
        o_ref[0, b * CUM_BLOCK:(b + 1) * CUM_BLOCK, :] = cs
        carry = cs[CUM_BLOCK - 1:CUM_BLOCK, :]


def time_cumsum(x):
    b, t, h = x.shape
    return pl.pallas_call(
        functools.partial(_cumsum_body, t=t),
        out_shape=jax.ShapeDtypeStruct((b, t, h), F32),
        grid=(b,),
        in_specs=[pl.BlockSpec((1, t, h), lambda i: (i, 0, 0))],
        out_specs=pl.BlockSpec((1, t, h), lambda i: (i, 0, 0)),
        compiler_params=_cparams(("parallel",)),
        name="time_cumsum",
    )(x)


FOX_PAIRS_PER_STEP = 2


def _fox_body(q_ref, k_ref, v_ref, g_ref, c_ref, o_ref, s_ref, m_ref, l_ref, acc_ref, *, tq):
    qi = pl.program_id(2)
    q = q_ref[0]
    width = q.shape[1]
    nh = width // FOX_HEAD_DIM
    lane = lax.broadcasted_iota(jnp.int32, (tq, width), 1)
    row = lax.broadcasted_iota(jnp.int32, (tq, tq), 0)
    col = lax.broadcasted_iota(jnp.int32, (tq, tq), 1)
    zero = jnp.zeros_like(q)
    head_of_lane = lane // FOX_HEAD_DIM
    qs = [jnp.where(head_of_lane == j, q, zero) for j in range(nh)]
    nfold = tq // LANES

    def fold(x, op):
        r = x[:, 0:LANES]
        for f in range(1, nfold):
            r = op(r, x[:, f * LANES:(f + 1) * LANES])
        return r

    def scores(j, off):
        ks = k_ref[0, pl.ds(off, tq), :]
        return _dot_nt(qs[j], ks) - c_ref[0, j // 2, pl.ds(j % 2, 1), pl.ds(off, tq)]

    def values(j, off):
        return v_ref[0, pl.ds(off, tq), (j // 2) * LANES:(j // 2 + 1) * LANES]

    m_ref[...] = jnp.full(m_ref.shape, -jnp.inf, F32)
    l_ref[...] = jnp.zeros(l_ref.shape, F32)
    acc_ref[...] = jnp.zeros(acc_ref.shape, F32)

    def pass1(kb, carry):
        off = pl.multiple_of(kb * tq, tq)
        for j in range(nh):
            s = scores(j, off)
            s_ref[j, :, pl.ds(off, tq)] = s
            m_ref[j] = jnp.maximum(m_ref[j], fold(s, jnp.maximum))
        return carry

    lax.fori_loop(0, qi, pass1, 0)
    offd = pl.multiple_of(qi * tq, tq)
    sd, ms = [], []
    for j in range(nh):
        s = jnp.where(row >= col, scores(j, offd), -jnp.inf)
        sd.append(s)
        ms.append(jnp.max(jnp.maximum(m_ref[j], fold(s, jnp.maximum)), axis=-1, keepdims=True))

    def pass2(kb, carry):
        off = pl.multiple_of(kb * tq, tq)
        for j in range(nh):
            p = jnp.exp(s_ref[j, :, pl.ds(off, tq)] - ms[j])
            l_ref[j] += fold(p, jnp.add)
            acc_ref[j] += _dot(p.astype(BF16), values(j, off))
        return carry

    lax.fori_loop(0, qi, pass2, 0)
    outs = []
    for j in range(nh):
        p = jnp.exp(sd[j] - ms[j])
        l = jnp.sum(l_ref[j] + fold(p, jnp.add), axis=-1, keepdims=True)
        outs.append((acc_ref[j] + _dot(p.astype(BF16), values(j, offd))) / l)
    lane1 = lax.broadcasted_iota(jnp.int32, (tq, LANES), 1)
    o = jnp.concatenate([jnp.where(lane1 < FOX_HEAD_DIM, outs[2 * pp], outs[2 * pp + 1])
                         for pp in range(nh // 2)], axis=1)
    o = o * _sigmoid(g_ref[0].astype(F32))
    o_ref[0] = o.astype(o_ref.dtype)


def fox_attention(qkvg, c_rows, tq=512):
    b, t, d4 = qkvg.shape
    d = d4 // 4
    tq = min(tq, t)
    pps = FOX_PAIRS_PER_STEP
    width = pps * LANES
    ngrp = d // width
    nh = 2 * pps
    qspec = pl.BlockSpec((1, tq, width), lambda bi, p, i: (bi, i, p))
    return pl.pallas_call(
        functools.partial(_fox_body, tq=tq),
        out_shape=jax.ShapeDtypeStruct((b, t, d), BF16),
        grid=(b, ngrp, t // tq),
        in_specs=[qspec,
                  pl.BlockSpec((1, t, width), lambda bi, p, i: (bi, 0, ngrp + p)),
                  pl.BlockSpec((1, t, width), lambda bi, p, i: (bi, 0, 2 * ngrp + p)),
                  pl.BlockSpec((1, tq, width), lambda bi, p, i: (bi, i, 3 * ngrp + p)),
                  pl.BlockSpec((1, pps, 2, t), lambda bi, p, i: (bi, p, 0, 0))],
        out_specs=qspec,
        scratch_shapes=[pltpu.VMEM((nh, tq, t), F32), pltpu.VMEM((nh, tq, LANES), F32),
                        pltpu.VMEM((nh, tq, LANES), F32), pltpu.VMEM((nh, tq, LANES), F32)],
        compiler_params=_cparams(("parallel", "parallel", "arbitrary")),
        name="fox_attention",
    )(qkvg, qkvg, qkvg, qkvg, c_rows)


_LEVELS = tuple(CHUNK >> s for s in range(CHUNK.bit_length() - 1))


def _gla_consts():
    c = CHUNK
    t = np.arange(c)
    u = t[None, :]
    blocks = [u <= t[:, None]]
    masks = [np.eye(c, dtype=bool)]
    for lv in _LEVELS:
        half = lv // 2
        blk, pos = t // lv, t % lv
        r = blk * lv + half - 1
        upper = pos >= half
        qrow = upper[:, None] & (u > r[:, None]) & (u <= t[:, None])
        krow = (~upper)[:, None] & (u > t[:, None]) & (u <= r[:, None])
        blocks.append(qrow | krow)
        masks.append((blk[:, None] == blk[None, :]) & upper[:, None] & (~upper)[None, :])
    blocks.append(u > t[:, None])
    mall = np.concatenate(blocks, axis=0).astype(np.float32)
    mask = np.stack(masks).astype(np.float32)
    mall = np.concatenate([mall, mall], axis=1)
    return jnp.asarray(mall, BF16), jnp.asarray(mask, F32)


def _gla_body(q_ref, k_ref, g_ref, v_ref, r_ref, gain_ref, mall_ref, mask_ref,
              o_ref, st_ref, *, nchunk, hg, dk, dv):
    c = CHUNK
    nl = len(_LEVELS)

    @pl.when(pl.program_id(2) == 0)
    def _():
        st_ref[...] = jnp.zeros_like(st_ref)

    def chunk(ci, carry):
        sl = pl.ds(pl.multiple_of(ci * c, c), c)
        g2 = jnp.concatenate(_split2(g_ref[0, sl, :]), axis=0)
        xall = jnp.exp(_dot(mall_ref[...], g2))
        for hh in range(hg):
            ks, vs = slice(hh * dk, (hh + 1) * dk), slice(hh * dv, (hh + 1) * dv)
            q = q_ref[0, sl, ks]
            k = k_ref[0, sl, ks]
            v = v_ref[0, sl, vs]
            x = xall[:, ks]
            scores = mask_ref[0] * _dot_nt(q.astype(BF16), k.astype(BF16))
            for lv in range(nl):
                z = x[(1 + lv) * c:(2 + lv) * c]
                scores = scores + mask_ref[1 + lv] * _dot_nt((q * z).astype(BF16), (k * z).astype(BF16))
            st = st_ref[hh]
            o = _dot(scores.astype(BF16), v) + _dot_nt((q * x[0:c]).astype(BF16), st.astype(BF16))
            kt = (k * x[(1 + nl) * c:(2 + nl) * c]).astype(BF16)
            st_ref[hh] = st * x[c - 1:c, :] + _dot_tn(v, kt)
            r = r_ref[0, sl, vs].astype(F32)
            y = _rms(o, gain_ref[hh]) * (r * _sigmoid(r))
            o_ref[0, sl, vs] = y.astype(o_ref.dtype)
        return carry

    lax.fori_loop(0, nchunk, chunk, 0, unroll=2)


def gated_linear_attention(q, k, g, v, r, gain, heads, dk, dv, hg=4, ts=512):
    b, t, _ = q.shape
    ts = min(ts, t)
    mall, mask = _gla_consts()
    qspec = pl.BlockSpec((1, ts, hg * dk), lambda bi, h, i: (bi, i, h))
    vspec = pl.BlockSpec((1, ts, hg * dv), lambda bi, h, i: (bi, i, h))
    return pl.pallas_call(
        functools.partial(_gla_body, nchunk=ts // CHUNK, hg=hg, dk=dk, dv=dv),
        out_shape=jax.ShapeDtypeStruct((b, t, heads * dv), BF16),
        grid=(b, heads // hg, t // ts),
        in_specs=[qspec, qspec, qspec, vspec, vspec,
                  pl.BlockSpec((hg, 1, dv), lambda bi, h, i: (h, 0, 0)),
                  pl.BlockSpec(mall.shape, lambda bi, h, i: (0, 0)),
                  pl.BlockSpec(mask.shape, lambda bi, h, i: (0, 0, 0))],
        out_specs=vspec,
        scratch_shapes=[pltpu.VMEM((hg, dv, dk), F32)],
        compiler_params=_cparams(("parallel", "parallel", "arbitrary")),
        name="gated_linear_attention",
    )(q, k, g, v, r, gain.astype(F32).reshape(heads, 1, dv), mall, mask)


def _peer_cand_layout():
    k = PEER_TOPK
    ab = [(0, b) for b in range(16)] + [(1, b) for b in range(8)]
    ab += [(2, b) if b < 5 else None for b in range(8)]
    ab += [(3, 0), (3, 1), (3, 2), (3, 3), (4, 0), (4, 1), (4, 2), None]
    ab += [(5, 0), (5, 1), (6, 0), (6, 1), (7, 0), (7, 1), None, None]
    ab += [(a, 0) for a in range(8, 16)]
    assert all(p is None or (p[0] + 1) * (p[1] + 1) <= k for p in ab)
    assert sum(p is not None for p in ab) == sum(k // (a + 1) for a in range(k))
    pos = np.array([PEER_POS_INVALID if p is None else p[0] * k + p[1] for p in ab], np.float32)
    tiles = pos.reshape(7, 8)
    first = tiles[0::2].reshape(-1)
    second = np.concatenate([tiles[1::2].reshape(-1), np.full(8, PEER_POS_INVALID, np.float32)])
    assert np.all((first < second) | (first == PEER_POS_INVALID))
    pos = np.concatenate([first, second])
    return np.broadcast_to(pos[:, None], (pos.size, LANES)).copy()


PEER_POS_INVALID = 1024.0


def _extract_max(s, ids, big):
    m = jnp.max(s, axis=0, keepdims=True)
    ix = jnp.min(jnp.where(s == m, ids, big), axis=0, keepdims=True)
    return m, ix


def _topk_pairs(a, b, ida, idb, big, k):
    a_wins = a >= b
    win, idw = jnp.maximum(a, b), jnp.where(a_wins, ida, idb)
    los, idl = jnp.minimum(a, b), jnp.where(a_wins, idb, ida)
    vals, sel = [], []
    for _ in range(k):
        m, ix = _extract_max(win, idw, big)
        vals.append(m)
        sel.append(ix)
        hit = idw == ix
        win = jnp.where(hit, los, win)
        idw = jnp.where(hit, idl, idw)
        los = jnp.where(hit, -jnp.inf, los)
    return jnp.concatenate(vals, axis=0), jnp.concatenate(sel, axis=0)


def _topk_keys(st, k):
    n = st.shape[0]
    a = jnp.concatenate([st[r:r + 8] for r in range(0, n, 16)], axis=0)
    b = jnp.concatenate([st[r + 8:r + 16] for r in range(0, n, 16)], axis=0)
    r = lax.broadcasted_iota(jnp.int32, a.shape, 0)
    ida = (((r >> 3) << 4) + (r & 7)).astype(F32)
    return _topk_pairs(a, b, ida, ida + 8.0, float(n), k)


def _route_head(q_ref, sk_ref, pos, h, tm):
    k = PEER_TOPK
    nk = PEER_NKEYS
    sub = lax.broadcasted_iota(jnp.int32, (8, tm), 0)

    def bc(x, r):
        return jnp.broadcast_to(x[r:r + 1], (8, tm))

    tops = []
    for p in range(2):
        hp = 2 * h + p
        qh = q_ref[:, pl.ds(pl.multiple_of(hp * nk, nk), nk)]
        st = _dot_nt(sk_ref[hp], qh)
        tops.append(_topk_keys(st, k))
    (s0, i0), (s1, i1) = tops
    lo1 = s1[0:8]
    tiles = [
        bc(s0, 0) + lo1,
        bc(s0, 0) + s1[8:16],
        bc(s0, 1) + lo1,
        bc(s0, 2) + lo1,
        jnp.where(sub < 4, bc(s0, 3), bc(s0, 4)) + jnp.where(sub < 4, lo1, pltpu.roll(lo1, 4, 0)),
        jnp.where(sub < 2, bc(s0, 5), jnp.where(sub < 4, bc(s0, 6), bc(s0, 7)))
        + jnp.where((sub & 1) == 0, bc(s1, 0), bc(s1, 1)),
        s0[8:16] + bc(s1, 0),
        jnp.full((8, tm), -jnp.inf, F32)]
    half = pos.shape[0] // 2
    pos_a, pos_b = pos[:half], pos[half:]
    ca = jnp.where(pos_a < PEER_POS_INVALID, jnp.concatenate(tiles[0::2], axis=0), -jnp.inf)
    cb = jnp.where(pos_b < PEER_POS_INVALID, jnp.concatenate(tiles[1::2], axis=0), -jnp.inf)
    best, bpos = _topk_pairs(ca, cb, pos_a, pos_b, 2.0 * PEER_POS_INVALID, k)
    bpos = bpos.astype(jnp.int32)
    ra = bpos >> 4
    rb = bpos & (k - 1)
    isel = jnp.zeros((k, tm), F32)
    jsel = jnp.zeros((k, tm), F32)
    for a in range(k):
        isel = jnp.where(ra == a, i0[a:a + 1], isel)
        jsel = jnp.where(rb == a, i1[a:a + 1], jsel)
    e = jnp.exp(best - best[0:1])
    gates = e / jnp.sum(e, axis=0, keepdims=True)
    return isel, jsel, gates


W3_GROUP = 8
W3_BATCH = 2 * W3_GROUP
ROUTE_HEADS_PER_TRIP = 2


def _peer_route_weights_body(x_ref, xn_ref, wq_ref, sk_ref, pos_ref, o_ref, q_ref, ri_ref, rj_ref,
                             rg_ref, pt_ref, w_ref, *, tm):
    k = PEER_TOPK
    nk = PEER_NKEYS
    ngroup = tm // LANES
    nbatch = tm // W3_BATCH
    per_head = nbatch // PEER_HEADS

    step = pl.program_id(0)
    cur = lax.rem(step, 2)
    q_cur = q_ref.at[cur]

    @pl.when(step == 0)
    def _():
        q_ref[0] = _dot(x_ref[...], wq_ref[...]).astype(q_ref.dtype)
        ri_ref[...] = jnp.zeros_like(ri_ref)
        rj_ref[...] = jnp.zeros_like(rj_ref)
        rg_ref[...] = jnp.zeros_like(rg_ref)

    for a, r_ref in enumerate((ri_ref, rj_ref, rg_ref)):
        for c in range(ngroup):
            pt_ref[a, c * LANES:(c + 1) * LANES, :] = r_ref[:, c * LANES:(c + 1) * LANES].T
    pos = jnp.concatenate([pos_ref[...]] * ngroup, axis=1)
    sub = lax.broadcasted_iota(jnp.int32, (nk, LANES), 0).astype(F32)

    def build(t):
        slot = lax.rem(t, 2 * ROUTE_HEADS_PER_TRIP * per_head)
        for half in range(2):
            tok0 = pl.multiple_of(t * W3_BATCH + half * W3_GROUP, W3_GROUP)
            it, jt, gt = (pt_ref[a, pl.ds(tok0, W3_GROUP), :] for a in range(3))
            for s in range(W3_GROUP):
                at = jnp.where(sub == it[s:s + 1], gt[s:s + 1], 0.0).astype(BF16)
                bt = jnp.where(sub == jt[s:s + 1], 1.0, 0.0).astype(BF16)
                w_ref[2 * slot + half, pl.ds(s, nk, stride=W3_GROUP), :] = _dot_nt(at, bt)

    def flush(t):
        slot = lax.rem(t, 2 * ROUTE_HEADS_PER_TRIP * per_head)
        r0 = pl.multiple_of(t * W3_BATCH, W3_BATCH)
        for i in range(nk):
            rows = jnp.concatenate([w_ref[2 * slot, i * W3_GROUP:(i + 1) * W3_GROUP, :],
                                    w_ref[2 * slot + 1, i * W3_GROUP:(i + 1) * W3_GROUP, :]], axis=0)
            o_ref[pl.ds(r0, W3_BATCH), i * nk:(i + 1) * nk] = rows.astype(o_ref.dtype)

    hpt = ROUTE_HEADS_PER_TRIP
    ntrip = PEER_HEADS // hpt
    qcols = q_ref.shape[2] // ntrip

    def do_heads(u, flush_prev):
        if flush_prev:
            for t in range(hpt * per_head):
                flush((u - 1) * hpt * per_head + t)
        for hh in range(hpt):
            h = u * hpt + hh
            isel, jsel, gates = _route_head(q_cur, sk_ref, pos, h, tm)
            rows = pl.ds(pl.multiple_of(h * k, k), k)
            ri_ref[rows, :] = isel
            rj_ref[rows, :] = jsel
            rg_ref[rows, :] = gates
            for t in range(per_head):
                build(h * per_head + t)
        cols = pl.ds(pl.multiple_of(u * qcols, qcols), qcols)
        q_ref[1 - cur, :, cols] = _dot(xn_ref[...], wq_ref[:, cols]).astype(q_ref.dtype)

    def trip(u, carry):
        do_heads(u, True)
        return carry

    do_heads(jnp.int32(0), False)
    lax.fori_loop(1, ntrip, trip, 0)
    for t in range(hpt * per_head):
        flush(jnp.int32((ntrip - 1) * hpt * per_head + t))


def peer_route_weights(x, w_q, sub_keys, tm=256):
    n, d = x.shape
    dq = w_q.shape[1]
    tm = min(tm, n)
    nblk = n // tm
    ne = PEER_NKEYS * PEER_NKEYS
    npair = PEER_HEADS * PEER_TOPK
    pos = jnp.asarray(_peer_cand_layout())
    return pl.pallas_call(
        functools.partial(_peer_route_weights_body, tm=tm),
        out_shape=jax.ShapeDtypeStruct((n, ne), BF16),
        grid=(nblk + 1,),
        in_specs=[pl.BlockSpec((tm, d), lambda s: (jnp.minimum(s, nblk - 1), 0)),
                  pl.BlockSpec((tm, d), lambda s: (jnp.minimum(s + 1, nblk - 1), 0)),
                  pl.BlockSpec((d, dq), lambda s: (0, 0), pipeline_mode=pl.Buffered(1)),
                  pl.BlockSpec(sub_keys.shape, lambda s: (0, 0, 0)),
                  pl.BlockSpec(pos.shape, lambda s: (0, 0))],
        out_specs=pl.BlockSpec((tm, ne), lambda s: (jnp.maximum(s - 1, 0), 0)),
        scratch_shapes=[pltpu.VMEM((2, tm, dq), BF16)]
        + [pltpu.VMEM((npair, tm), F32)] * 3 + [pltpu.VMEM((3, tm, npair), F32)]
        + [pltpu.VMEM((4 * ROUTE_HEADS_PER_TRIP * tm // (W3_BATCH * PEER_HEADS),
                       W3_GROUP * PEER_NKEYS, PEER_NKEYS), F32)],
        compiler_params=_cparams(("arbitrary",)),
        name="peer_route_weights",
    )(x, x, w_q, sub_keys, pos)


DENSE_SUBTILE = 1024


def _gelu(x):
    return 0.5 * x * (1.0 + lax.erf(x * (1.0 / math.sqrt(2.0))))


def _peer_dense_body(x_ref, u_ref, v_ref, w_ref, h_ref, gn_ref, ho_ref, hn_ref, acc_ref, *, nj):
    j = pl.program_id(1)

    @pl.when(j == 0)
    def _():
        acc_ref[...] = jnp.zeros_like(acc_ref)

    x = x_ref[...]
    te = u_ref.shape[0]
    for s in range(te // min(te, DENSE_SUBTILE)):
        rows = slice(s * DENSE_SUBTILE, (s + 1) * DENSE_SUBTILE)
        hid = _dot_nt(x, u_ref[rows, :])
        a = (_gelu(hid) * w_ref[:, rows].astype(F32)).astype(BF16)
        acc_ref[...] += _dot(a, v_ref[rows, :])

    @pl.when(j == nj - 1)
    def _():
        hnew = h_ref[...] + acc_ref[...]
        ho_ref[...] = hnew
        hn_ref[...] = _rms(hnew, gn_ref[...]).astype(hn_ref.dtype)


def peer_dense(x, u, v, layer, w, h, gain_next, tm=1024, te=2048):
    n, d = x.shape
    ne = u.shape[1]
    tm = min(tm, n)
    te = min(te, ne)
    nj = ne // te
    rowspec_in = pl.BlockSpec((tm, d), lambda i, j: (i, 0), pipeline_mode=pl.Buffered(1))
    rowspec_out = pl.BlockSpec((tm, d), lambda i, j: (i, 0), pipeline_mode=pl.Buffered(1))
    return pl.pallas_call(
        functools.partial(_peer_dense_body, nj=nj),
        out_shape=[jax.ShapeDtypeStruct((n, d), F32), jax.ShapeDtypeStruct((n, d), BF16)],
        grid=(n // tm, nj),
        in_specs=[rowspec_in,
                  pl.BlockSpec((None, te, d), lambda i, j: (layer, j, 0)),
                  pl.BlockSpec((None, te, d), lambda i, j: (layer, j, 0)),
                  pl.BlockSpec((tm, te), lambda i, j: (i, j)),
                  rowspec_in,
                  pl.BlockSpec((1, d), lambda i, j: (0, 0))],
        out_specs=[rowspec_out, rowspec_out],
        scratch_shapes=[pltpu.VMEM((tm, d), F32)],
        compiler_params=_cparams(("parallel", "arbitrary")),
        name="peer_dense",
    )(x, u, v, w, h, gain_next.astype(F32).reshape(1, d))


def _fox_layer(hn, b, t, w_in, b_f, q_gain, k_gain):
    d = hn.shape[1]
    w = w_in.astype(BF16)
    qkvg = fox_proj(hn, w[:, 0:4 * d], q_gain, k_gain, FOX_HEAD_DIM ** -0.5, FOX_HEAD_DIM)
    lf = mm_logsig(hn, w[:, 4 * d:], b_f, 1.0)
    c = time_cumsum(lf.reshape(b, t, FOX_HEADS))
    c_rows = c.transpose(0, 2, 1).reshape(b, FOX_HEADS // 2, 2, t)
    o = fox_attention(qkvg.reshape(b, t, 4 * d), c_rows)
    return o.reshape(b * t, d)


def _gla_layer(hn, b, t, w_in, w_up, b_alpha, out_gain):
    d = hn.shape[1]
    gk = GLA_HEADS * GLA_DK
    w = w_in.astype(BF16)
    q = mm_plain(hn, w[:, 0:gk], F32, scale=GLA_DK ** -0.5)
    k = mm_plain(hn, w[:, gk:2 * gk], F32)
    v = mm_plain(hn, w[:, 2 * gk:2 * gk + d], BF16)
    r = mm_plain(hn, w[:, 2 * gk + d:2 * gk + 2 * d], BF16)
    low = mm_plain(hn, w[:, 2 * gk + 2 * d:], BF16)
    log_a = mm_logsig(low, w_up.astype(BF16), b_alpha, 1.0 / GLA_TAU)
    o = gated_linear_attention(q.reshape(b, t, gk), k.reshape(b, t, gk), log_a.reshape(b, t, gk),
                               v.reshape(b, t, d), r.reshape(b, t, d), out_gain,
                               GLA_HEADS, GLA_DK, GLA_DV)
    return o.reshape(b * t, d)


def _hgrn_layer(hn, b, t, layer, w_in, lb_logits, out_gain):
    d = hn.shape[1]
    w = w_in.astype(BF16)
    q = mm_plain(hn, w[:, 0:d], F32, scale=HGRN_DK ** -0.5)
    k, log_g = mm_hgrn_gate(hn, w[:, d:2 * d], lb_logits, layer)
    v = mm_plain(hn, w[:, 2 * d:3 * d], BF16)
    r = mm_plain(hn, w[:, 3 * d:4 * d], BF16)
    shp = (b, t, d)
    o = gated_linear_attention(q.reshape(shp), k.reshape(shp), log_g.reshape(shp),
                               v.reshape(shp), r.reshape(shp), out_gain,
                               HGRN_HEADS, HGRN_DK, HGRN_DV)
    return o.reshape(b * t, d)


def _peer_layer(h, hn, w_q, sub_keys, u, v, layer, gain_next):
    sk = sub_keys.astype(BF16).reshape(2 * PEER_HEADS, PEER_NKEYS, -1)
    w = peer_route_weights(hn, w_q.astype(BF16), sk)
    return peer_dense(hn, u, v, layer, w, h, gain_next)


def kernel(x, norm_mix, norm_ffn, fox_w_in, fox_b_f, fox_q_gain, fox_k_gain, fox_w_out,
           gla_w_in, gla_w_up, gla_b_alpha, gla_out_gain, gla_w_out,
           hgrn_w_in, hgrn_lb_logits, hgrn_out_gain, hgrn_w_out,
           peer_w_q, peer_sub_keys, peer_u, peer_v):
    b, t, d = x.shape
    depth = norm_mix.shape[0]
    h = x.reshape(b * t, d)
    hn = rmsnorm(h, norm_mix[0])
    u_all, v_all = peer_u.astype(BF16), peer_v.astype(BF16)
    for i in range(depth):
        m, j = i % N_MIXERS, i // N_MIXERS
        if m == 0:
            o = _fox_layer(hn, b, t, fox_w_in[j], fox_b_f[j], fox_q_gain[j], fox_k_gain[j])
            w_out = fox_w_out[j]
        elif m == 1:
            o = _gla_layer(hn, b, t, gla_w_in[j], gla_w_up[j], gla_b_alpha[j], gla_out_gain[j])
            w_out = gla_w_out[j]
        else:
            o = _hgrn_layer(hn, b, t, i, hgrn_w_in[j], hgrn_lb_logits, hgrn_out_gain[j])
            w_out = hgrn_w_out[j]
        h, hn = mm_res_norm(o, w_out.astype(BF16), h, norm_ffn[i])
        gain_next = norm_mix[(i + 1) % depth]
        h, hn = _peer_layer(h, hn, peer_w_q[i], peer_sub_keys[i], u_all, v_all, i, gain_next)
    return h.reshape(b, t, d)
```

```python
import functools
import math

import numpy as np
import jax
import jax.numpy as jnp
from jax import lax
from jax.experimental import pallas as pl
from jax.experimental.pallas import tpu as pltpu

F32 = jnp.float32
BF16 = jnp.bfloat16
EPS = 1e-6

DEPTH = 4
N_MIXERS = 3
CHUNK = 128
FOX_HEADS = 16
FOX_HEAD_DIM = 64
GLA_HEADS = 4
GLA_DK = 128
GLA_DV = 256
GLA_TAU = 16.0
HGRN_HEADS = 8
HGRN_DK = 128
HGRN_DV = 128
PEER_HEADS = 8
PEER_NKEYS = 128
PEER_TOPK = 16
LANES = 128
VMEM_LIMIT = 56 * 1024 * 1024


def _cparams(sem):
    return pltpu.CompilerParams(dimension_semantics=sem, vmem_limit_bytes=VMEM_LIMIT)


def _dot(a, b):
    return jnp.dot(a, b, preferred_element_type=F32)


def _dot_nt(a, b):
    return lax.dot_general(a, b, (((1,), (1,)), ((), ())), preferred_element_type=F32)


def _dot_tn(a, b):
    return lax.dot_general(a, b, (((0,), (0,)), ((), ())), preferred_element_type=F32)


def _split2(x):
    hi = x.astype(BF16)
    lo = (x - hi.astype(F32)).astype(BF16)
    return hi, lo


def _split3(x):
    hi = x.astype(BF16)
    r = x - hi.astype(F32)
    mid = r.astype(BF16)
    lo = (r - mid.astype(F32)).astype(BF16)
    return hi, mid, lo


def _log_sigmoid(x):
    return jnp.minimum(x, 0.0) - jnp.log1p(jnp.exp(-jnp.abs(x)))


def _sigmoid(x):
    return 1.0 / (1.0 + jnp.exp(-x))


def _rms(x, gain):
    ms = jnp.mean(x * x, axis=-1, keepdims=True)
    return x * lax.rsqrt(ms + EPS) * gain


def _rmsnorm_body(x_ref, g_ref, o_ref):
    o_ref[...] = _rms(x_ref[...], g_ref[...]).astype(o_ref.dtype)


def rmsnorm(x, gain, tm=512):
    n, d = x.shape
    tm = min(tm, n)
    return pl.pallas_call(
        _rmsnorm_body,
        out_shape=jax.ShapeDtypeStruct((n, d), BF16),
        grid=(n // tm,),
        in_specs=[pl.BlockSpec((tm, d), lambda i: (i, 0)),
                  pl.BlockSpec((1, d), lambda i: (0, 0))],
        out_specs=pl.BlockSpec((tm, d), lambda i: (i, 0)),
        compiler_params=_cparams(("parallel",)),
        name="rmsnorm",
    )(x, gain.reshape(1, d))


def _mm_call(body, x, w, extras, extra_specs, out_dtypes, tm, tn, name):
    n, k = x.shape
    m = w.shape[1]
    tm = min(tm, n)
    tn = min(tn, m)
    outs = [jax.ShapeDtypeStruct((n, m), dt) for dt in out_dtypes]
    return pl.pallas_call(
        body,
        out_shape=outs,
        grid=(n // tm, m // tn),
        in_specs=[pl.BlockSpec((tm, k), lambda i, j: (i, 0)),
                  pl.BlockSpec((k, tn), lambda i, j: (0, j))] + extra_specs,
        out_specs=[pl.BlockSpec((tm, tn), lambda i, j: (i, j)) for _ in outs],
        compiler_params=_cparams(("parallel", "arbitrary")),
        name=name,
    )(x, w, *extras)


def _mm_plain_body(x_ref, w_ref, o_ref, *, scale):
    acc = _dot(x_ref[...], w_ref[...])
    if scale != 1.0:
        acc = acc * scale
    o_ref[...] = acc.astype(o_ref.dtype)


def mm_plain(x, w, out_dtype, scale=1.0, tm=1024, tn=1024):
    return _mm_call(functools.partial(_mm_plain_body, scale=scale), x, w, [], [],
                    [out_dtype], tm, tn, "mm_plain")[0]


MXU_DIM = 256


def _fox_proj_body(x_ref, w_ref, bd_ref, g_ref, o_ref, *, hd):
    j = pl.program_id(1)
    acc = _dot(x_ref[...], w_ref[...])

    @pl.when(j < 2)
    def _():
        for c in range(acc.shape[1] // MXU_DIM):
            cols = slice(c * MXU_DIM, (c + 1) * MXU_DIM)
            a = acc[:, cols]
            ms = _dot((a * a).astype(BF16), bd_ref[...]) * (1.0 / hd)
            o_ref[:, cols] = (a * lax.rsqrt(ms + EPS) * g_ref[0, :, cols]).astype(o_ref.dtype)

    @pl.when(j >= 2)
    def _():
        o_ref[...] = acc.astype(o_ref.dtype)


def fox_proj(x, w, q_gain, k_gain, scale, hd, tm=1024):
    n, d = x.shape
    tm = min(tm, n)
    blk = np.arange(MXU_DIM) // hd
    bd = jnp.asarray(blk[:, None] == blk[None, :], BF16)
    g = jnp.stack([jnp.tile(q_gain.astype(F32) * scale, d // hd),
                   jnp.tile(k_gain.astype(F32), d // hd)]).reshape(2, 1, d)
    return pl.pallas_call(
        functools.partial(_fox_proj_body, hd=hd),
        out_shape=jax.ShapeDtypeStruct((n, 4 * d), BF16),
        grid=(n // tm, 4),
        in_specs=[pl.BlockSpec((tm, d), lambda i, j: (i, 0)),
                  pl.BlockSpec((d, d), lambda i, j: (0, j)),
                  pl.BlockSpec((MXU_DIM, MXU_DIM), lambda i, j: (0, 0)),
                  pl.BlockSpec((1, 1, d), lambda i, j: (jnp.minimum(j, 1), 0, 0))],
        out_specs=pl.BlockSpec((tm, d), lambda i, j: (i, j)),
        compiler_params=_cparams(("parallel", "arbitrary")),
        name="fox_proj",
    )(x, w, bd, g)


def _mm_logsig_body(x_ref, w_ref, b_ref, o_ref, *, mult):
    acc = _dot(x_ref[...], w_ref[...]) + b_ref[...]
    o_ref[...] = (_log_sigmoid(acc) * mult).astype(o_ref.dtype)


def mm_logsig(x, w, bias, mult, tm=512, tn=512):
    m = w.shape[1]
    tn = min(tn, m)
    return _mm_call(functools.partial(_mm_logsig_body, mult=mult), x, w,
                    [bias.astype(F32).reshape(1, m)],
                    [pl.BlockSpec((1, tn), lambda i, j: (0, j))],
                    [F32], tm, tn, "mm_logsig")[0]


def _mm_hgrn_gate_body(x_ref, w_ref, lbl_ref, k_ref, lg_ref, *, layer):
    acc = _dot(x_ref[...], w_ref[...])
    logits = lbl_ref[...]
    e = jnp.exp(logits - jnp.max(logits, axis=0, keepdims=True))
    p = e / jnp.sum(e, axis=0, keepdims=True)
    lb = jnp.zeros_like(p[0:1])
    for u in range(1, layer + 1):
        lb = lb + p[u:u + 1]
    gate = lb + (1.0 - lb) * _sigmoid(acc)
    k_ref[...] = (1.0 - gate).astype(k_ref.dtype)
    lg_ref[...] = jnp.log(gate).astype(lg_ref.dtype)


def mm_hgrn_gate(x, w, lb_logits, layer, tm=1024, tn=1024):
    m = w.shape[1]
    tn = min(tn, m)
    depth = lb_logits.shape[0]
    return _mm_call(functools.partial(_mm_hgrn_gate_body, layer=layer), x, w,
                    [lb_logits.astype(F32)],
                    [pl.BlockSpec((depth, tn), lambda i, j: (0, j))],
                    [F32, F32], tm, tn, "mm_hgrn_gate")


def _mm_res_norm_body(x_ref, w_ref, h_ref, g_ref, ho_ref, hn_ref):
    hnew = h_ref[...] + _dot(x_ref[...], w_ref[...])
    ho_ref[...] = hnew
    hn_ref[...] = _rms(hnew, g_ref[...]).astype(hn_ref.dtype)


def mm_res_norm(x, w, h, gain, tm=512):
    n, d = h.shape
    return _mm_call(_mm_res_norm_body, x, w, [h, gain.astype(F32).reshape(1, d)],
                    [pl.BlockSpec((min(tm, n), d), lambda i, j: (i, 0)),
                     pl.BlockSpec((1, d), lambda i, j: (0, 0))],
                    [F32, BF16], tm, d, "mm_res_norm")


CUM_BLOCK = 256


def _cumsum_body(x_ref, o_ref, *, t):
    nb = t // CUM_BLOCK
    r = lax.broadcasted_iota(jnp.int32, (CUM_BLOCK, CUM_BLOCK), 0)
    c = lax.broadcasted_iota(jnp.int32, (CUM_BLOCK, CUM_BLOCK), 1)
    tril = jnp.where(r >= c, 1.0, 0.0).astype(BF16)
    carry = jnp.zeros((1, x_ref.shape[-1]), F32)
    for b in range(nb):
        x = x_ref[0, b * CUM_BLOCK:(b + 1) * CUM_BLOCK, :]
        hi, mid, lo = _split3(x)
        cs = _dot(tril, hi) + _dot(tril, mid) + _dot(tril, lo) + carry
        o_ref[0, b * CUM_BLOCK:(b + 1) * CUM_BLOCK, :] = cs
        carry = cs[CUM_BLOCK - 1:CUM_BLOCK, :]


def time_cumsum(x):
    b, t, h = x.shape
    return pl.pallas_call(
        functools.partial(_cumsum_body, t=t),
        out_shape=jax.ShapeDtypeStruct((b, t, h), F32),
        grid=(b,),
        in_specs=[pl.BlockSpec((1, t, h), lambda i: (i, 0, 0))],
        out_specs=pl.BlockSpec((1, t, h), lambda i: (i, 0, 0)),
        compiler_params=_cparams(("parallel",)),
        name="time_cumsum",
    )(x)


FOX_PAIRS_PER_STEP = 2


def _fox_body(q_ref, k_ref, v_ref, g_ref, c_ref, o_ref, s_ref, m_ref, l_ref, acc_ref, *, tq):
    qi = pl.program_id(2)
    q = q_ref[0]
    width = q.shape[1]
    nh = width // FOX_HEAD_DIM
    lane = lax.broadcasted_iota(jnp.int32, (tq, width), 1)
    row = lax.broadcasted_iota(jnp.int32, (tq, tq), 0)
    col = lax.broadcasted_iota(jnp.int32, (tq, tq), 1)
    zero = jnp.zeros_like(q)
    head_of_lane = lane // FOX_HEAD_DIM
    qs = [jnp.where(head_of_lane == j, q, zero) for j in range(nh)]
    nfold = tq // LANES

    def fold(x, op):
        r = x[:, 0:LANES]
        for f in range(1, nfold):
            r = op(r, x[:, f * LANES:(f + 1) * LANES])
        return r

    def scores(j, off):
        ks = k_ref[0, pl.ds(off, tq), :]
        return _dot_nt(qs[j], ks) - c_ref[0, j // 2, pl.ds(j % 2, 1), pl.ds(off, tq)]

    def values(j, off):
        return v_ref[0, pl.ds(off, tq), (j // 2) * LANES:(j // 2 + 1) * LANES]

    m_ref[...] = jnp.full(m_ref.shape, -jnp.inf, F32)
    l_ref[...] = jnp.zeros(l_ref.shape, F32)
    acc_ref[...] = jnp.zeros(acc_ref.shape, F32)

    def pass1(kb, carry):
        off = pl.multiple_of(kb * tq, tq)
        for j in range(nh):
            s = scores(j, off)
            s_ref[j, :, pl.ds(off, tq)] = s
            m_ref[j] = jnp.maximum(m_ref[j], fold(s, jnp.maximum))
        return carry

    lax.fori_loop(0, qi, pass1, 0)
    offd = pl.multiple_of(qi * tq, tq)
    sd, ms = [], []
    for j in range(nh):
        s = jnp.where(row >= col, scores(j, offd), -jnp.inf)
        sd.append(s)
        ms.append(jnp.max(jnp.maximum(m_ref[j], fold(s, jnp.maximum)), axis=-1, keepdims=True))

    def pass2(kb, carry):
        off = pl.multiple_of(kb * tq, tq)
        for j in range(nh):
            p = jnp.exp(s_ref[j, :, pl.ds(off, tq)] - ms[j])
            l_ref[j] += fold(p, jnp.add)
            acc_ref[j] += _dot(p.astype(BF16), values(j, off))
        return carry

    lax.fori_loop(0, qi, pass2, 0)
    outs = []
    for j in range(nh):
        p = jnp.exp(sd[j] - ms[j])
        l = jnp.sum(l_ref[j] + fold(p, jnp.add), axis=-1, keepdims=True)
        outs.append((acc_ref[j] + _dot(p.astype(BF16), values(j, offd))) / l)
    lane1 = lax.broadcasted_iota(jnp.int32, (tq, LANES), 1)
    o = jnp.concatenate([jnp.where(lane1 < FOX_HEAD_DIM, outs[2 * pp], outs[2 * pp + 1])
                         for pp in range(nh // 2)], axis=1)
    o = o * _sigmoid(g_ref[0].astype(F32))
    o_ref[0] = o.astype(o_ref.dtype)


def fox_attention(qkvg, c_rows, tq=512):
    b, t, d4 = qkvg.shape
    d = d4 // 4
    tq = min(tq, t)
    pps = FOX_PAIRS_PER_STEP
    width = pps * LANES
    ngrp = d // width
    nh = 2 * pps
    qspec = pl.BlockSpec((1, tq, width), lambda bi, p, i: (bi, i, p))
    return pl.pallas_call(
        functools.partial(_fox_body, tq=tq),
        out_shape=jax.ShapeDtypeStruct((b, t, d), BF16),
        grid=(b, ngrp, t // tq),
        in_specs=[qspec,
                  pl.BlockSpec((1, t, width), lambda bi, p, i: (bi, 0, ngrp + p)),
                  pl.BlockSpec((1, t, width), lambda bi, p, i: (bi, 0, 2 * ngrp + p)),
                  pl.BlockSpec((1, tq, width), lambda bi, p, i: (bi, i, 3 * ngrp + p)),
                  pl.BlockSpec((1, pps, 2, t), lambda bi, p, i: (bi, p, 0, 0))],
        out_specs=qspec,
        scratch_shapes=[pltpu.VMEM((nh, tq, t), F32), pltpu.VMEM((nh, tq, LANES), F32),
                        pltpu.VMEM((nh, tq, LANES), F32), pltpu.VMEM((nh, tq, LANES), F32)],
        compiler_params=_cparams(("parallel", "parallel", "arbitrary")),
        name="fox_attention",
    )(qkvg, qkvg, qkvg, qkvg, c_rows)


_LEVELS = tuple(CHUNK >> s for s in range(CHUNK.bit_length() - 1))


def _gla_consts():
    c = CHUNK
    t = np.arange(c)
    u = t[None, :]
    blocks = [u <= t[:, None]]
    masks = [np.eye(c, dtype=bool)]
    for lv in _LEVELS:
        half = lv // 2
        blk, pos = t // lv, t % lv
        r = blk * lv + half - 1
        upper = pos >= half
        qrow = upper[:, None] & (u > r[:, None]) & (u <= t[:, None])
        krow = (~upper)[:, None] & (u > t[:, None]) & (u <= r[:, None])
        blocks.append(qrow | krow)
        masks.append((blk[:, None] == blk[None, :]) & upper[:, None] & (~upper)[None, :])
    blocks.append(u > t[:, None])
    mall = np.concatenate(blocks, axis=0).astype(np.float32)
    mask = np.stack(masks).astype(np.float32)
    mall = np.concatenate([mall, mall], axis=1)
    return jnp.asarray(mall, BF16), jnp.asarray(mask, F32)


def _gla_body(q_ref, k_ref, g_ref, v_ref, r_ref, gain_ref, mall_ref, mask_ref,
              o_ref, st_ref, *, nchunk, hg, dk, dv):
    c = CHUNK
    nl = len(_LEVELS)

    @pl.when(pl.program_id(2) == 0)
    def _():
        st_ref[...] = jnp.zeros_like(st_ref)

    def chunk(ci, carry):
        sl = pl.ds(pl.multiple_of(ci * c, c), c)
        g2 = jnp.concatenate(_split2(g_ref[0, sl, :]), axis=0)
        xall = jnp.exp(_dot(mall_ref[...], g2))
        for hh in range(hg):
            ks, vs = slice(hh * dk, (hh + 1) * dk), slice(hh * dv, (hh + 1) * dv)
            q = q_ref[0, sl, ks]
            k = k_ref[0, sl, ks]
            v = v_ref[0, sl, vs]
            x = xall[:, ks]
            scores = mask_ref[0] * _dot_nt(q.astype(BF16), k.astype(BF16))
            for lv in range(nl):
                z = x[(1 + lv) * c:(2 + lv) * c]
                scores = scores + mask_ref[1 + lv] * _dot_nt((q * z).astype(BF16), (k * z).astype(BF16))
            st = st_ref[hh]
            o = _dot(scores.astype(BF16), v) + _dot_nt((q * x[0:c]).astype(BF16), st.astype(BF16))
            kt = (k * x[(1 + nl) * c:(2 + nl) * c]).astype(BF16)
            st_ref[hh] = st * x[c - 1:c, :] + _dot_tn(v, kt)
            r = r_ref[0, sl, vs].astype(F32)
            y = _rms(o, gain_ref[hh]) * (r * _sigmoid(r))
            o_ref[0, sl, vs] = y.astype(o_ref.dtype)
        return carry

    lax.fori_loop(0, nchunk, chunk, 0, unroll=2)


def gated_linear_attention(q, k, g, v, r, gain, heads, dk, dv, hg=4, ts=512):
    b, t, _ = q.shape
    ts = min(ts, t)
    mall, mask = _gla_consts()
    qspec = pl.BlockSpec((1, ts, hg * dk), lambda bi, h, i: (bi, i, h))
    vspec = pl.BlockSpec((1, ts, hg * dv), lambda bi, h, i: (bi, i, h))
    return pl.pallas_call(
        functools.partial(_gla_body, nchunk=ts // CHUNK, hg=hg, dk=dk, dv=dv),
        out_shape=jax.ShapeDtypeStruct((b, t, heads * dv), BF16),
        grid=(b, heads // hg, t // ts),
        in_specs=[qspec, qspec, qspec, vspec, vspec,
                  pl.BlockSpec((hg, 1, dv), lambda bi, h, i: (h, 0, 0)),
                  pl.BlockSpec(mall.shape, lambda bi, h, i: (0, 0)),
                  pl.BlockSpec(mask.shape, lambda bi, h, i: (0, 0, 0))],
        out_specs=vspec,
        scratch_shapes=[pltpu.VMEM((hg, dv, dk), F32)],
        compiler_params=_cparams(("parallel", "parallel", "arbitrary")),
        name="gated_linear_attention",
    )(q, k, g, v, r, gain.astype(F32).reshape(heads, 1, dv), mall, mask)


def _peer_cand_layout():
    k = PEER_TOPK
    ab = [(0, b) for b in range(16)] + [(1, b) for b in range(8)]
    ab += [(2, b) if b < 5 else None for b in range(8)]
    ab += [(3, 0), (3, 1), (3, 2), (3, 3), (4, 0), (4, 1), (4, 2), None]
    ab += [(5, 0), (5, 1), (6, 0), (6, 1), (7, 0), (7, 1), None, None]
    ab += [(a, 0) for a in range(8, 16)]
    assert all(p is None or (p[0] + 1) * (p[1] + 1) <= k for p in ab)
    assert sum(p is not None for p in ab) == sum(k // (a + 1) for a in range(k))
    pos = np.array([PEER_POS_INVALID if p is None else p[0] * k + p[1] for p in ab], np.float32)
    tiles = pos.reshape(7, 8)
    first = tiles[0::2].reshape(-1)
    second = np.concatenate([tiles[1::2].reshape(-1), np.full(8, PEER_POS_INVALID, np.float32)])
    assert np.all((first < second) | (first == PEER_POS_INVALID))
    pos = np.concatenate([first, second])
    return np.broadcast_to(pos[:, None], (pos.size, LANES)).copy()


PEER_POS_INVALID = 1024.0


def _extract_max(s, ids, big):
    m = jnp.max(s, axis=0, keepdims=True)
    ix = jnp.min(jnp.where(s == m, ids, big), axis=0, keepdims=True)
    return m, ix


def _topk_pairs(a, b, ida, idb, big, k):
    a_wins = a >= b
    win, idw = jnp.maximum(a, b), jnp.where(a_wins, ida, idb)
    los, idl = jnp.minimum(a, b), jnp.where(a_wins, idb, ida)
    vals, sel = [], []
    for _ in range(k):
        m, ix = _extract_max(win, idw, big)
        vals.append(m)
        sel.append(ix)
        hit = idw == ix
        win = jnp.where(hit, los, win)
        idw = jnp.where(hit, idl, idw)
        los = jnp.where(hit, -jnp.inf, los)
    return jnp.concatenate(vals, axis=0), jnp.concatenate(sel, axis=0)


def _topk_keys(st, k):
    n = st.shape[0]
    a = jnp.concatenate([st[r:r + 8] for r in range(0, n, 16)], axis=0)
    b = jnp.concatenate([st[r + 8:r + 16] for r in range(0, n, 16)], axis=0)
    r = lax.broadcasted_iota(jnp.int32, a.shape, 0)
    ida = (((r >> 3) << 4) + (r & 7)).astype(F32)
    return _topk_pairs(a, b, ida, ida + 8.0, float(n), k)


def _route_head(q_ref, sk_ref, pos, h, tm):
    k = PEER_TOPK
    nk = PEER_NKEYS
    sub = lax.broadcasted_iota(jnp.int32, (8, tm), 0)

    def bc(x, r):
        return jnp.broadcast_to(x[r:r + 1], (8, tm))

    tops = []
    for p in range(2):
        hp = 2 * h + p
        qh = q_ref[:, pl.ds(pl.multiple_of(hp * nk, nk), nk)]
        st = _dot_nt(sk_ref[hp], qh)
        tops.append(_topk_keys(st, k))
    (s0, i0), (s1, i1) = tops
    lo1 = s1[0:8]
    tiles = [
        bc(s0, 0) + lo1,
        bc(s0, 0) + s1[8:16],
        bc(s0, 1) + lo1,
        bc(s0, 2) + lo1,
        jnp.where(sub < 4, bc(s0, 3), bc(s0, 4)) + jnp.where(sub < 4, lo1, pltpu.roll(lo1, 4, 0)),
        jnp.where(sub < 2, bc(s0, 5), jnp.where(sub < 4, bc(s0, 6), bc(s0, 7)))
        + jnp.where((sub & 1) == 0, bc(s1, 0), bc(s1, 1)),
        s0[8:16] + bc(s1, 0),
        jnp.full((8, tm), -jnp.inf, F32)]
    half = pos.shape[0] // 2
    pos_a, pos_b = pos[:half], pos[half:]
    ca = jnp.where(pos_a < PEER_POS_INVALID, jnp.concatenate(tiles[0::2], axis=0), -jnp.inf)
    cb = jnp.where(pos_b < PEER_POS_INVALID, jnp.concatenate(tiles[1::2], axis=0), -jnp.inf)
    best, bpos = _topk_pairs(ca, cb, pos_a, pos_b, 2.0 * PEER_POS_INVALID, k)
    bpos = bpos.astype(jnp.int32)
    ra = bpos >> 4
    rb = bpos & (k - 1)
    isel = jnp.zeros((k, tm), F32)
    jsel = jnp.zeros((k, tm), F32)
    for a in range(k):
        isel = jnp.where(ra == a, i0[a:a + 1], isel)
        jsel = jnp.where(rb == a, i1[a:a + 1], jsel)
    e = jnp.exp(best - best[0:1])
    gates = e / jnp.sum(e, axis=0, keepdims=True)
    return isel, jsel, gates


W3_GROUP = 8
W3_BATCH = 2 * W3_GROUP
ROUTE_HEADS_PER_TRIP = 2


def _peer_route_weights_body(x_ref, xn_ref, wq_ref, sk_ref, pos_ref, o_ref, q_ref, ri_ref, rj_ref,
                             rg_ref, pt_ref, w_ref, *, tm):
    k = PEER_TOPK
    nk = PEER_NKEYS
    ngroup = tm // LANES
    nbatch = tm // W3_BATCH
    per_head = nbatch // PEER_HEADS

    step = pl.program_id(0)
    cur = lax.rem(step, 2)
    q_cur = q_ref.at[cur]

    @pl.when(step == 0)
    def _():
        q_ref[0] = _dot(x_ref[...], wq_ref[...]).astype(q_ref.dtype)
        ri_ref[...] = jnp.zeros_like(ri_ref)
        rj_ref[...] = jnp.zeros_like(rj_ref)
        rg_ref[...] = jnp.zeros_like(rg_ref)

    for a, r_ref in enumerate((ri_ref, rj_ref, rg_ref)):
        for c in range(ngroup):
            pt_ref[a, c * LANES:(c + 1) * LANES, :] = r_ref[:, c * LANES:(c + 1) * LANES].T
    pos = jnp.concatenate([pos_ref[...]] * ngroup, axis=1)
    sub = lax.broadcasted_iota(jnp.int32, (nk, LANES), 0).astype(F32)

    def build(t):
        slot = lax.rem(t, 2 * ROUTE_HEADS_PER_TRIP * per_head)
        for half in range(2):
            tok0 = pl.multiple_of(t * W3_BATCH + half * W3_GROUP, W3_GROUP)
            it, jt, gt = (pt_ref[a, pl.ds(tok0, W3_GROUP), :] for a in range(3))
            for s in range(W3_GROUP):
                at = jnp.where(sub == it[s:s + 1], gt[s:s + 1], 0.0).astype(BF16)
                bt = jnp.where(sub == jt[s:s + 1], 1.0, 0.0).astype(BF16)
                w_ref[2 * slot + half, pl.ds(s, nk, stride=W3_GROUP), :] = _dot_nt(at, bt)

    def flush(t):
        slot = lax.rem(t, 2 * ROUTE_HEADS_PER_TRIP * per_head)
        r0 = pl.multiple_of(t * W3_BATCH, W3_BATCH)
        for i in range(nk):
            rows = jnp.concatenate([w_ref[2 * slot, i * W3_GROUP:(i + 1) * W3_GROUP, :],
                                    w_ref[2 * slot + 1, i * W3_GROUP:(i + 1) * W3_GROUP, :]], axis=0)
            o_ref[pl.ds(r0, W3_BATCH), i * nk:(i + 1) * nk] = rows.astype(o_ref.dtype)

    hpt = ROUTE_HEADS_PER_TRIP
    ntrip = PEER_HEADS // hpt
    qcols = q_ref.shape[2] // ntrip

    def do_heads(u, flush_prev):
        if flush_prev:
            for t in range(hpt * per_head):
                flush((u - 1) * hpt * per_head + t)
        for hh in range(hpt):
            h = u * hpt + hh
            isel, jsel, gates = _route_head(q_cur, sk_ref, pos, h, tm)
            rows = pl.ds(pl.multiple_of(h * k, k), k)
            ri_ref[rows, :] = isel
            rj_ref[rows, :] = jsel
            rg_ref[rows, :] = gates
            for t in range(per_head):
                build(h * per_head + t)
        cols = pl.ds(pl.multiple_of(u * qcols, qcols), qcols)
        q_ref[1 - cur, :, cols] = _dot(xn_ref[...], wq_ref[:, cols]).astype(q_ref.dtype)

    def trip(u, carry):
        do_heads(u, True)
        return carry

    do_heads(jnp.int32(0), False)
    lax.fori_loop(1, ntrip, trip, 0)
    for t in range(hpt * per_head):
        flush(jnp.int32((ntrip - 1) * hpt * per_head + t))


def peer_route_weights(x, w_q, sub_keys, tm=256):
    n, d = x.shape
    dq = w_q.shape[1]
    tm = min(tm, n)
    nblk = n // tm
    ne = PEER_NKEYS * PEER_NKEYS
    npair = PEER_HEADS * PEER_TOPK
    pos = jnp.asarray(_peer_cand_layout())
    return pl.pallas_call(
        functools.partial(_peer_route_weights_body, tm=tm),
        out_shape=jax.ShapeDtypeStruct((n, ne), BF16),
        grid=(nblk + 1,),
        in_specs=[pl.BlockSpec((tm, d), lambda s: (jnp.minimum(s, nblk - 1), 0)),
                  pl.BlockSpec((tm, d), lambda s: (jnp.minimum(s + 1, nblk - 1), 0)),
                  pl.BlockSpec((d, dq), lambda s: (0, 0), pipeline_mode=pl.Buffered(1)),
                  pl.BlockSpec(sub_keys.shape, lambda s: (0, 0, 0)),
                  pl.BlockSpec(pos.shape, lambda s: (0, 0))],
        out_specs=pl.BlockSpec((tm, ne), lambda s: (jnp.maximum(s - 1, 0), 0)),
        scratch_shapes=[pltpu.VMEM((2, tm, dq), BF16)]
        + [pltpu.VMEM((npair, tm), F32)] * 3 + [pltpu.VMEM((3, tm, npair), F32)]
        + [pltpu.VMEM((4 * ROUTE_HEADS_PER_TRIP * tm // (W3_BATCH * PEER_HEADS),
                       W3_GROUP * PEER_NKEYS, PEER_NKEYS), F32)],
        compiler_params=_cparams(("arbitrary",)),
        name="peer_route_weights",
    )(x, x, w_q, sub_keys, pos)


DENSE_SUBTILE = 1024


def _gelu(x):
    return 0.5 * x * (1.0 + lax.erf(x * (1.0 / math.sqrt(2.0))))


def _peer_dense_body(x_ref, u_ref, v_ref, w_ref, h_ref, gn_ref, ho_ref, hn_ref, acc_ref, *, nj):
    j = pl.program_id(1)

    @pl.when(j == 0)
    def _():
        acc_ref[...] = jnp.zeros_like(acc_ref)

    x = x_ref[...]
    te = u_ref.shape[0]
    for s in range(te // min(te, DENSE_SUBTILE)):
        rows = slice(s * DENSE_SUBTILE, (s + 1) * DENSE_SUBTILE)
        hid = _dot_nt(x, u_ref[rows, :])
        a = (_gelu(hid) * w_ref[:, rows].astype(F32)).astype(BF16)
        acc_ref[...] += _dot(a, v_ref[rows, :])

    @pl.when(j == nj - 1)
    def _():
        hnew = h_ref[...] + acc_ref[...]
        ho_ref[...] = hnew
        hn_ref[...] = _rms(hnew, gn_ref[...]).astype(hn_ref.dtype)


def peer_dense(x, u, v, layer, w, h, gain_next, tm=1024, te=2048):
    n, d = x.shape
    ne = u.shape[1]
    tm = min(tm, n)
    te = min(te, ne)
    nj = ne // te
    rowspec_in = pl.BlockSpec((tm, d), lambda i, j: (i, 0), pipeline_mode=pl.Buffered(1))
    rowspec_out = pl.BlockSpec((tm, d), lambda i, j: (i, 0), pipeline_mode=pl.Buffered(1))
    return pl.pallas_call(
        functools.partial(_peer_dense_body, nj=nj),
        out_shape=[jax.ShapeDtypeStruct((n, d), F32), jax.ShapeDtypeStruct((n, d), BF16)],
        grid=(n // tm, nj),
        in_specs=[rowspec_in,
                  pl.BlockSpec((None, te, d), lambda i, j: (layer, j, 0)),
                  pl.BlockSpec((None, te, d), lambda i, j: (layer, j, 0)),
                  pl.BlockSpec((tm, te), lambda i, j: (i, j)),
                  rowspec_in,
                  pl.BlockSpec((1, d), lambda i, j: (0, 0))],
        out_specs=[rowspec_out, rowspec_out],
        scratch_shapes=[pltpu.VMEM((tm, d), F32)],
        compiler_params=_cparams(("parallel", "arbitrary")),
        name="peer_dense",
    )(x, u, v, w, h, gain_next.astype(F32).reshape(1, d))


def _fox_layer(hn, b, t, w_in, b_f, q_gain, k_gain):
    d = hn.shape[1]
    w = w_in.astype(BF16)
    qkvg = fox_proj(hn, w[:, 0:4 * d], q_gain, k_gain, FOX_HEAD_DIM ** -0.5, FOX_HEAD_DIM)
    lf = mm_logsig(hn, w[:, 4 * d:], b_f, 1.0)
    c = time_cumsum(lf.reshape(b, t, FOX_HEADS))
    c_rows = c.transpose(0, 2, 1).reshape(b, FOX_HEADS // 2, 2, t)
    o = fox_attention(qkvg.reshape(b, t, 4 * d), c_rows)
    return o.reshape(b * t, d)


def _gla_layer(hn, b, t, w_in, w_up, b_alpha, out_gain):
    d = hn.shape[1]
    gk = GLA_HEADS * GLA_DK
    w = w_in.astype(BF16)
    q = mm_plain(hn, w[:, 0:gk], F32, scale=GLA_DK ** -0.5)
    k = mm_plain(hn, w[:, gk:2 * gk], F32)
    v = mm_plain(hn, w[:, 2 * gk:2 * gk + d], BF16)
    r = mm_plain(hn, w[:, 2 * gk + d:2 * gk + 2 * d], BF16)
    low = mm_plain(hn, w[:, 2 * gk + 2 * d:], BF16)
    log_a = mm_logsig(low, w_up.astype(BF16), b_alpha, 1.0 / GLA_TAU)
    o = gated_linear_attention(q.reshape(b, t, gk), k.reshape(b, t, gk), log_a.reshape(b, t, gk),
                               v.reshape(b, t, d), r.reshape(b, t, d), out_gain,
                               GLA_HEADS, GLA_DK, GLA_DV)
    return o.reshape(b * t, d)


def _hgrn_layer(hn, b, t, layer, w_in, lb_logits, out_gain):
    d = hn.shape[1]
    w = w_in.astype(BF16)
    q = mm_plain(hn, w[:, 0:d], F32, scale=HGRN_DK ** -0.5)
    k, log_g = mm_hgrn_gate(hn, w[:, d:2 * d], lb_logits, layer)
    v = mm_plain(hn, w[:, 2 * d:3 * d], BF16)
    r = mm_plain(hn, w[:, 3 * d:4 * d], BF16)
    shp = (b, t, d)
    o = gated_linear_attention(q.reshape(shp), k.reshape(shp), log_g.reshape(shp),
                               v.reshape(shp), r.reshape(shp), out_gain,
                               HGRN_HEADS, HGRN_DK, HGRN_DV)
    return o.reshape(b * t, d)


def _peer_layer(h, hn, w_q, sub_keys, u, v, layer, gain_next):
    sk = sub_keys.astype(BF16).reshape(2 * PEER_HEADS, PEER_NKEYS, -1)
    w = peer_route_weights(hn, w_q.astype(BF16), sk)
    return peer_dense(hn, u, v, layer, w, h, gain_next)


def kernel(x, norm_mix, norm_ffn, fox_w_in, fox_b_f, fox_q_gain, fox_k_gain, fox_w_out,
           gla_w_in, gla_w_up, gla_b_alpha, gla_out_gain, gla_w_out,
           hgrn_w_in, hgrn_lb_logits, hgrn_out_gain, hgrn_w_out,
           peer_w_q, peer_sub_keys, peer_u, peer_v):
    b, t, d = x.shape
    depth = norm_mix.shape[0]
    h = x.reshape(b * t, d)
    hn = rmsnorm(h, norm_mix[0])
    u_all, v_all = peer_u.astype(BF16), peer_v.astype(BF16)
    for i in range(depth):
        m, j = i % N_MIXERS, i // N_MIXERS
        if m == 0:
            o = _fox_layer(hn, b, t, fox_w_in[j], fox_b_f[j], fox_q_gain[j], fox_k_gain[j])
            w_out = fox_w_out[j]
        elif m == 1:
            o = _gla_layer(hn, b, t, gla_w_in[j], gla_w_up[j], gla_b_alpha[j], gla_out_gain[j])
            w_out = gla_w_out[j]
        else:
            o = _hgrn_layer(hn, b, t, i, hgrn_w_in[j], hgrn_lb_logits, hgrn_out_gain[j])
            w_out = hgrn_w_out[j]
        h, hn = mm_res_norm(o, w_out.astype(BF16), h, norm_ffn[i])
        gain_next = norm_mix[(i + 1) % depth]
        h, hn = _peer_layer(h, hn, peer_w_q[i], peer_sub_keys[i], u_all, v_all, i, gain_next)
    return h.reshape(b, t, d)
```

```python
import functools
import math

import numpy as np
import jax
import jax.numpy as jnp
from jax import lax
from jax.experimental import pallas as pl
from jax.experimental.pallas import tpu as pltpu

F32 = jnp.float32
BF16 = jnp.bfloat16
EPS = 1e-6

DEPTH = 4
N_MIXERS = 3
CHUNK = 128
FOX_HEADS = 16
FOX_HEAD_DIM = 64
GLA_HEADS = 4
GLA_DK = 128
GLA_DV = 256
GLA_TAU = 16.0
HGRN_HEADS = 8
HGRN_DK = 128
HGRN_DV = 128
PEER_HEADS = 8
PEER_NKEYS = 128
PEER_TOPK = 16
LANES = 128
VMEM_LIMIT = 56 * 1024 * 1024


def _cparams(sem):
    return pltpu.CompilerParams(dimension_semantics=sem, vmem_limit_bytes=VMEM_LIMIT)


def _dot(a, b):
    return jnp.dot(a, b, preferred_element_type=F32)


def _dot_nt(a, b):
    return lax.dot_general(a, b, (((1,), (1,)), ((), ())), preferred_element_type=F32)


def _dot_tn(a, b):
    return lax.dot_general(a, b, (((0,), (0,)), ((), ())), preferred_element_type=F32)


def _split2(x):
    hi = x.astype(BF16)
    lo = (x - hi.astype(F32)).astype(BF16)
    return hi, lo


def _split3(x):
    hi = x.astype(BF16)
    r = x - hi.astype(F32)
    mid = r.astype(BF16)
    lo = (r - mid.astype(F32)).astype(BF16)
    return hi, mid, lo


def _log_sigmoid(x):
    return jnp.minimum(x, 0.0) - jnp.log1p(jnp.exp(-jnp.abs(x)))


def _sigmoid(x):
    return 1.0 / (1.0 + jnp.exp(-x))


def _rms(x, gain):
    ms = jnp.mean(x * x, axis=-1, keepdims=True)
    return x * lax.rsqrt(ms + EPS) * gain


def _rmsnorm_body(x_ref, g_ref, o_ref):
    o_ref[...] = _rms(x_ref[...], g_ref[...]).astype(o_ref.dtype)


def rmsnorm(x, gain, tm=512):
    n, d = x.shape
    tm = min(tm, n)
    return pl.pallas_call(
        _rmsnorm_body,
        out_shape=jax.ShapeDtypeStruct((n, d), BF16),
        grid=(n // tm,),
        in_specs=[pl.BlockSpec((tm, d), lambda i: (i, 0)),
                  pl.BlockSpec((1, d), lambda i: (0, 0))],
        out_specs=pl.BlockSpec((tm, d), lambda i: (i, 0)),
        compiler_params=_cparams(("parallel",)),
        name="rmsnorm",
    )(x, gain.reshape(1, d))


def _mm_call(body, x, w, extras, extra_specs, out_dtypes, tm, tn, name):
    n, k = x.shape
    m = w.shape[1]
    tm = min(tm, n)
    tn = min(tn, m)
    outs = [jax.ShapeDtypeStruct((n, m), dt) for dt in out_dtypes]
    return pl.pallas_call(
        body,
        out_shape=outs,
        grid=(n // tm, m // tn),
        in_specs=[pl.BlockSpec((tm, k), lambda i, j: (i, 0)),
                  pl.BlockSpec((k, tn), lambda i, j: (0, j))] + extra_specs,
        out_specs=[pl.BlockSpec((tm, tn), lambda i, j: (i, j)) for _ in outs],
        compiler_params=_cparams(("parallel", "arbitrary")),
        name=name,
    )(x, w, *extras)


def _mm_plain_body(x_ref, w_ref, o_ref, *, scale):
    acc = _dot(x_ref[...], w_ref[...])
    if scale != 1.0:
        acc = acc * scale
    o_ref[...] = acc.astype(o_ref.dtype)


def mm_plain(x, w, out_dtype, scale=1.0, tm=1024, tn=1024):
    return _mm_call(functools.partial(_mm_plain_body, scale=scale), x, w, [], [],
                    [out_dtype], tm, tn, "mm_plain")[0]


MXU_DIM = 256


def _fox_proj_body(x_ref, w_ref, bd_ref, g_ref, o_ref, *, hd):
    j = pl.program_id(1)
    acc = _dot(x_ref[...], w_ref[...])

    @pl.when(j < 2)
    def _():
        for c in range(acc.shape[1] // MXU_DIM):
            cols = slice(c * MXU_DIM, (c + 1) * MXU_DIM)
            a = acc[:, cols]
            ms = _dot((a * a).astype(BF16), bd_ref[...]) * (1.0 / hd)
            o_ref[:, cols] = (a * lax.rsqrt(ms + EPS) * g_ref[0, :, cols]).astype(o_ref.dtype)

    @pl.when(j >= 2)
    def _():
        o_ref[...] = acc.astype(o_ref.dtype)


def fox_proj(x, w, q_gain, k_gain, scale, hd, tm=1024):
    n, d = x.shape
    tm = min(tm, n)
    blk = np.arange(MXU_DIM) // hd
    bd = jnp.asarray(blk[:, None] == blk[None, :], BF16)
    g = jnp.stack([jnp.tile(q_gain.astype(F32) * scale, d // hd),
                   jnp.tile(k_gain.astype(F32), d // hd)]).reshape(2, 1, d)
    return pl.pallas_call(
        functools.partial(_fox_proj_body, hd=hd),
        out_shape=jax.ShapeDtypeStruct((n, 4 * d), BF16),
        grid=(n // tm, 4),
        in_specs=[pl.BlockSpec((tm, d), lambda i, j: (i, 0)),
                  pl.BlockSpec((d, d), lambda i, j: (0, j)),
                  pl.BlockSpec((MXU_DIM, MXU_DIM), lambda i, j: (0, 0)),
                  pl.BlockSpec((1, 1, d), lambda i, j: (jnp.minimum(j, 1), 0, 0))],
        out_specs=pl.BlockSpec((tm, d), lambda i, j: (i, j)),
        compiler_params=_cparams(("parallel", "arbitrary")),
        name="fox_proj",
    )(x, w, bd, g)


def _mm_logsig_body(x_ref, w_ref, b_ref, o_ref, *, mult):
    acc = _dot(x_ref[...], w_ref[...]) + b_ref[...]
    o_ref[...] = (_log_sigmoid(acc) * mult).astype(o_ref.dtype)


def mm_logsig(x, w, bias, mult, tm=512, tn=512):
    m = w.shape[1]
    tn = min(tn, m)
    return _mm_call(functools.partial(_mm_logsig_body, mult=mult), x, w,
                    [bias.astype(F32).reshape(1, m)],
                    [pl.BlockSpec((1, tn), lambda i, j: (0, j))],
                    [F32], tm, tn, "mm_logsig")[0]


def _mm_hgrn_gate_body(x_ref, w_ref, lbl_ref, k_ref, lg_ref, *, layer):
    acc = _dot(x_ref[...], w_ref[...])
    logits = lbl_ref[...]
    e = jnp.exp(logits - jnp.max(logits, axis=0, keepdims=True))
    p = e / jnp.sum(e, axis=0, keepdims=True)
    lb = jnp.zeros_like(p[0:1])
    for u in range(1, layer + 1):
        lb = lb + p[u:u + 1]
    gate = lb + (1.0 - lb) * _sigmoid(acc)
    k_ref[...] = (1.0 - gate).astype(k_ref.dtype)
    lg_ref[...] = jnp.log(gate).astype(lg_ref.dtype)


def mm_hgrn_gate(x, w, lb_logits, layer, tm=1024, tn=1024):
    m = w.shape[1]
    tn = min(tn, m)
    depth = lb_logits.shape[0]
    return _mm_call(functools.partial(_mm_hgrn_gate_body, layer=layer), x, w,
                    [lb_logits.astype(F32)],
                    [pl.BlockSpec((depth, tn), lambda i, j: (0, j))],
                    [F32, F32], tm, tn, "mm_hgrn_gate")


def _mm_res_norm_body(x_ref, w_ref, h_ref, g_ref, ho_ref, hn_ref):
    hnew = h_ref[...] + _dot(x_ref[...], w_ref[...])
    ho_ref[...] = hnew
    hn_ref[...] = _rms(hnew, g_ref[...]).astype(hn_ref.dtype)


def mm_res_norm(x, w, h, gain, tm=512):
    n, d = h.shape
    return _mm_call(_mm_res_norm_body, x, w, [h, gain.astype(F32).reshape(1, d)],
                    [pl.BlockSpec((min(tm, n), d), lambda i, j: (i, 0)),
                     pl.BlockSpec((1, d), lambda i, j: (0, 0))],
                    [F32, BF16], tm, d, "mm_res_norm")


CUM_BLOCK = 256


def _cumsum_body(x_ref, o_ref, *, t):
    nb = t // CUM_BLOCK
    r = lax.broadcasted_iota(jnp.int32, (CUM_BLOCK, CUM_BLOCK), 0)
    c = lax.broadcasted_iota(jnp.int32, (CUM_BLOCK, CUM_BLOCK), 1)
    tril = jnp.where(r >= c, 1.0, 0.0).astype(BF16)
    carry = jnp.zeros((1, x_ref.shape[-1]), F32)
    for b in range(nb):
        x = x_ref[0, b * CUM_BLOCK:(b + 1) * CUM_BLOCK, :]
        hi, mid, lo = _split3(x)
        cs = _dot(tril, hi) + _dot(tril, mid) + _dot(tril, lo) + carry
        o_ref[0, b * CUM_BLOCK:(b + 1) * CUM_BLOCK, :] = cs
        carry = cs[CUM_BLOCK - 1:CUM_BLOCK, :]


def time_cumsum(x):
    b, t, h = x.shape
    return pl.pallas_call(
        functools.partial(_cumsum_body, t=t),
        out_shape=jax.ShapeDtypeStruct((b, t, h), F32),
        grid=(b,),
        in_specs=[pl.BlockSpec((1, t, h), lambda i: (i, 0, 0))],
        out_specs=pl.BlockSpec((1, t, h), lambda i: (i, 0, 0)),
        compiler_params=_cparams(("parallel",)),
        name="time_cumsum",
    )(x)


FOX_PAIRS_PER_STEP = 2


def _fox_body(q_ref, k_ref, v_ref, g_ref, c_ref, o_ref, s_ref, m_ref, l_ref, acc_ref, *, tq):
    qi = pl.program_id(2)
    q = q_ref[0]
    width = q.shape[1]
    nh = width // FOX_HEAD_DIM
    lane = lax.broadcasted_iota(jnp.int32, (tq, width), 1)
    row = lax.broadcasted_iota(jnp.int32, (tq, tq), 0)
    col = lax.broadcasted_iota(jnp.int32, (tq, tq), 1)
    zero = jnp.zeros_like(q)
    head_of_lane = lane // FOX_HEAD_DIM
    qs = [jnp.where(head_of_lane == j, q, zero) for j in range(nh)]
    nfold = tq // LANES

    def fold(x, op):
        r = x[:, 0:LANES]
        for f in range(1, nfold):
            r = op(r, x[:, f * LANES:(f + 1) * LANES])
        return r

    def scores(j, off):
        ks = k_ref[0, pl.ds(off, tq), :]
        return _dot_nt(qs[j], ks) - c_ref[0, j // 2, pl.ds(j % 2, 1), pl.ds(off, tq)]

    def values(j, off):
        return v_ref[0, pl.ds(off, tq), (j // 2) * LANES:(j // 2 + 1) * LANES]

    m_ref[...] = jnp.full(m_ref.shape, -jnp.inf, F32)
    l_ref[...] = jnp.zeros(l_ref.shape, F32)
    acc_ref[...] = jnp.zeros(acc_ref.shape, F32)

    def pass1(kb, carry):
        off = pl.multiple_of(kb * tq, tq)
        for j in range(nh):
            s = scores(j, off)
            s_ref[j, :, pl.ds(off, tq)] = s
            m_ref[j] = jnp.maximum(m_ref[j], fold(s, jnp.maximum))
        return carry

    lax.fori_loop(0, qi, pass1, 0)
    offd = pl.multiple_of(qi * tq, tq)
    sd, ms = [], []
    for j in range(nh):
        s = jnp.where(row >= col, scores(j, offd), -jnp.inf)
        sd.append(s)
        ms.append(jnp.max(jnp.maximum(m_ref[j], fold(s, jnp.maximum)), axis=-1, keepdims=True))

    def pass2(kb, carry):
        off = pl.multiple_of(kb * tq, tq)
        for j in range(nh):
            p = jnp.exp(s_ref[j, :, pl.ds(off, tq)] - ms[j])
            l_ref[j] += fold(p, jnp.add)
            acc_ref[j] += _dot(p.astype(BF16), values(j, off))
        return carry

    lax.fori_loop(0, qi, pass2, 0)
    outs = []
    for j in range(nh):
        p = jnp.exp(sd[j] - ms[j])
        l = jnp.sum(l_ref[j] + fold(p, jnp.add), axis=-1, keepdims=True)
        outs.append((acc_ref[j] + _dot(p.astype(BF16), values(j, offd))) / l)
    lane1 = lax.broadcasted_iota(jnp.int32, (tq, LANES), 1)
    o = jnp.concatenate([jnp.where(lane1 < FOX_HEAD_DIM, outs[2 * pp], outs[2 * pp + 1])
                         for pp in range(nh // 2)], axis=1)
    o = o * _sigmoid(g_ref[0].astype(F32))
    o_ref[0] = o.astype(o_ref.dtype)


def fox_attention(qkvg, c_rows, tq=512):
    b, t, d4 = qkvg.shape
    d = d4 // 4
    tq = min(tq, t)
    pps = FOX_PAIRS_PER_STEP
    width = pps * LANES
    ngrp = d // width
    nh = 2 * pps
    qspec = pl.BlockSpec((1, tq, width), lambda bi, p, i: (bi, i, p))
    return pl.pallas_call(
        functools.partial(_fox_body, tq=tq),
        out_shape=jax.ShapeDtypeStruct((b, t, d), BF16),
        grid=(b, ngrp, t // tq),
        in_specs=[qspec,
                  pl.BlockSpec((1, t, width), lambda bi, p, i: (bi, 0, ngrp + p)),
                  pl.BlockSpec((1, t, width), lambda bi, p, i: (bi, 0, 2 * ngrp + p)),
                  pl.BlockSpec((1, tq, width), lambda bi, p, i: (bi, i, 3 * ngrp + p)),
                  pl.BlockSpec((1, pps, 2, t), lambda bi, p, i: (bi, p, 0, 0))],
        out_specs=qspec,
        scratch_shapes=[pltpu.VMEM((nh, tq, t), F32), pltpu.VMEM((nh, tq, LANES), F32),
                        pltpu.VMEM((nh, tq, LANES), F32), pltpu.VMEM((nh, tq, LANES), F32)],
        compiler_params=_cparams(("parallel", "parallel", "arbitrary")),
        name="fox_attention",
    )(qkvg, qkvg, qkvg, qkvg, c_rows)


_LEVELS = tuple(CHUNK >> s for s in range(CHUNK.bit_length() - 1))


def _gla_consts():
    c = CHUNK
    t = np.arange(c)
    u = t[None, :]
    blocks = [u <= t[:, None]]
    masks = [np.eye(c, dtype=bool)]
    for lv in _LEVELS:
        half = lv // 2
        blk, pos = t // lv, t % lv
        r = blk * lv + half - 1
        upper = pos >= half
        qrow = upper[:, None] & (u > r[:, None]) & (u <= t[:, None])
        krow = (~upper)[:, None] & (u > t[:, None]) & (u <= r[:, None])
        blocks.append(qrow | krow)
        masks.append((blk[:, None] == blk[None, :]) & upper[:, None] & (~upper)[None, :])
    blocks.append(u > t[:, None])
    mall = np.concatenate(blocks, axis=0).astype(np.float32)
    mask = np.stack(masks).astype(np.float32)
    mall = np.concatenate([mall, mall], axis=1)
    return jnp.asarray(mall, BF16), jnp.asarray(mask, F32)


def _gla_body(q_ref, k_ref, g_ref, v_ref, r_ref, gain_ref, mall_ref, mask_ref,
              o_ref, st_ref, *, nchunk, hg, dk, dv):
    c = CHUNK
    nl = len(_LEVELS)

    @pl.when(pl.program_id(2) == 0)
    def _():
        st_ref[...] = jnp.zeros_like(st_ref)

    def chunk(ci, carry):
        sl = pl.ds(pl.multiple_of(ci * c, c), c)
        g2 = jnp.concatenate(_split2(g_ref[0, sl, :]), axis=0)
        xall = jnp.exp(_dot(mall_ref[...], g2))
        for hh in range(hg):
            ks, vs = slice(hh * dk, (hh + 1) * dk), slice(hh * dv, (hh + 1) * dv)
            q = q_ref[0, sl, ks]
            k = k_ref[0, sl, ks]
            v = v_ref[0, sl, vs]
            x = xall[:, ks]
            scores = mask_ref[0] * _dot_nt(q.astype(BF16), k.astype(BF16))
            for lv in range(nl):
                z = x[(1 + lv) * c:(2 + lv) * c]
                scores = scores + mask_ref[1 + lv] * _dot_nt((q * z).astype(BF16), (k * z).astype(BF16))
            st = st_ref[hh]
            o = _dot(scores.astype(BF16), v) + _dot_nt((q * x[0:c]).astype(BF16), st.astype(BF16))
            kt = (k * x[(1 + nl) * c:(2 + nl) * c]).astype(BF16)
            st_ref[hh] = st * x[c - 1:c, :] + _dot_tn(v, kt)
            r = r_ref[0, sl, vs].astype(F32)
            y = _rms(o, gain_ref[hh]) * (r * _sigmoid(r))
            o_ref[0, sl, vs] = y.astype(o_ref.dtype)
        return carry

    lax.fori_loop(0, nchunk, chunk, 0, unroll=2)


def gated_linear_attention(q, k, g, v, r, gain, heads, dk, dv, hg=4, ts=512):
    b, t, _ = q.shape
    ts = min(ts, t)
    mall, mask = _gla_consts()
    qspec = pl.BlockSpec((1, ts, hg * dk), lambda bi, h, i: (bi, i, h))
    vspec = pl.BlockSpec((1, ts, hg * dv), lambda bi, h, i: (bi, i, h))
    return pl.pallas_call(
        functools.partial(_gla_body, nchunk=ts // CHUNK, hg=hg, dk=dk, dv=dv),
        out_shape=jax.ShapeDtypeStruct((b, t, heads * dv), BF16),
        grid=(b, heads // hg, t // ts),
        in_specs=[qspec, qspec, qspec, vspec, vspec,
                  pl.BlockSpec((hg, 1, dv), lambda bi, h, i: (h, 0, 0)),
                  pl.BlockSpec(mall.shape, lambda bi, h, i: (0, 0)),
                  pl.BlockSpec(mask.shape, lambda bi, h, i: (0, 0, 0))],
        out_specs=vspec,
        scratch_shapes=[pltpu.VMEM((hg, dv, dk), F32)],
        compiler_params=_cparams(("parallel", "parallel", "arbitrary")),
        name="gated_linear_attention",
    )(q, k, g, v, r, gain.astype(F32).reshape(heads, 1, dv), mall, mask)


def _peer_cand_layout():
    k = PEER_TOPK
    ab = [(0, b) for b in range(16)] + [(1, b) for b in range(8)]
    ab += [(2, b) if b < 5 else None for b in range(8)]
    ab += [(3, 0), (3, 1), (3, 2), (3, 3), (4, 0), (4, 1), (4, 2), None]
    ab += [(5, 0), (5, 1), (6, 0), (6, 1), (7, 0), (7, 1), None, None]
    ab += [(a, 0) for a in range(8, 16)]
    assert all(p is None or (p[0] + 1) * (p[1] + 1) <= k for p in ab)
    assert sum(p is not None for p in ab) == sum(k // (a + 1) for a in range(k))
    pos = np.array([PEER_POS_INVALID if p is None else p[0] * k + p[1] for p in ab], np.float32)
    tiles = pos.reshape(7, 8)
    first = tiles[0::2].reshape(-1)
    second = np.concatenate([tiles[1::2].reshape(-1), np.full(8, PEER_POS_INVALID, np.float32)])
    assert np.all((first < second) | (first == PEER_POS_INVALID))
    pos = np.concatenate([first, second])
    return np.broadcast_to(pos[:, None], (pos.size, LANES)).copy()


PEER_POS_INVALID = 1024.0


def _extract_max(s, ids, big):
    m = jnp.max(s, axis=0, keepdims=True)
    ix = jnp.min(jnp.where(s == m, ids, big), axis=0, keepdims=True)
    return m, ix


def _topk_pairs(a, b, ida, idb, big, k):
    a_wins = a >= b
    win, idw = jnp.maximum(a, b), jnp.where(a_wins, ida, idb)
    los, idl = jnp.minimum(a, b), jnp.where(a_wins, idb, ida)
    vals, sel = [], []
    for _ in range(k):
        m, ix = _extract_max(win, idw, big)
        vals.append(m)
        sel.append(ix)
        hit = idw == ix
        win = jnp.where(hit, los, win)
        idw = jnp.where(hit, idl, idw)
        los = jnp.where(hit, -jnp.inf, los)
    return jnp.concatenate(vals, axis=0), jnp.concatenate(sel, axis=0)


def _topk_keys(st, k):
    n = st.shape[0]
    a = jnp.concatenate([st[r:r + 8] for r in range(0, n, 16)], axis=0)
    b = jnp.concatenate([st[r + 8:r + 16] for r in range(0, n, 16)], axis=0)
    r = lax.broadcasted_iota(jnp.int32, a.shape, 0)
    ida = (((r >> 3) << 4) + (r & 7)).astype(F32)
    return _topk_pairs(a, b, ida, ida + 8.0, float(n), k)


def _route_head(q_ref, sk_ref, pos, h, tm):
    k = PEER_TOPK
    nk = PEER_NKEYS
    sub = lax.broadcasted_iota(jnp.int32, (8, tm), 0)

    def bc(x, r):
        return jnp.broadcast_to(x[r:r + 1], (8, tm))

    tops = []
    for p in range(2):
        hp = 2 * h + p
        qh = q_ref[:, pl.ds(pl.multiple_of(hp * nk, nk), nk)]
        st = _dot_nt(sk_ref[hp], qh)
        tops.append(_topk_keys(st, k))
    (s0, i0), (s1, i1) = tops
    lo1 = s1[0:8]
    tiles = [
        bc(s0, 0) + lo1,
        bc(s0, 0) + s1[8:16],
        bc(s0, 1) + lo1,
        bc(s0, 2) + lo1,
        jnp.where(sub < 4, bc(s0, 3), bc(s0, 4)) + jnp.where(sub < 4, lo1, pltpu.roll(lo1, 4, 0)),
        jnp.where(sub < 2, bc(s0, 5), jnp.where(sub < 4, bc(s0, 6), bc(s0, 7)))
        + jnp.where((sub & 1) == 0, bc(s1, 0), bc(s1, 1)),
        s0[8:16] + bc(s1, 0),
        jnp.full((8, tm), -jnp.inf, F32)]
    half = pos.shape[0] // 2
    pos_a, pos_b = pos[:half], pos[half:]
    ca = jnp.where(pos_a < PEER_POS_INVALID, jnp.concatenate(tiles[0::2], axis=0), -jnp.inf)
    cb = jnp.where(pos_b < PEER_POS_INVALID, jnp.concatenate(tiles[1::2], axis=0), -jnp.inf)
    best, bpos = _topk_pairs(ca, cb, pos_a, pos_b, 2.0 * PEER_POS_INVALID, k)
    bpos = bpos.astype(jnp.int32)
    ra = bpos >> 4
    rb = bpos & (k - 1)
    isel = jnp.zeros((k, tm), F32)
    jsel = jnp.zeros((k, tm), F32)
    for a in range(k):
        isel = jnp.where(ra == a, i0[a:a + 1], isel)
        jsel = jnp.where(rb == a, i1[a:a + 1], jsel)
    e = jnp.exp(best - best[0:1])
    gates = e / jnp.sum(e, axis=0, keepdims=True)
    return isel, jsel, gates


W3_GROUP = 8
W3_BATCH = 2 * W3_GROUP
ROUTE_HEADS_PER_TRIP = 2


def _peer_route_weights_body(x_ref, xn_ref, wq_ref, sk_ref, pos_ref, o_ref, q_ref, ri_ref, rj_ref,
                             rg_ref, pt_ref, w_ref, *, tm):
    k = PEER_TOPK
    nk = PEER_NKEYS
    ngroup = tm // LANES
    nbatch = tm // W3_BATCH
    per_head = nbatch // PEER_HEADS

    step = pl.program_id(0)
    cur = lax.rem(step, 2)
    q_cur = q_ref.at[cur]

    @pl.when(step == 0)
    def _():
        q_ref[0] = _dot(x_ref[...], wq_ref[...]).astype(q_ref.dtype)
        ri_ref[...] = jnp.zeros_like(ri_ref)
        rj_ref[...] = jnp.zeros_like(rj_ref)
        rg_ref[...] = jnp.zeros_like(rg_ref)

    for a, r_ref in enumerate((ri_ref, rj_ref, rg_ref)):
        for c in range(ngroup):
            pt_ref[a, c * LANES:(c + 1) * LANES, :] = r_ref[:, c * LANES:(c + 1) * LANES].T
    pos = jnp.concatenate([pos_ref[...]] * ngroup, axis=1)
    sub = lax.broadcasted_iota(jnp.int32, (nk, LANES), 0).astype(F32)

    def build(t):
        slot = lax.rem(t, 2 * ROUTE_HEADS_PER_TRIP * per_head)
        for half in range(2):
            tok0 = pl.multiple_of(t * W3_BATCH + half * W3_GROUP, W3_GROUP)
            it, jt, gt = (pt_ref[a, pl.ds(tok0, W3_GROUP), :] for a in range(3))
            for s in range(W3_GROUP):
                at = jnp.where(sub == it[s:s + 1], gt[s:s + 1], 0.0).astype(BF16)
                bt = jnp.where(sub == jt[s:s + 1], 1.0, 0.0).astype(BF16)
                w_ref[2 * slot + half, pl.ds(s, nk, stride=W3_GROUP), :] = _dot_nt(at, bt)

    def flush(t):
        slot = lax.rem(t, 2 * ROUTE_HEADS_PER_TRIP * per_head)
        r0 = pl.multiple_of(t * W3_BATCH, W3_BATCH)
        for i in range(nk):
            rows = jnp.concatenate([w_ref[2 * slot, i * W3_GROUP:(i + 1) * W3_GROUP, :],
                                    w_ref[2 * slot + 1, i * W3_GROUP:(i + 1) * W3_GROUP, :]], axis=0)
            o_ref[pl.ds(r0, W3_BATCH), i * nk:(i + 1) * nk] = rows.astype(o_ref.dtype)

    hpt = ROUTE_HEADS_PER_TRIP
    ntrip = PEER_HEADS // hpt
    qcols = q_ref.shape[2] // ntrip

    def do_heads(u, flush_prev):
        if flush_prev:
            for t in range(hpt * per_head):
                flush((u - 1) * hpt * per_head + t)
        for hh in range(hpt):
            h = u * hpt + hh
            isel, jsel, gates = _route_head(q_cur, sk_ref, pos, h, tm)
            rows = pl.ds(pl.multiple_of(h * k, k), k)
            ri_ref[rows, :] = isel
            rj_ref[rows, :] = jsel
            rg_ref[rows, :] = gates
            for t in range(per_head):
                build(h * per_head + t)
        cols = pl.ds(pl.multiple_of(u * qcols, qcols), qcols)
        q_ref[1 - cur, :, cols] = _dot(xn_ref[...], wq_ref[:, cols]).astype(q_ref.dtype)

    def trip(u, carry):
        do_heads(u, True)
        return carry

    do_heads(jnp.int32(0), False)
    lax.fori_loop(1, ntrip, trip, 0)
    for t in range(hpt * per_head):
        flush(jnp.int32((ntrip - 1) * hpt * per_head + t))


def peer_route_weights(x, w_q, sub_keys, tm=256):
    n, d = x.shape
    dq = w_q.shape[1]
    tm = min(tm, n)
    nblk = n // tm
    ne = PEER_NKEYS * PEER_NKEYS
    npair = PEER_HEADS * PEER_TOPK
    pos = jnp.asarray(_peer_cand_layout())
    return pl.pallas_call(
        functools.partial(_peer_route_weights_body, tm=tm),
        out_shape=jax.ShapeDtypeStruct((n, ne), BF16),
        grid=(nblk + 1,),
        in_specs=[pl.BlockSpec((tm, d), lambda s: (jnp.minimum(s, nblk - 1), 0)),
                  pl.BlockSpec((tm, d), lambda s: (jnp.minimum(s + 1, nblk - 1), 0)),
                  pl.BlockSpec((d, dq), lambda s: (0, 0), pipeline_mode=pl.Buffered(1)),
                  pl.BlockSpec(sub_keys.shape, lambda s: (0, 0, 0)),
                  pl.BlockSpec(pos.shape, lambda s: (0, 0))],
        out_specs=pl.BlockSpec((tm, ne), lambda s: (jnp.maximum(s - 1, 0), 0)),
        scratch_shapes=[pltpu.VMEM((2, tm, dq), BF16)]
        + [pltpu.VMEM((npair, tm), F32)] * 3 + [pltpu.VMEM((3, tm, npair), F32)]
        + [pltpu.VMEM((4 * ROUTE_HEADS_PER_TRIP * tm // (W3_BATCH * PEER_HEADS),
                       W3_GROUP * PEER_NKEYS, PEER_NKEYS), F32)],
        compiler_params=_cparams(("arbitrary",)),
        name="peer_route_weights",
    )(x, x, w_q, sub_keys, pos)


DENSE_SUBTILE = 1024


def _gelu(x):
    return 0.5 * x * (1.0 + lax.erf(x * (1.0 / math.sqrt(2.0))))


def _peer_dense_body(x_ref, u_ref, v_ref, w_ref, h_ref, gn_ref, ho_ref, hn_ref, acc_ref, *, nj):
    j = pl.program_id(1)

    @pl.when(j == 0)
    def _():
        acc_ref[...] = jnp.zeros_like(acc_ref)

    x = x_ref[...]
    te = u_ref.shape[0]
    for s in range(te // min(te, DENSE_SUBTILE)):
        rows = slice(s * DENSE_SUBTILE, (s + 1) * DENSE_SUBTILE)
        hid = _dot_nt(x, u_ref[rows, :].astype(BF16))
        a = (_gelu(hid) * w_ref[:, rows].astype(F32)).astype(BF16)
        acc_ref[...] += _dot(a, v_ref[rows, :].astype(BF16))

    @pl.when(j == nj - 1)
    def _():
        hnew = h_ref[...] + acc_ref[...]
        ho_ref[...] = hnew
        hn_ref[...] = _rms(hnew, gn_ref[...]).astype(hn_ref.dtype)


def peer_dense(x, u, v, layer, w, h, gain_next, tm=1024, te=1024):
    n, d = x.shape
    ne = u.shape[1]
    tm = min(tm, n)
    te = min(te, ne)
    nj = ne // te
    rowspec_in = pl.BlockSpec((tm, d), lambda i, j: (i, 0), pipeline_mode=pl.Buffered(1))
    rowspec_out = pl.BlockSpec((tm, d), lambda i, j: (i, 0), pipeline_mode=pl.Buffered(1))
    return pl.pallas_call(
        functools.partial(_peer_dense_body, nj=nj),
        out_shape=[jax.ShapeDtypeStruct((n, d), F32), jax.ShapeDtypeStruct((n, d), BF16)],
        grid=(n // tm, nj),
        in_specs=[rowspec_in,
                  pl.BlockSpec((None, te, d), lambda i, j: (layer, j, 0)),
                  pl.BlockSpec((None, te, d), lambda i, j: (layer, j, 0)),
                  pl.BlockSpec((tm, te), lambda i, j: (i, j)),
                  rowspec_in,
                  pl.BlockSpec((1, d), lambda i, j: (0, 0))],
        out_specs=[rowspec_out, rowspec_out],
        scratch_shapes=[pltpu.VMEM((tm, d), F32)],
        compiler_params=_cparams(("parallel", "arbitrary")),
        name="peer_dense",
    )(x, u, v, w, h, gain_next.astype(F32).reshape(1, d))


def _fox_layer(hn, b, t, w_in, b_f, q_gain, k_gain):
    d = hn.shape[1]
    w = w_in.astype(BF16)
    qkvg = fox_proj(hn, w[:, 0:4 * d], q_gain, k_gain, FOX_HEAD_DIM ** -0.5, FOX_HEAD_DIM)
    lf = mm_logsig(hn, w[:, 4 * d:], b_f, 1.0)
    c = time_cumsum(lf.reshape(b, t, FOX_HEADS))
    c_rows = c.transpose(0, 2, 1).reshape(b, FOX_HEADS // 2, 2, t)
    o = fox_attention(qkvg.reshape(b, t, 4 * d), c_rows)
    return o.reshape(b * t, d)


def _gla_layer(hn, b, t, w_in, w_up, b_alpha, out_gain):
    d = hn.shape[1]
    gk = GLA_HEADS * GLA_DK
    w = w_in.astype(BF16)
    q = mm_plain(hn, w[:, 0:gk], F32, scale=GLA_DK ** -0.5)
    k = mm_plain(hn, w[:, gk:2 * gk], F32)
    v = mm_plain(hn, w[:, 2 * gk:2 * gk + d], BF16)
    r = mm_plain(hn, w[:, 2 * gk + d:2 * gk + 2 * d], BF16)
    low = mm_plain(hn, w[:, 2 * gk + 2 * d:], BF16)
    log_a = mm_logsig(low, w_up.astype(BF16), b_alpha, 1.0 / GLA_TAU)
    o = gated_linear_attention(q.reshape(b, t, gk), k.reshape(b, t, gk), log_a.reshape(b, t, gk),
                               v.reshape(b, t, d), r.reshape(b, t, d), out_gain,
                               GLA_HEADS, GLA_DK, GLA_DV)
    return o.reshape(b * t, d)


def _hgrn_layer(hn, b, t, layer, w_in, lb_logits, out_gain):
    d = hn.shape[1]
    w = w_in.astype(BF16)
    q = mm_plain(hn, w[:, 0:d], F32, scale=HGRN_DK ** -0.5)
    k, log_g = mm_hgrn_gate(hn, w[:, d:2 * d], lb_logits, layer)
    v = mm_plain(hn, w[:, 2 * d:3 * d], BF16)
    r = mm_plain(hn, w[:, 3 * d:4 * d], BF16)
    shp = (b, t, d)
    o = gated_linear_attention(q.reshape(shp), k.reshape(shp), log_g.reshape(shp),
                               v.reshape(shp), r.reshape(shp), out_gain,
                               HGRN_HEADS, HGRN_DK, HGRN_DV)
    return o.reshape(b * t, d)


def _peer_layer(h, hn, w_q, sub_keys, u, v, layer, gain_next):
    sk = sub_keys.astype(BF16).reshape(2 * PEER_HEADS, PEER_NKEYS, -1)
    w = peer_route_weights(hn, w_q.astype(BF16), sk)
    return peer_dense(hn, u, v, layer, w, h, gain_next)


def kernel(x, norm_mix, norm_ffn, fox_w_in, fox_b_f, fox_q_gain, fox_k_gain, fox_w_out,
           gla_w_in, gla_w_up, gla_b_alpha, gla_out_gain, gla_w_out,
           hgrn_w_in, hgrn_lb_logits, hgrn_out_gain, hgrn_w_out,
           peer_w_q, peer_sub_keys, peer_u, peer_v):
    b, t, d = x.shape
    depth = norm_mix.shape[0]
    h = x.reshape(b * t, d)
    hn = rmsnorm(h, norm_mix[0])
    u_all, v_all = peer_u, peer_v
    for i in range(depth):
        m, j = i % N_MIXERS, i // N_MIXERS
        if m == 0:
            o = _fox_layer(hn, b, t, fox_w_in[j], fox_b_f[j], fox_q_gain[j], fox_k_gain[j])
            w_out = fox_w_out[j]
        elif m == 1:
            o = _gla_layer(hn, b, t, gla_w_in[j], gla_w_up[j], gla_b_alpha[j], gla_out_gain[j])
            w_out = gla_w_out[j]
        else:
            o = _hgrn_layer(hn, b, t, i, hgrn_w_in[j], hgrn_lb_logits, hgrn_out_gain[j])
            w_out = hgrn_w_out[j]
        h, hn = mm_res_norm(o, w_out.astype(BF16), h, norm_ffn[i])
        gain_next = norm_mix[(i + 1) % depth]
        h, hn = _peer_layer(h, hn, peer_w_q[i], peer_sub_keys[i], u_all, v_all, i, gain_next)
    return h.reshape(b, t, d)
```

```python
import functools
import math

import numpy as np
import jax
import jax.numpy as jnp
from jax import lax
from jax.experimental import pallas as pl
from jax.experimental.pallas import tpu as pltpu

F32 = jnp.float32
BF16 = jnp.bfloat16
EPS = 1e-6

DEPTH = 4
N_MIXERS = 3
CHUNK = 128
FOX_HEADS = 16
FOX_HEAD_DIM = 64
GLA_HEADS = 4
GLA_DK = 128
GLA_DV = 256
GLA_TAU = 16.0
HGRN_HEADS = 8
HGRN_DK = 128
HGRN_DV = 128
PEER_HEADS = 8
PEER_NKEYS = 128
PEER_TOPK = 16
LANES = 128
SUBLANES = 8
VMEM_LIMIT = 56 * 1024 * 1024


def _cparams(sem):
    return pltpu.CompilerParams(dimension_semantics=sem, vmem_limit_bytes=VMEM_LIMIT)


def _dot(a, b):
    return jnp.dot(a, b, preferred_element_type=F32)


def _dot_nt(a, b):
    return lax.dot_general(a, b, (((1,), (1,)), ((), ())), preferred_element_type=F32)


def _dot_tn(a, b):
    return lax.dot_general(a, b, (((0,), (0,)), ((), ())), preferred_element_type=F32)


def _split2(x):
    hi = x.astype(BF16)
    lo = (x - hi.astype(F32)).astype(BF16)
    return hi, lo


def _split3(x):
    hi = x.astype(BF16)
    r = x - hi.astype(F32)
    mid = r.astype(BF16)
    lo = (r - mid.astype(F32)).astype(BF16)
    return hi, mid, lo


def _log_sigmoid(x):
    return jnp.minimum(x, 0.0) - jnp.log1p(jnp.exp(-jnp.abs(x)))


def _sigmoid(x):
    return 1.0 / (1.0 + jnp.exp(-x))


def _rms(x, gain):
    ms = jnp.mean(x * x, axis=-1, keepdims=True)
    return x * lax.rsqrt(ms + EPS) * gain


def _rmsnorm_body(x_ref, g_ref, o_ref):
    o_ref[...] = _rms(x_ref[...], g_ref[...]).astype(o_ref.dtype)


def rmsnorm(x, gain, tm=512):
    n, d = x.shape
    tm = min(tm, n)
    return pl.pallas_call(
        _rmsnorm_body,
        out_shape=jax.ShapeDtypeStruct((n, d), BF16),
        grid=(n // tm,),
        in_specs=[pl.BlockSpec((tm, d), lambda i: (i, 0)),
                  pl.BlockSpec((1, d), lambda i: (0, 0))],
        out_specs=pl.BlockSpec((tm, d), lambda i: (i, 0)),
        compiler_params=_cparams(("parallel",)),
        name="rmsnorm",
    )(x, gain.reshape(1, d))


def _mm_call(body, x, w, extras, extra_specs, out_dtypes, tm, tn, name):
    n, k = x.shape
    m = w.shape[1]
    tm = min(tm, n)
    tn = min(tn, m)
    outs = [jax.ShapeDtypeStruct((n, m), dt) for dt in out_dtypes]
    return pl.pallas_call(
        body,
        out_shape=outs,
        grid=(n // tm, m // tn),
        in_specs=[pl.BlockSpec((tm, k), lambda i, j: (i, 0)),
                  pl.BlockSpec((k, tn), lambda i, j: (0, j))] + extra_specs,
        out_specs=[pl.BlockSpec((tm, tn), lambda i, j: (i, j)) for _ in outs],
        compiler_params=_cparams(("parallel", "arbitrary")),
        name=name,
    )(x, w, *extras)


def _mm_plain_body(x_ref, w_ref, o_ref, *, scale):
    acc = _dot(x_ref[...], w_ref[...])
    if scale != 1.0:
        acc = acc * scale
    o_ref[...] = acc.astype(o_ref.dtype)


def mm_plain(x, w, out_dtype, scale=1.0, tm=1024, tn=1024):
    return _mm_call(functools.partial(_mm_plain_body, scale=scale), x, w, [], [],
                    [out_dtype], tm, tn, "mm_plain")[0]


MXU_DIM = 256


def _fox_proj_body(x_ref, w_ref, bd_ref, g_ref, o_ref, *, hd):
    j = pl.program_id(1)
    acc = _dot(x_ref[...], w_ref[...])

    @pl.when(j < 2)
    def _():
        for c in range(acc.shape[1] // MXU_DIM):
            cols = slice(c * MXU_DIM, (c + 1) * MXU_DIM)
            a = acc[:, cols]
            ms = _dot((a * a).astype(BF16), bd_ref[...]) * (1.0 / hd)
            o_ref[:, cols] = (a * lax.rsqrt(ms + EPS) * g_ref[0, :, cols]).astype(o_ref.dtype)

    @pl.when(j >= 2)
    def _():
        o_ref[...] = acc.astype(o_ref.dtype)


def fox_proj(x, w, q_gain, k_gain, scale, hd, tm=1024):
    n, d = x.shape
    tm = min(tm, n)
    blk = np.arange(MXU_DIM) // hd
    bd = jnp.asarray(blk[:, None] == blk[None, :], BF16)
    g = jnp.stack([jnp.tile(q_gain.astype(F32) * scale, d // hd),
                   jnp.tile(k_gain.astype(F32), d // hd)]).reshape(2, 1, d)
    return pl.pallas_call(
        functools.partial(_fox_proj_body, hd=hd),
        out_shape=jax.ShapeDtypeStruct((n, 4 * d), BF16),
        grid=(n // tm, 4),
        in_specs=[pl.BlockSpec((tm, d), lambda i, j: (i, 0)),
                  pl.BlockSpec((d, d), lambda i, j: (0, j)),
                  pl.BlockSpec((MXU_DIM, MXU_DIM), lambda i, j: (0, 0)),
                  pl.BlockSpec((1, 1, d), lambda i, j: (jnp.minimum(j, 1), 0, 0))],
        out_specs=pl.BlockSpec((tm, d), lambda i, j: (i, j)),
        compiler_params=_cparams(("parallel", "arbitrary")),
        name="fox_proj",
    )(x, w, bd, g)


def _mm_logsig_body(x_ref, w_ref, b_ref, o_ref, *, mult):
    acc = _dot(x_ref[...], w_ref[...]) + b_ref[...]
    o_ref[...] = (_log_sigmoid(acc) * mult).astype(o_ref.dtype)


def mm_logsig(x, w, bias, mult, tm=512, tn=512):
    m = w.shape[1]
    tn = min(tn, m)
    return _mm_call(functools.partial(_mm_logsig_body, mult=mult), x, w,
                    [bias.astype(F32).reshape(1, m)],
                    [pl.BlockSpec((1, tn), lambda i, j: (0, j))],
                    [F32], tm, tn, "mm_logsig")[0]


def _mm_hgrn_gate_body(x_ref, w_ref, lbl_ref, k_ref, lg_ref, *, layer):
    acc = _dot(x_ref[...], w_ref[...])
    logits = lbl_ref[...]
    e = jnp.exp(logits - jnp.max(logits, axis=0, keepdims=True))
    p = e / jnp.sum(e, axis=0, keepdims=True)
    lb = jnp.zeros_like(p[0:1])
    for u in range(1, layer + 1):
        lb = lb + p[u:u + 1]
    gate = lb + (1.0 - lb) * _sigmoid(acc)
    k_ref[...] = (1.0 - gate).astype(k_ref.dtype)
    lg_ref[...] = jnp.log(gate).astype(lg_ref.dtype)


def mm_hgrn_gate(x, w, lb_logits, layer, tm=1024, tn=1024):
    m = w.shape[1]
    tn = min(tn, m)
    depth = lb_logits.shape[0]
    return _mm_call(functools.partial(_mm_hgrn_gate_body, layer=layer), x, w,
                    [lb_logits.astype(F32)],
                    [pl.BlockSpec((depth, tn), lambda i, j: (0, j))],
                    [F32, F32], tm, tn, "mm_hgrn_gate")


def _mm_res_norm_body(x_ref, w_ref, h_ref, g_ref, ho_ref, hn_ref):
    hnew = h_ref[...] + _dot(x_ref[...], w_ref[...])
    ho_ref[...] = hnew
    hn_ref[...] = _rms(hnew, g_ref[...]).astype(hn_ref.dtype)


def mm_res_norm(x, w, h, gain, tm=1024):
    n, d = h.shape
    return _mm_call(_mm_res_norm_body, x, w, [h, gain.astype(F32).reshape(1, d)],
                    [pl.BlockSpec((min(tm, n), d), lambda i, j: (i, 0)),
                     pl.BlockSpec((1, d), lambda i, j: (0, 0))],
                    [F32, BF16], tm, d, "mm_res_norm")


CUM_BLOCK = 256


def _cumsum_body(x_ref, o_ref, *, t):
    nb = t // CUM_BLOCK
    r = lax.broadcasted_iota(jnp.int32, (CUM_BLOCK, CUM_BLOCK), 0)
    c = lax.broadcasted_iota(jnp.int32, (CUM_BLOCK, CUM_BLOCK), 1)
    tril = jnp.where(r >= c, 1.0, 0.0).astype(BF16)
    carry = jnp.zeros((1, x_ref.shape[-1]), F32)
    for b in range(nb):
        x = x_ref[0, b * CUM_BLOCK:(b + 1) * CUM_BLOCK, :]
        hi, mid, lo = _split3(x)
        cs = _dot(tril, hi) + _dot(tril, mid) + _dot(tril, lo) + carry
        o_ref[0, b * CUM_BLOCK:(b + 1) * CUM_BLOCK, :] = cs
        carry = cs[CUM_BLOCK - 1:CUM_BLOCK, :]


def time_cumsum(x):
    b, t, h = x.shape
    return pl.pallas_call(
        functools.partial(_cumsum_body, t=t),
        out_shape=jax.ShapeDtypeStruct((b, t, h), F32),
        grid=(b,),
        in_specs=[pl.BlockSpec((1, t, h), lambda i: (i, 0, 0))],
        out_specs=pl.BlockSpec((1, t, h), lambda i: (i, 0, 0)),
        compiler_params=_cparams(("parallel",)),
        name="time_cumsum",
    )(x)


FOX_PAIRS_PER_STEP = 2


def _fox_body(q_ref, k_ref, v_ref, g_ref, c_ref, o_ref, s_ref, m_ref, l_ref, acc_ref, *, tq):
    qi = pl.program_id(2)
    q = q_ref[0]
    width = q.shape[1]
    nh = width // FOX_HEAD_DIM
    lane = lax.broadcasted_iota(jnp.int32, (tq, width), 1)
    row = lax.broadcasted_iota(jnp.int32, (tq, tq), 0)
    col = lax.broadcasted_iota(jnp.int32, (tq, tq), 1)
    zero = jnp.zeros_like(q)
    head_of_lane = lane // FOX_HEAD_DIM
    qs = [jnp.where(head_of_lane == j, q, zero) for j in range(nh)]
    nfold = tq // LANES

    def fold(x, op):
        r = x[:, 0:LANES]
        for f in range(1, nfold):
            r = op(r, x[:, f * LANES:(f + 1) * LANES])
        return r

    def scores(j, off):
        ks = k_ref[0, pl.ds(off, tq), :]
        return _dot_nt(qs[j], ks) - c_ref[0, j // 2, pl.ds(j % 2, 1), pl.ds(off, tq)]

    def values(j, off):
        return v_ref[0, pl.ds(off, tq), (j // 2) * LANES:(j // 2 + 1) * LANES]

    m_ref[...] = jnp.full(m_ref.shape, -jnp.inf, F32)
    l_ref[...] = jnp.zeros(l_ref.shape, F32)
    acc_ref[...] = jnp.zeros(acc_ref.shape, F32)

    def pass1(kb, carry):
        off = pl.multiple_of(kb * tq, tq)
        for j in range(nh):
            s = scores(j, off)
            s_ref[j, :, pl.ds(off, tq)] = s
            m_ref[j] = jnp.maximum(m_ref[j], fold(s, jnp.maximum))
        return carry

    lax.fori_loop(0, qi, pass1, 0)
    offd = pl.multiple_of(qi * tq, tq)
    sd, ms = [], []
    for j in range(nh):
        s = jnp.where(row >= col, scores(j, offd), -jnp.inf)
        sd.append(s)
        ms.append(jnp.max(jnp.maximum(m_ref[j], fold(s, jnp.maximum)), axis=-1, keepdims=True))

    def pass2(kb, carry):
        off = pl.multiple_of(kb * tq, tq)
        for j in range(nh):
            p = jnp.exp(s_ref[j, :, pl.ds(off, tq)] - ms[j])
            l_ref[j] += fold(p, jnp.add)
            acc_ref[j] += _dot(p.astype(BF16), values(j, off))
        return carry

    lax.fori_loop(0, qi, pass2, 0)
    outs = []
    for j in range(nh):
        p = jnp.exp(sd[j] - ms[j])
        l = jnp.sum(l_ref[j] + fold(p, jnp.add), axis=-1, keepdims=True)
        outs.append((acc_ref[j] + _dot(p.astype(BF16), values(j, offd))) / l)
    lane1 = lax.broadcasted_iota(jnp.int32, (tq, LANES), 1)
    o = jnp.concatenate([jnp.where(lane1 < FOX_HEAD_DIM, outs[2 * pp], outs[2 * pp + 1])
                         for pp in range(nh // 2)], axis=1)
    o = o * _sigmoid(g_ref[0].astype(F32))
    o_ref[0] = o.astype(o_ref.dtype)


def fox_attention(qkvg, c_rows, tq=512):
    b, t, d4 = qkvg.shape
    d = d4 // 4
    tq = min(tq, t)
    pps = FOX_PAIRS_PER_STEP
    width = pps * LANES
    ngrp = d // width
    nh = 2 * pps
    qspec = pl.BlockSpec((1, tq, width), lambda bi, p, i: (bi, i, p))
    return pl.pallas_call(
        functools.partial(_fox_body, tq=tq),
        out_shape=jax.ShapeDtypeStruct((b, t, d), BF16),
        grid=(b, ngrp, t // tq),
        in_specs=[qspec,
                  pl.BlockSpec((1, t, width), lambda bi, p, i: (bi, 0, ngrp + p)),
                  pl.BlockSpec((1, t, width), lambda bi, p, i: (bi, 0, 2 * ngrp + p)),
                  pl.BlockSpec((1, tq, width), lambda bi, p, i: (bi, i, 3 * ngrp + p)),
                  pl.BlockSpec((1, pps, 2, t), lambda bi, p, i: (bi, p, 0, 0))],
        out_specs=qspec,
        scratch_shapes=[pltpu.VMEM((nh, tq, t), F32), pltpu.VMEM((nh, tq, LANES), F32),
                        pltpu.VMEM((nh, tq, LANES), F32), pltpu.VMEM((nh, tq, LANES), F32)],
        compiler_params=_cparams(("parallel", "parallel", "arbitrary")),
        name="fox_attention",
    )(qkvg, qkvg, qkvg, qkvg, c_rows)


_LEVELS = tuple(CHUNK >> s for s in range(CHUNK.bit_length() - 1))


def _gla_consts():
    c = CHUNK
    t = np.arange(c)
    u = t[None, :]
    blocks = [u <= t[:, None]]
    masks = [np.eye(c, dtype=bool)]
    for lv in _LEVELS:
        half = lv // 2
        blk, pos = t // lv, t % lv
        r = blk * lv + half - 1
        upper = pos >= half
        qrow = upper[:, None] & (u > r[:, None]) & (u <= t[:, None])
        krow = (~upper)[:, None] & (u > t[:, None]) & (u <= r[:, None])
        blocks.append(qrow | krow)
        masks.append((blk[:, None] == blk[None, :]) & upper[:, None] & (~upper)[None, :])
    blocks.append(u > t[:, None])
    mall = np.concatenate(blocks, axis=0).astype(np.float32)
    mask = np.stack(masks).astype(np.float32)
    mall = np.concatenate([mall, mall], axis=1)
    return jnp.asarray(mall, BF16), jnp.asarray(mask, F32)


def _gla_body(q_ref, k_ref, g_ref, v_ref, r_ref, gain_ref, mall_ref, mask_ref,
              o_ref, st_ref, *, nchunk, hg, dk, dv):
    c = CHUNK
    nl = len(_LEVELS)

    @pl.when(pl.program_id(2) == 0)
    def _():
        st_ref[...] = jnp.zeros_like(st_ref)

    def chunk(ci, carry):
        sl = pl.ds(pl.multiple_of(ci * c, c), c)
        g2 = jnp.concatenate(_split2(g_ref[0, sl, :]), axis=0)
        xall = jnp.exp(_dot(mall_ref[...], g2))
        for hh in range(hg):
            ks, vs = slice(hh * dk, (hh + 1) * dk), slice(hh * dv, (hh + 1) * dv)
            q = q_ref[0, sl, ks]
            k = k_ref[0, sl, ks]
            v = v_ref[0, sl, vs]
            x = xall[:, ks]
            scores = mask_ref[0] * _dot_nt(q.astype(BF16), k.astype(BF16))
            for lv in range(nl):
                z = x[(1 + lv) * c:(2 + lv) * c]
                scores = scores + mask_ref[1 + lv] * _dot_nt((q * z).astype(BF16), (k * z).astype(BF16))
            st = st_ref[hh]
            o = _dot(scores.astype(BF16), v) + _dot_nt((q * x[0:c]).astype(BF16), st.astype(BF16))
            kt = (k * x[(1 + nl) * c:(2 + nl) * c]).astype(BF16)
            st_ref[hh] = st * x[c - 1:c, :] + _dot_tn(v, kt)
            r = r_ref[0, sl, vs].astype(F32)
            y = _rms(o, gain_ref[hh]) * (r * _sigmoid(r))
            o_ref[0, sl, vs] = y.astype(o_ref.dtype)
        return carry

    lax.fori_loop(0, nchunk, chunk, 0, unroll=2)


def gated_linear_attention(q, k, g, v, r, gain, heads, dk, dv, hg=4, ts=512):
    b, t, _ = q.shape
    ts = min(ts, t)
    mall, mask = _gla_consts()
    qspec = pl.BlockSpec((1, ts, hg * dk), lambda bi, h, i: (bi, i, h))
    vspec = pl.BlockSpec((1, ts, hg * dv), lambda bi, h, i: (bi, i, h))
    return pl.pallas_call(
        functools.partial(_gla_body, nchunk=ts // CHUNK, hg=hg, dk=dk, dv=dv),
        out_shape=jax.ShapeDtypeStruct((b, t, heads * dv), BF16),
        grid=(b, heads // hg, t // ts),
        in_specs=[qspec, qspec, qspec, vspec, vspec,
                  pl.BlockSpec((hg, 1, dv), lambda bi, h, i: (h, 0, 0)),
                  pl.BlockSpec(mall.shape, lambda bi, h, i: (0, 0)),
                  pl.BlockSpec(mask.shape, lambda bi, h, i: (0, 0, 0))],
        out_specs=vspec,
        scratch_shapes=[pltpu.VMEM((hg, dv, dk), F32)],
        compiler_params=_cparams(("parallel", "parallel", "arbitrary")),
        name="gated_linear_attention",
    )(q, k, g, v, r, gain.astype(F32).reshape(heads, 1, dv), mall, mask)


def _peer_cand_layout():
    k = PEER_TOPK
    ab = [(0, b) for b in range(16)] + [(1, b) for b in range(8)]
    ab += [(2, b) if b < 5 else None for b in range(8)]
    ab += [(3, 0), (3, 1), (3, 2), (3, 3), (4, 0), (4, 1), (4, 2), None]
    ab += [(5, 0), (5, 1), (6, 0), (6, 1), (7, 0), (7, 1), None, None]
    ab += [(a, 0) for a in range(8, 16)]
    assert all(p is None or (p[0] + 1) * (p[1] + 1) <= k for p in ab)
    assert sum(p is not None for p in ab) == sum(k // (a + 1) for a in range(k))
    pos = np.array([PEER_POS_INVALID if p is None else p[0] * k + p[1] for p in ab], np.float32)
    tiles = pos.reshape(-1, SUBLANES)
    first = tiles[0::2].reshape(-1)
    second = np.concatenate([tiles[1::2].reshape(-1), np.full(SUBLANES, PEER_POS_INVALID, np.float32)])
    assert np.all((first < second) | (first == PEER_POS_INVALID))
    pos = np.concatenate([first, second])
    return np.broadcast_to(pos[:, None], (pos.size, LANES)).copy()


PEER_POS_INVALID = 1024.0


def _extract_max(s, ids, big):
    m = jnp.max(s, axis=0, keepdims=True)
    ix = jnp.min(jnp.where(s == m, ids, big), axis=0, keepdims=True)
    return m, ix


def _topk_pairs(a, b, ida, idb, big, k):
    a_wins = a >= b
    win, idw = jnp.maximum(a, b), jnp.where(a_wins, ida, idb)
    los, idl = jnp.minimum(a, b), jnp.where(a_wins, idb, ida)
    vals, sel = [], []
    for _ in range(k):
        m, ix = _extract_max(win, idw, big)
        vals.append(m)
        sel.append(ix)
        hit = idw == ix
        win = jnp.where(hit, los, win)
        idw = jnp.where(hit, idl, idw)
        los = jnp.where(hit, -jnp.inf, los)
    return jnp.concatenate(vals, axis=0), jnp.concatenate(sel, axis=0)


def _topk_keys(st, k):
    n = st.shape[0]
    s8 = SUBLANES
    a = jnp.concatenate([st[r:r + s8] for r in range(0, n, 2 * s8)], axis=0)
    b = jnp.concatenate([st[r + s8:r + 2 * s8] for r in range(0, n, 2 * s8)], axis=0)
    r = lax.broadcasted_iota(jnp.int32, a.shape, 0)
    sh = s8.bit_length() - 1
    ida = (((r >> sh) << (sh + 1)) + (r & (s8 - 1))).astype(F32)
    return _topk_pairs(a, b, ida, ida + float(s8), float(n), k)


def _route_head(q_ref, sk_ref, pos, h, tm):
    k = PEER_TOPK
    nk = PEER_NKEYS
    s8 = SUBLANES
    sub = lax.broadcasted_iota(jnp.int32, (s8, tm), 0)

    def bc(x, r):
        return jnp.broadcast_to(x[r:r + 1], (s8, tm))

    tops = []
    for p in range(2):
        hp = 2 * h + p
        qh = q_ref[:, pl.ds(pl.multiple_of(hp * nk, nk), nk)]
        st = _dot_nt(sk_ref[hp], qh)
        tops.append(_topk_keys(st, k))
    (s0, i0), (s1, i1) = tops
    lo1 = s1[0:s8]
    tiles = [
        bc(s0, 0) + lo1,
        bc(s0, 0) + s1[s8:2 * s8],
        bc(s0, 1) + lo1,
        bc(s0, 2) + lo1,
        jnp.where(sub < 4, bc(s0, 3), bc(s0, 4)) + jnp.where(sub < 4, lo1, pltpu.roll(lo1, 4, 0)),
        jnp.where(sub < 2, bc(s0, 5), jnp.where(sub < 4, bc(s0, 6), bc(s0, 7)))
        + jnp.where((sub & 1) == 0, bc(s1, 0), bc(s1, 1)),
        s0[s8:2 * s8] + bc(s1, 0),
        jnp.full((s8, tm), -jnp.inf, F32)]
    half = pos.shape[0] // 2
    pos_a, pos_b = pos[:half], pos[half:]
    ca = jnp.where(pos_a < PEER_POS_INVALID, jnp.concatenate(tiles[0::2], axis=0), -jnp.inf)
    cb = jnp.where(pos_b < PEER_POS_INVALID, jnp.concatenate(tiles[1::2], axis=0), -jnp.inf)
    best, bpos = _topk_pairs(ca, cb, pos_a, pos_b, 2.0 * PEER_POS_INVALID, k)
    bpos = bpos.astype(jnp.int32)
    ra = bpos >> 4
    rb = bpos & (k - 1)
    isel = jnp.zeros((k, tm), F32)
    jsel = jnp.zeros((k, tm), F32)
    for a in range(k):
        isel = jnp.where(ra == a, i0[a:a + 1], isel)
        jsel = jnp.where(rb == a, i1[a:a + 1], jsel)
    e = jnp.exp(best - best[0:1])
    gates = e / jnp.sum(e, axis=0, keepdims=True)
    return isel, jsel, gates


W3_GROUP = SUBLANES
W3_BATCH = 2 * W3_GROUP
ROUTE_HEADS_PER_TRIP = 2


def _peer_route_weights_body(x_ref, xn_ref, wq_ref, sk_ref, pos_ref, o_ref, q_ref, ri_ref, rj_ref,
                             rg_ref, pt_ref, w_ref, *, tm):
    k = PEER_TOPK
    nk = PEER_NKEYS
    ngroup = tm // LANES
    nbatch = tm // W3_BATCH
    per_head = nbatch // PEER_HEADS

    step = pl.program_id(0)
    cur = lax.rem(step, 2)
    q_cur = q_ref.at[cur]

    @pl.when(step == 0)
    def _():
        q_ref[0] = _dot(x_ref[...], wq_ref[...]).astype(q_ref.dtype)
        ri_ref[...] = jnp.zeros_like(ri_ref)
        rj_ref[...] = jnp.zeros_like(rj_ref)
        rg_ref[...] = jnp.zeros_like(rg_ref)

    for a, r_ref in enumerate((ri_ref, rj_ref, rg_ref)):
        for c in range(ngroup):
            pt_ref[a, c * LANES:(c + 1) * LANES, :] = r_ref[:, c * LANES:(c + 1) * LANES].T
    pos = jnp.concatenate([pos_ref[...]] * ngroup, axis=1)
    sub = lax.broadcasted_iota(jnp.int32, (nk, LANES), 0).astype(F32)

    def build(t):
        slot = lax.rem(t, 2 * ROUTE_HEADS_PER_TRIP * per_head)
        for half in range(2):
            tok0 = pl.multiple_of(t * W3_BATCH + half * W3_GROUP, W3_GROUP)
            it, jt, gt = (pt_ref[a, pl.ds(tok0, W3_GROUP), :] for a in range(3))
            for s in range(W3_GROUP):
                at = jnp.where(sub == it[s:s + 1], gt[s:s + 1], 0.0).astype(BF16)
                bt = jnp.where(sub == jt[s:s + 1], 1.0, 0.0).astype(BF16)
                w_ref[2 * slot + half, pl.ds(s, nk, stride=W3_GROUP), :] = _dot_nt(at, bt)

    def flush(t):
        slot = lax.rem(t, 2 * ROUTE_HEADS_PER_TRIP * per_head)
        r0 = pl.multiple_of(t * W3_BATCH, W3_BATCH)
        for i in range(nk):
            rows = jnp.concatenate([w_ref[2 * slot, i * W3_GROUP:(i + 1) * W3_GROUP, :],
                                    w_ref[2 * slot + 1, i * W3_GROUP:(i + 1) * W3_GROUP, :]], axis=0)
            o_ref[pl.ds(r0, W3_BATCH), i * nk:(i + 1) * nk] = rows.astype(o_ref.dtype)

    hpt = ROUTE_HEADS_PER_TRIP
    ntrip = PEER_HEADS // hpt
    qcols = q_ref.shape[2] // ntrip

    def do_heads(u, flush_prev):
        if flush_prev:
            for t in range(hpt * per_head):
                flush((u - 1) * hpt * per_head + t)
        for hh in range(hpt):
            h = u * hpt + hh
            isel, jsel, gates = _route_head(q_cur, sk_ref, pos, h, tm)
            rows = pl.ds(pl.multiple_of(h * k, k), k)
            ri_ref[rows, :] = isel
            rj_ref[rows, :] = jsel
            rg_ref[rows, :] = gates
            for t in range(per_head):
                build(h * per_head + t)
        cols = pl.ds(pl.multiple_of(u * qcols, qcols), qcols)
        q_ref[1 - cur, :, cols] = _dot(xn_ref[...], wq_ref[:, cols]).astype(q_ref.dtype)

    def trip(u, carry):
        do_heads(u, True)
        return carry

    do_heads(jnp.int32(0), False)
    lax.fori_loop(1, ntrip, trip, 0)
    for t in range(hpt * per_head):
        flush(jnp.int32((ntrip - 1) * hpt * per_head + t))


def peer_route_weights(x, w_q, sub_keys, tm=256):
    n, d = x.shape
    dq = w_q.shape[1]
    tm = min(tm, n)
    nblk = n // tm
    ne = PEER_NKEYS * PEER_NKEYS
    npair = PEER_HEADS * PEER_TOPK
    pos = jnp.asarray(_peer_cand_layout())
    return pl.pallas_call(
        functools.partial(_peer_route_weights_body, tm=tm),
        out_shape=jax.ShapeDtypeStruct((n, ne), BF16),
        grid=(nblk + 1,),
        in_specs=[pl.BlockSpec((tm, d), lambda s: (jnp.minimum(s, nblk - 1), 0)),
                  pl.BlockSpec((tm, d), lambda s: (jnp.minimum(s + 1, nblk - 1), 0)),
                  pl.BlockSpec((d, dq), lambda s: (0, 0), pipeline_mode=pl.Buffered(1)),
                  pl.BlockSpec(sub_keys.shape, lambda s: (0, 0, 0)),
                  pl.BlockSpec(pos.shape, lambda s: (0, 0))],
        out_specs=pl.BlockSpec((tm, ne), lambda s: (jnp.maximum(s - 1, 0), 0)),
        scratch_shapes=[pltpu.VMEM((2, tm, dq), BF16)]
        + [pltpu.VMEM((npair, tm), F32)] * 3 + [pltpu.VMEM((3, tm, npair), F32)]
        + [pltpu.VMEM((4 * ROUTE_HEADS_PER_TRIP * tm // (W3_BATCH * PEER_HEADS),
                       W3_GROUP * PEER_NKEYS, PEER_NKEYS), F32)],
        compiler_params=_cparams(("arbitrary",)),
        name="peer_route_weights",
    )(x, x, w_q, sub_keys, pos)


DENSE_SUBTILE = 1024


def _gelu(x):
    return 0.5 * x * (1.0 + lax.erf(x * (1.0 / math.sqrt(2.0))))


def _peer_dense_body(x_ref, u_ref, v_ref, w_ref, h_ref, gn_ref, ho_ref, hn_ref, acc_ref, *, nj):
    j = pl.program_id(1)

    @pl.when(j == 0)
    def _():
        acc_ref[...] = jnp.zeros_like(acc_ref)

    x = x_ref[...]
    te = u_ref.shape[0]
    for s in range(te // min(te, DENSE_SUBTILE)):
        rows = slice(s * DENSE_SUBTILE, (s + 1) * DENSE_SUBTILE)
        hid = _dot_nt(x, u_ref[rows, :].astype(BF16))
        a = (_gelu(hid) * w_ref[:, rows].astype(F32)).astype(BF16)
        acc_ref[...] += _dot(a, v_ref[rows, :].astype(BF16))

    @pl.when(j == nj - 1)
    def _():
        hnew = h_ref[...] + acc_ref[...]
        ho_ref[...] = hnew
        hn_ref[...] = _rms(hnew, gn_ref[...]).astype(hn_ref.dtype)


def peer_dense(x, u, v, layer, w, h, gain_next, tm=1024, te=1024):
    n, d = x.shape
    ne = u.shape[1]
    tm = min(tm, n)
    te = min(te, ne)
    nj = ne // te
    rowspec_in = pl.BlockSpec((tm, d), lambda i, j: (i, 0), pipeline_mode=pl.Buffered(1))
    rowspec_out = pl.BlockSpec((tm, d), lambda i, j: (i, 0), pipeline_mode=pl.Buffered(1))
    return pl.pallas_call(
        functools.partial(_peer_dense_body, nj=nj),
        out_shape=[jax.ShapeDtypeStruct((n, d), F32), jax.ShapeDtypeStruct((n, d), BF16)],
        grid=(n // tm, nj),
        in_specs=[rowspec_in,
                  pl.BlockSpec((None, te, d), lambda i, j: (layer, j, 0)),
                  pl.BlockSpec((None, te, d), lambda i, j: (layer, j, 0)),
                  pl.BlockSpec((tm, te), lambda i, j: (i, j)),
                  rowspec_in,
                  pl.BlockSpec((1, d), lambda i, j: (0, 0))],
        out_specs=[rowspec_out, rowspec_out],
        scratch_shapes=[pltpu.VMEM((tm, d), F32)],
        compiler_params=_cparams(("parallel", "arbitrary")),
        name="peer_dense",
    )(x, u, v, w, h, gain_next.astype(F32).reshape(1, d))


def _fox_layer(hn, b, t, w_in, b_f, q_gain, k_gain):
    d = hn.shape[1]
    w = w_in.astype(BF16)
    qkvg = fox_proj(hn, w[:, 0:4 * d], q_gain, k_gain, FOX_HEAD_DIM ** -0.5, FOX_HEAD_DIM)
    lf = mm_logsig(hn, w[:, 4 * d:], b_f, 1.0)
    c = time_cumsum(lf.reshape(b, t, FOX_HEADS))
    c_rows = c.transpose(0, 2, 1).reshape(b, FOX_HEADS // 2, 2, t)
    o = fox_attention(qkvg.reshape(b, t, 4 * d), c_rows)
    return o.reshape(b * t, d)


def _gla_layer(hn, b, t, w_in, w_up, b_alpha, out_gain):
    d = hn.shape[1]
    gk = GLA_HEADS * GLA_DK
    w = w_in.astype(BF16)
    q = mm_plain(hn, w[:, 0:gk], F32, scale=GLA_DK ** -0.5)
    k = mm_plain(hn, w[:, gk:2 * gk], F32)
    v = mm_plain(hn, w[:, 2 * gk:2 * gk + d], BF16)
    r = mm_plain(hn, w[:, 2 * gk + d:2 * gk + 2 * d], BF16)
    low = mm_plain(hn, w[:, 2 * gk + 2 * d:], BF16)
    log_a = mm_logsig(low, w_up.astype(BF16), b_alpha, 1.0 / GLA_TAU)
    o = gated_linear_attention(q.reshape(b, t, gk), k.reshape(b, t, gk), log_a.reshape(b, t, gk),
                               v.reshape(b, t, d), r.reshape(b, t, d), out_gain,
                               GLA_HEADS, GLA_DK, GLA_DV)
    return o.reshape(b * t, d)


def _hgrn_layer(hn, b, t, layer, w_in, lb_logits, out_gain):
    d = hn.shape[1]
    w = w_in.astype(BF16)
    q = mm_plain(hn, w[:, 0:d], F32, scale=HGRN_DK ** -0.5)
    k, log_g = mm_hgrn_gate(hn, w[:, d:2 * d], lb_logits, layer)
    v = mm_plain(hn, w[:, 2 * d:3 * d], BF16)
    r = mm_plain(hn, w[:, 3 * d:4 * d], BF16)
    shp = (b, t, d)
    o = gated_linear_attention(q.reshape(shp), k.reshape(shp), log_g.reshape(shp),
                               v.reshape(shp), r.reshape(shp), out_gain,
                               HGRN_HEADS, HGRN_DK, HGRN_DV)
    return o.reshape(b * t, d)


def _peer_layer(h, hn, w_q, sub_keys, u, v, layer, gain_next):
    sk = sub_keys.astype(BF16).reshape(2 * PEER_HEADS, PEER_NKEYS, -1)
    w = peer_route_weights(hn, w_q.astype(BF16), sk)
    return peer_dense(hn, u, v, layer, w, h, gain_next)


def kernel(x, norm_mix, norm_ffn, fox_w_in, fox_b_f, fox_q_gain, fox_k_gain, fox_w_out,
           gla_w_in, gla_w_up, gla_b_alpha, gla_out_gain, gla_w_out,
           hgrn_w_in, hgrn_lb_logits, hgrn_out_gain, hgrn_w_out,
           peer_w_q, peer_sub_keys, peer_u, peer_v):
    b, t, d = x.shape
    depth = norm_mix.shape[0]
    h = x.reshape(b * t, d)
    hn = rmsnorm(h, norm_mix[0])
    u_all, v_all = peer_u, peer_v
    for i in range(depth):
        m, j = i % N_MIXERS, i // N_MIXERS
        if m == 0:
            o = _fox_layer(hn, b, t, fox_w_in[j], fox_b_f[j], fox_q_gain[j], fox_k_gain[j])
            w_out = fox_w_out[j]
        elif m == 1:
            o = _gla_layer(hn, b, t, gla_w_in[j], gla_w_up[j], gla_b_alpha[j], gla_out_gain[j])
            w_out = gla_w_out[j]
        else:
            o = _hgrn_layer(hn, b, t, i, hgrn_w_in[j], hgrn_lb_logits, hgrn_out_gain[j])
            w_out = hgrn_w_out[j]
        h, hn = mm_res_norm(o, w_out.astype(BF16), h, norm_ffn[i])
        gain_next = norm_mix[(i + 1) % depth]
        h, hn = _peer_layer(h, hn, peer_w_q[i], peer_sub_keys[i], u_all, v_all, i, gain_next)
    return h.reshape(b, t, d)
```

```python
import functools
import math

import numpy as np
import jax
import jax.numpy as jnp
from jax import lax
from jax.experimental import pallas as pl
from jax.experimental.pallas import tpu as pltpu

F32 = jnp.float32
BF16 = jnp.bfloat16
EPS = 1e-6

DEPTH = 4
N_MIXERS = 3
CHUNK = 128
FOX_HEADS = 16
FOX_HEAD_DIM = 64
GLA_HEADS = 4
GLA_DK = 128
GLA_DV = 256
GLA_TAU = 16.0
HGRN_HEADS = 8
HGRN_DK = 128
HGRN_DV = 128
PEER_HEADS = 8
PEER_NKEYS = 128
PEER_TOPK = 16
LANES = 128
SUBLANES = 8
VMEM_LIMIT = 56 * 1024 * 1024


def _cparams(sem):
    return pltpu.CompilerParams(dimension_semantics=sem, vmem_limit_bytes=VMEM_LIMIT)


def _dot(a, b):
    return jnp.dot(a, b, preferred_element_type=F32)


def _dot_nt(a, b):
    return lax.dot_general(a, b, (((1,), (1,)), ((), ())), preferred_element_type=F32)


def _dot_tn(a, b):
    return lax.dot_general(a, b, (((0,), (0,)), ((), ())), preferred_element_type=F32)


def _split2(x):
    hi = x.astype(BF16)
    lo = (x - hi.astype(F32)).astype(BF16)
    return hi, lo


def _split3(x):
    hi = x.astype(BF16)
    r = x - hi.astype(F32)
    mid = r.astype(BF16)
    lo = (r - mid.astype(F32)).astype(BF16)
    return hi, mid, lo


def _log_sigmoid(x):
    return jnp.minimum(x, 0.0) - jnp.log1p(jnp.exp(-jnp.abs(x)))


def _sigmoid(x):
    return 1.0 / (1.0 + jnp.exp(-x))


def _rms(x, gain):
    ms = jnp.mean(x * x, axis=-1, keepdims=True)
    return x * lax.rsqrt(ms + EPS) * gain


def _rmsnorm_body(x_ref, g_ref, o_ref):
    o_ref[...] = _rms(x_ref[...], g_ref[...]).astype(o_ref.dtype)


def rmsnorm(x, gain, tm=512):
    n, d = x.shape
    tm = min(tm, n)
    return pl.pallas_call(
        _rmsnorm_body,
        out_shape=jax.ShapeDtypeStruct((n, d), BF16),
        grid=(n // tm,),
        in_specs=[pl.BlockSpec((tm, d), lambda i: (i, 0)),
                  pl.BlockSpec((1, d), lambda i: (0, 0))],
        out_specs=pl.BlockSpec((tm, d), lambda i: (i, 0)),
        compiler_params=_cparams(("parallel",)),
        name="rmsnorm",
    )(x, gain.reshape(1, d))


def _mm_call(body, x, w, extras, extra_specs, out_dtypes, tm, tn, name):
    n, k = x.shape
    m = w.shape[1]
    tm = min(tm, n)
    tn = min(tn, m)
    outs = [jax.ShapeDtypeStruct((n, m), dt) for dt in out_dtypes]
    return pl.pallas_call(
        body,
        out_shape=outs,
        grid=(n // tm, m // tn),
        in_specs=[pl.BlockSpec((tm, k), lambda i, j: (i, 0)),
                  pl.BlockSpec((k, tn), lambda i, j: (0, j))] + extra_specs,
        out_specs=[pl.BlockSpec((tm, tn), lambda i, j: (i, j)) for _ in outs],
        compiler_params=_cparams(("parallel", "arbitrary")),
        name=name,
    )(x, w, *extras)


def _mm_plain_body(x_ref, w_ref, o_ref, *, scale):
    acc = _dot(x_ref[...], w_ref[...])
    if scale != 1.0:
        acc = acc * scale
    o_ref[...] = acc.astype(o_ref.dtype)


def mm_plain(x, w, out_dtype, scale=1.0, tm=1024, tn=1024):
    return _mm_call(functools.partial(_mm_plain_body, scale=scale), x, w, [], [],
                    [out_dtype], tm, tn, "mm_plain")[0]


MXU_DIM = 256


def _fox_proj_body(x_ref, w_ref, bd_ref, g_ref, o_ref, *, hd):
    j = pl.program_id(1)
    acc = _dot(x_ref[...], w_ref[...])

    @pl.when(j < 2)
    def _():
        for c in range(acc.shape[1] // MXU_DIM):
            cols = slice(c * MXU_DIM, (c + 1) * MXU_DIM)
            a = acc[:, cols]
            ms = _dot((a * a).astype(BF16), bd_ref[...]) * (1.0 / hd)
            o_ref[:, cols] = (a * lax.rsqrt(ms + EPS) * g_ref[0, :, cols]).astype(o_ref.dtype)

    @pl.when(j >= 2)
    def _():
        o_ref[...] = acc.astype(o_ref.dtype)


def fox_proj(x, w, q_gain, k_gain, scale, hd, tm=1024):
    n, d = x.shape
    tm = min(tm, n)
    blk = np.arange(MXU_DIM) // hd
    bd = jnp.asarray(blk[:, None] == blk[None, :], BF16)
    g = jnp.stack([jnp.tile(q_gain.astype(F32) * scale, d // hd),
                   jnp.tile(k_gain.astype(F32), d // hd)]).reshape(2, 1, d)
    return pl.pallas_call(
        functools.partial(_fox_proj_body, hd=hd),
        out_shape=jax.ShapeDtypeStruct((n, 4 * d), BF16),
        grid=(n // tm, 4),
        in_specs=[pl.BlockSpec((tm, d), lambda i, j: (i, 0)),
                  pl.BlockSpec((d, d), lambda i, j: (0, j)),
                  pl.BlockSpec((MXU_DIM, MXU_DIM), lambda i, j: (0, 0)),
                  pl.BlockSpec((1, 1, d), lambda i, j: (jnp.minimum(j, 1), 0, 0))],
        out_specs=pl.BlockSpec((tm, d), lambda i, j: (i, j)),
        compiler_params=_cparams(("parallel", "arbitrary")),
        name="fox_proj",
    )(x, w, bd, g)


def _mm_logsig_body(x_ref, w_ref, b_ref, o_ref, *, mult):
    acc = _dot(x_ref[...], w_ref[...]) + b_ref[...]
    o_ref[...] = (_log_sigmoid(acc) * mult).astype(o_ref.dtype)


def mm_logsig(x, w, bias, mult, tm=512, tn=512):
    m = w.shape[1]
    tn = min(tn, m)
    return _mm_call(functools.partial(_mm_logsig_body, mult=mult), x, w,
                    [bias.astype(F32).reshape(1, m)],
                    [pl.BlockSpec((1, tn), lambda i, j: (0, j))],
                    [F32], tm, tn, "mm_logsig")[0]


def _mm_hgrn_gate_body(x_ref, w_ref, lbl_ref, k_ref, lg_ref, *, layer):
    acc = _dot(x_ref[...], w_ref[...])
    logits = lbl_ref[...]
    e = jnp.exp(logits - jnp.max(logits, axis=0, keepdims=True))
    p = e / jnp.sum(e, axis=0, keepdims=True)
    lb = jnp.zeros_like(p[0:1])
    for u in range(1, layer + 1):
        lb = lb + p[u:u + 1]
    gate = lb + (1.0 - lb) * _sigmoid(acc)
    k_ref[...] = (1.0 - gate).astype(k_ref.dtype)
    lg_ref[...] = jnp.log(gate).astype(lg_ref.dtype)


def mm_hgrn_gate(x, w, lb_logits, layer, tm=1024, tn=1024):
    m = w.shape[1]
    tn = min(tn, m)
    depth = lb_logits.shape[0]
    return _mm_call(functools.partial(_mm_hgrn_gate_body, layer=layer), x, w,
                    [lb_logits.astype(F32)],
                    [pl.BlockSpec((depth, tn), lambda i, j: (0, j))],
                    [F32, F32], tm, tn, "mm_hgrn_gate")


def _mm_res_norm_body(x_ref, w_ref, h_ref, g_ref, ho_ref, hn_ref):
    hnew = h_ref[...] + _dot(x_ref[...], w_ref[...])
    ho_ref[...] = hnew
    hn_ref[...] = _rms(hnew, g_ref[...]).astype(hn_ref.dtype)


def mm_res_norm(x, w, h, gain, tm=1024):
    n, d = h.shape
    return _mm_call(_mm_res_norm_body, x, w, [h, gain.astype(F32).reshape(1, d)],
                    [pl.BlockSpec((min(tm, n), d), lambda i, j: (i, 0)),
                     pl.BlockSpec((1, d), lambda i, j: (0, 0))],
                    [F32, BF16], tm, d, "mm_res_norm")


CUM_BLOCK = 256


def _cumsum_body(x_ref, o_ref, *, t):
    nb = t // CUM_BLOCK
    r = lax.broadcasted_iota(jnp.int32, (CUM_BLOCK, CUM_BLOCK), 0)
    c = lax.broadcasted_iota(jnp.int32, (CUM_BLOCK, CUM_BLOCK), 1)
    tril = jnp.where(r >= c, 1.0, 0.0).astype(BF16)
    carry = jnp.zeros((1, x_ref.shape[-1]), F32)
    for b in range(nb):
        x = x_ref[0, b * CUM_BLOCK:(b + 1) * CUM_BLOCK, :]
        hi, mid, lo = _split3(x)
        cs = _dot(tril, hi) + _dot(tril, mid) + _dot(tril, lo) + carry
        o_ref[0, b * CUM_BLOCK:(b + 1) * CUM_BLOCK, :] = cs
        carry = cs[CUM_BLOCK - 1:CUM_BLOCK, :]


def time_cumsum(x):
    b, t, h = x.shape
    return pl.pallas_call(
        functools.partial(_cumsum_body, t=t),
        out_shape=jax.ShapeDtypeStruct((b, t, h), F32),
        grid=(b,),
        in_specs=[pl.BlockSpec((1, t, h), lambda i: (i, 0, 0))],
        out_specs=pl.BlockSpec((1, t, h), lambda i: (i, 0, 0)),
        compiler_params=_cparams(("parallel",)),
        name="time_cumsum",
    )(x)


FOX_PAIRS_PER_STEP = 2


def _fox_body(q_ref, k_ref, v_ref, g_ref, c_ref, o_ref, s_ref, m_ref, l_ref, acc_ref, *, tq):
    qi = pl.program_id(2)
    q = q_ref[0]
    width = q.shape[1]
    nh = width // FOX_HEAD_DIM
    lane = lax.broadcasted_iota(jnp.int32, (tq, width), 1)
    row = lax.broadcasted_iota(jnp.int32, (tq, tq), 0)
    col = lax.broadcasted_iota(jnp.int32, (tq, tq), 1)
    zero = jnp.zeros_like(q)
    head_of_lane = lane // FOX_HEAD_DIM
    qs = [jnp.where(head_of_lane == j, q, zero) for j in range(nh)]
    nfold = tq // LANES

    def fold(x, op):
        r = x[:, 0:LANES]
        for f in range(1, nfold):
            r = op(r, x[:, f * LANES:(f + 1) * LANES])
        return r

    def scores(j, off):
        ks = k_ref[0, pl.ds(off, tq), :]
        return _dot_nt(qs[j], ks) - c_ref[0, j // 2, pl.ds(j % 2, 1), pl.ds(off, tq)]

    def values(j, off):
        return v_ref[0, pl.ds(off, tq), (j // 2) * LANES:(j // 2 + 1) * LANES]

    m_ref[...] = jnp.full(m_ref.shape, -jnp.inf, F32)
    l_ref[...] = jnp.zeros(l_ref.shape, F32)
    acc_ref[...] = jnp.zeros(acc_ref.shape, F32)

    def pass1(kb, carry):
        off = pl.multiple_of(kb * tq, tq)
        for j in range(nh):
            s = scores(j, off)
            s_ref[j, :, pl.ds(off, tq)] = s
            m_ref[j] = jnp.maximum(m_ref[j], fold(s, jnp.maximum))
        return carry

    lax.fori_loop(0, qi, pass1, 0)
    offd = pl.multiple_of(qi * tq, tq)
    sd, ms = [], []
    for j in range(nh):
        s = jnp.where(row >= col, scores(j, offd), -jnp.inf)
        sd.append(s)
        ms.append(jnp.max(jnp.maximum(m_ref[j], fold(s, jnp.maximum)), axis=-1, keepdims=True))

    def pass2(kb, carry):
        off = pl.multiple_of(kb * tq, tq)
        for j in range(nh):
            p = jnp.exp(s_ref[j, :, pl.ds(off, tq)] - ms[j])
            l_ref[j] += fold(p, jnp.add)
            acc_ref[j] += _dot(p.astype(BF16), values(j, off))
        return carry

    lax.fori_loop(0, qi, pass2, 0)
    outs = []
    for j in range(nh):
        p = jnp.exp(sd[j] - ms[j])
        l = jnp.sum(l_ref[j] + fold(p, jnp.add), axis=-1, keepdims=True)
        outs.append((acc_ref[j] + _dot(p.astype(BF16), values(j, offd))) / l)
    lane1 = lax.broadcasted_iota(jnp.int32, (tq, LANES), 1)
    o = jnp.concatenate([jnp.where(lane1 < FOX_HEAD_DIM, outs[2 * pp], outs[2 * pp + 1])
                         for pp in range(nh // 2)], axis=1)
    o = o * _sigmoid(g_ref[0].astype(F32))
    o_ref[0] = o.astype(o_ref.dtype)


def fox_attention(qkvg, c_rows, tq=512):
    b, t, d4 = qkvg.shape
    d = d4 // 4
    tq = min(tq, t)
    pps = FOX_PAIRS_PER_STEP
    width = pps * LANES
    ngrp = d // width
    nh = 2 * pps
    qspec = pl.BlockSpec((1, tq, width), lambda bi, p, i: (bi, i, p))
    return pl.pallas_call(
        functools.partial(_fox_body, tq=tq),
        out_shape=jax.ShapeDtypeStruct((b, t, d), BF16),
        grid=(b, ngrp, t // tq),
        in_specs=[qspec,
                  pl.BlockSpec((1, t, width), lambda bi, p, i: (bi, 0, ngrp + p)),
                  pl.BlockSpec((1, t, width), lambda bi, p, i: (bi, 0, 2 * ngrp + p)),
                  pl.BlockSpec((1, tq, width), lambda bi, p, i: (bi, i, 3 * ngrp + p)),
                  pl.BlockSpec((1, pps, 2, t), lambda bi, p, i: (bi, p, 0, 0))],
        out_specs=qspec,
        scratch_shapes=[pltpu.VMEM((nh, tq, t), F32), pltpu.VMEM((nh, tq, LANES), F32),
                        pltpu.VMEM((nh, tq, LANES), F32), pltpu.VMEM((nh, tq, LANES), F32)],
        compiler_params=_cparams(("parallel", "parallel", "arbitrary")),
        name="fox_attention",
    )(qkvg, qkvg, qkvg, qkvg, c_rows)


_LEVELS = tuple(CHUNK >> s for s in range(CHUNK.bit_length() - 1))


def _gla_consts():
    c = CHUNK
    t = np.arange(c)
    u = t[None, :]
    blocks = [u <= t[:, None]]
    masks = [np.eye(c, dtype=bool)]
    for lv in _LEVELS:
        half = lv // 2
        blk, pos = t // lv, t % lv
        r = blk * lv + half - 1
        upper = pos >= half
        qrow = upper[:, None] & (u > r[:, None]) & (u <= t[:, None])
        krow = (~upper)[:, None] & (u > t[:, None]) & (u <= r[:, None])
        blocks.append(qrow | krow)
        masks.append((blk[:, None] == blk[None, :]) & upper[:, None] & (~upper)[None, :])
    blocks.append(u > t[:, None])
    mall = np.concatenate(blocks, axis=0).astype(np.float32)
    mask = np.stack(masks).astype(np.float32)
    mall = np.concatenate([mall, mall], axis=1)
    return jnp.asarray(mall, BF16), jnp.asarray(mask, F32)


def _gla_body(q_ref, k_ref, g_ref, v_ref, r_ref, gain_ref, mall_ref, mask_ref,
              o_ref, st_ref, *, nchunk, hg, dk, dv):
    c = CHUNK
    nl = len(_LEVELS)

    @pl.when(pl.program_id(2) == 0)
    def _():
        st_ref[...] = jnp.zeros_like(st_ref)

    def chunk(ci, carry):
        sl = pl.ds(pl.multiple_of(ci * c, c), c)
        g2 = jnp.concatenate(_split2(g_ref[0, sl, :]), axis=0)
        xall = jnp.exp(_dot(mall_ref[...], g2))
        for hh in range(hg):
            ks, vs = slice(hh * dk, (hh + 1) * dk), slice(hh * dv, (hh + 1) * dv)
            q = q_ref[0, sl, ks]
            k = k_ref[0, sl, ks]
            v = v_ref[0, sl, vs]
            x = xall[:, ks]
            scores = mask_ref[0] * _dot_nt(q.astype(BF16), k.astype(BF16))
            for lv in range(nl):
                z = x[(1 + lv) * c:(2 + lv) * c]
                scores = scores + mask_ref[1 + lv] * _dot_nt((q * z).astype(BF16), (k * z).astype(BF16))
            st = st_ref[hh]
            o = _dot(scores.astype(BF16), v) + _dot_nt((q * x[0:c]).astype(BF16), st.astype(BF16))
            kt = (k * x[(1 + nl) * c:(2 + nl) * c]).astype(BF16)
            st_ref[hh] = st * x[c - 1:c, :] + _dot_tn(v, kt)
            r = r_ref[0, sl, vs].astype(F32)
            y = _rms(o, gain_ref[hh]) * (r * _sigmoid(r))
            o_ref[0, sl, vs] = y.astype(o_ref.dtype)
        return carry

    lax.fori_loop(0, nchunk, chunk, 0, unroll=2)


def gated_linear_attention(q, k, g, v, r, gain, heads, dk, dv, hg=4, ts=512):
    b, t, _ = q.shape
    ts = min(ts, t)
    mall, mask = _gla_consts()
    qspec = pl.BlockSpec((1, ts, hg * dk), lambda bi, h, i: (bi, i, h))
    vspec = pl.BlockSpec((1, ts, hg * dv), lambda bi, h, i: (bi, i, h))
    return pl.pallas_call(
        functools.partial(_gla_body, nchunk=ts // CHUNK, hg=hg, dk=dk, dv=dv),
        out_shape=jax.ShapeDtypeStruct((b, t, heads * dv), BF16),
        grid=(b, heads // hg, t // ts),
        in_specs=[qspec, qspec, qspec, vspec, vspec,
                  pl.BlockSpec((hg, 1, dv), lambda bi, h, i: (h, 0, 0)),
                  pl.BlockSpec(mall.shape, lambda bi, h, i: (0, 0)),
                  pl.BlockSpec(mask.shape, lambda bi, h, i: (0, 0, 0))],
        out_specs=vspec,
        scratch_shapes=[pltpu.VMEM((hg, dv, dk), F32)],
        compiler_params=_cparams(("parallel", "parallel", "arbitrary")),
        name="gated_linear_attention",
    )(q, k, g, v, r, gain.astype(F32).reshape(heads, 1, dv), mall, mask)


def _peer_cand_layout():
    k = PEER_TOPK
    ab = [(0, b) for b in range(16)] + [(1, b) for b in range(8)]
    ab += [(2, b) if b < 5 else None for b in range(8)]
    ab += [(3, 0), (3, 1), (3, 2), (3, 3), (4, 0), (4, 1), (4, 2), None]
    ab += [(5, 0), (5, 1), (6, 0), (6, 1), (7, 0), (7, 1), None, None]
    ab += [(a, 0) for a in range(8, 16)]
    assert all(p is None or (p[0] + 1) * (p[1] + 1) <= k for p in ab)
    assert sum(p is not None for p in ab) == sum(k // (a + 1) for a in range(k))
    pos = np.array([PEER_POS_INVALID if p is None else p[0] * k + p[1] for p in ab], np.float32)
    tiles = pos.reshape(-1, SUBLANES)
    first = tiles[0::2].reshape(-1)
    second = np.concatenate([tiles[1::2].reshape(-1), np.full(SUBLANES, PEER_POS_INVALID, np.float32)])
    assert np.all((first < second) | (first == PEER_POS_INVALID))
    pos = np.concatenate([first, second])
    return np.broadcast_to(pos[:, None], (pos.size, LANES)).copy()


PEER_POS_INVALID = 1024.0


def _extract_max(s, ids, big):
    m = jnp.max(s, axis=0, keepdims=True)
    ix = jnp.min(jnp.where(s == m, ids, big), axis=0, keepdims=True)
    return m, ix


def _topk_pairs(a, b, ida, idb, big, k):
    a_wins = a >= b
    win, idw = jnp.maximum(a, b), jnp.where(a_wins, ida, idb)
    los, idl = jnp.minimum(a, b), jnp.where(a_wins, idb, ida)
    vals, sel = [], []
    for _ in range(k):
        m, ix = _extract_max(win, idw, big)
        vals.append(m)
        sel.append(ix)
        hit = idw == ix
        win = jnp.where(hit, los, win)
        idw = jnp.where(hit, idl, idw)
        los = jnp.where(hit, -jnp.inf, los)
    return jnp.concatenate(vals, axis=0), jnp.concatenate(sel, axis=0)


def _topk_keys(st, k):
    n = st.shape[0]
    s8 = SUBLANES
    a = jnp.concatenate([st[r:r + s8] for r in range(0, n, 2 * s8)], axis=0)
    b = jnp.concatenate([st[r + s8:r + 2 * s8] for r in range(0, n, 2 * s8)], axis=0)
    r = lax.broadcasted_iota(jnp.int32, a.shape, 0)
    sh = s8.bit_length() - 1
    ida = (((r >> sh) << (sh + 1)) + (r & (s8 - 1))).astype(F32)
    return _topk_pairs(a, b, ida, ida + float(s8), float(n), k)


def _route_head(q_ref, sk_ref, pos, h, tm):
    k = PEER_TOPK
    nk = PEER_NKEYS
    s8 = SUBLANES
    sub = lax.broadcasted_iota(jnp.int32, (s8, tm), 0)

    def bc(x, r):
        return jnp.broadcast_to(x[r:r + 1], (s8, tm))

    tops = []
    for p in range(2):
        hp = 2 * h + p
        qh = q_ref[:, pl.ds(pl.multiple_of(hp * nk, nk), nk)]
        st = _dot_nt(sk_ref[hp], qh)
        tops.append(_topk_keys(st, k))
    (s0, i0), (s1, i1) = tops
    lo1 = s1[0:s8]
    tiles = [
        bc(s0, 0) + lo1,
        bc(s0, 0) + s1[s8:2 * s8],
        bc(s0, 1) + lo1,
        bc(s0, 2) + lo1,
        jnp.where(sub < 4, bc(s0, 3), bc(s0, 4)) + jnp.where(sub < 4, lo1, pltpu.roll(lo1, 4, 0)),
        jnp.where(sub < 2, bc(s0, 5), jnp.where(sub < 4, bc(s0, 6), bc(s0, 7)))
        + jnp.where((sub & 1) == 0, bc(s1, 0), bc(s1, 1)),
        s0[s8:2 * s8] + bc(s1, 0),
        jnp.full((s8, tm), -jnp.inf, F32)]
    half = pos.shape[0] // 2
    pos_a, pos_b = pos[:half], pos[half:]
    ca = jnp.where(pos_a < PEER_POS_INVALID, jnp.concatenate(tiles[0::2], axis=0), -jnp.inf)
    cb = jnp.where(pos_b < PEER_POS_INVALID, jnp.concatenate(tiles[1::2], axis=0), -jnp.inf)
    best, bpos = _topk_pairs(ca, cb, pos_a, pos_b, 2.0 * PEER_POS_INVALID, k)
    bpos = bpos.astype(jnp.int32)
    ra = bpos >> (k.bit_length() - 1)
    rb = bpos & (k - 1)
    isel = jnp.zeros((k, tm), F32)
    jsel = jnp.zeros((k, tm), F32)
    for a in range(k):
        isel = jnp.where(ra == a, i0[a:a + 1], isel)
        jsel = jnp.where(rb == a, i1[a:a + 1], jsel)
    e = jnp.exp(best - best[0:1])
    gates = e / jnp.sum(e, axis=0, keepdims=True)
    return isel, jsel, gates


W3_GROUP = SUBLANES
W3_BATCH = 2 * W3_GROUP
ROUTE_HEADS_PER_TRIP = 2


def _peer_route_weights_body(x_ref, xn_ref, wq_ref, sk_ref, pos_ref, o_ref, q_ref, ri_ref, rj_ref,
                             rg_ref, pt_ref, w_ref, *, tm):
    k = PEER_TOPK
    nk = PEER_NKEYS
    ngroup = tm // LANES
    nbatch = tm // W3_BATCH
    per_head = nbatch // PEER_HEADS

    step = pl.program_id(0)
    cur = lax.rem(step, 2)
    q_cur = q_ref.at[cur]

    @pl.when(step == 0)
    def _():
        q_ref[0] = _dot(x_ref[...], wq_ref[...]).astype(q_ref.dtype)
        ri_ref[...] = jnp.zeros_like(ri_ref)
        rj_ref[...] = jnp.zeros_like(rj_ref)
        rg_ref[...] = jnp.zeros_like(rg_ref)

    for a, r_ref in enumerate((ri_ref, rj_ref, rg_ref)):
        for c in range(ngroup):
            pt_ref[a, c * LANES:(c + 1) * LANES, :] = r_ref[:, c * LANES:(c + 1) * LANES].T
    pos = jnp.concatenate([pos_ref[...]] * ngroup, axis=1)
    sub = lax.broadcasted_iota(jnp.int32, (nk, LANES), 0).astype(F32)

    def build(t):
        slot = lax.rem(t, 2 * ROUTE_HEADS_PER_TRIP * per_head)
        for half in range(2):
            tok0 = pl.multiple_of(t * W3_BATCH + half * W3_GROUP, W3_GROUP)
            it, jt, gt = (pt_ref[a, pl.ds(tok0, W3_GROUP), :] for a in range(3))
            for s in range(W3_GROUP):
                at = jnp.where(sub == it[s:s + 1], gt[s:s + 1], 0.0).astype(BF16)
                bt = jnp.where(sub == jt[s:s + 1], 1.0, 0.0).astype(BF16)
                w_ref[2 * slot + half, pl.ds(s, nk, stride=W3_GROUP), :] = _dot_nt(at, bt)

    def flush(t):
        slot = lax.rem(t, 2 * ROUTE_HEADS_PER_TRIP * per_head)
        r0 = pl.multiple_of(t * W3_BATCH, W3_BATCH)
        for i in range(nk):
            rows = jnp.concatenate([w_ref[2 * slot, i * W3_GROUP:(i + 1) * W3_GROUP, :],
                                    w_ref[2 * slot + 1, i * W3_GROUP:(i + 1) * W3_GROUP, :]], axis=0)
            o_ref[pl.ds(r0, W3_BATCH), i * nk:(i + 1) * nk] = rows.astype(o_ref.dtype)

    hpt = ROUTE_HEADS_PER_TRIP
    ntrip = PEER_HEADS // hpt
    qcols = q_ref.shape[2] // ntrip

    def do_heads(u, flush_prev):
        if flush_prev:
            for t in range(hpt * per_head):
                flush((u - 1) * hpt * per_head + t)
        for hh in range(hpt):
            h = u * hpt + hh
            isel, jsel, gates = _route_head(q_cur, sk_ref, pos, h, tm)
            rows = pl.ds(pl.multiple_of(h * k, k), k)
            ri_ref[rows, :] = isel
            rj_ref[rows, :] = jsel
            rg_ref[rows, :] = gates
            for t in range(per_head):
                build(h * per_head + t)
        cols = pl.ds(pl.multiple_of(u * qcols, qcols), qcols)
        q_ref[1 - cur, :, cols] = _dot(xn_ref[...], wq_ref[:, cols]).astype(q_ref.dtype)

    def trip(u, carry):
        do_heads(u, True)
        return carry

    do_heads(jnp.int32(0), False)
    lax.fori_loop(1, ntrip, trip, 0)
    for t in range(hpt * per_head):
        flush(jnp.int32((ntrip - 1) * hpt * per_head + t))


def peer_route_weights(x, w_q, sub_keys, tm=256):
    n, d = x.shape
    dq = w_q.shape[1]
    tm = min(tm, n)
    nblk = n // tm
    ne = PEER_NKEYS * PEER_NKEYS
    npair = PEER_HEADS * PEER_TOPK
    pos = jnp.asarray(_peer_cand_layout())
    return pl.pallas_call(
        functools.partial(_peer_route_weights_body, tm=tm),
        out_shape=jax.ShapeDtypeStruct((n, ne), BF16),
        grid=(nblk + 1,),
        in_specs=[pl.BlockSpec((tm, d), lambda s: (jnp.minimum(s, nblk - 1), 0)),
                  pl.BlockSpec((tm, d), lambda s: (jnp.minimum(s + 1, nblk - 1), 0)),
                  pl.BlockSpec((d, dq), lambda s: (0, 0), pipeline_mode=pl.Buffered(1)),
                  pl.BlockSpec(sub_keys.shape, lambda s: (0, 0, 0)),
                  pl.BlockSpec(pos.shape, lambda s: (0, 0))],
        out_specs=pl.BlockSpec((tm, ne), lambda s: (jnp.maximum(s - 1, 0), 0)),
        scratch_shapes=[pltpu.VMEM((2, tm, dq), BF16)]
        + [pltpu.VMEM((npair, tm), F32)] * 3 + [pltpu.VMEM((3, tm, npair), F32)]
        + [pltpu.VMEM((4 * ROUTE_HEADS_PER_TRIP * tm // (W3_BATCH * PEER_HEADS),
                       W3_GROUP * PEER_NKEYS, PEER_NKEYS), F32)],
        compiler_params=_cparams(("arbitrary",)),
        name="peer_route_weights",
    )(x, x, w_q, sub_keys, pos)


DENSE_SUBTILE = 1024


def _gelu(x):
    return 0.5 * x * (1.0 + lax.erf(x * (1.0 / math.sqrt(2.0))))


def _peer_dense_body(x_ref, u_ref, v_ref, w_ref, h_ref, gn_ref, ho_ref, hn_ref, acc_ref, *, nj):
    j = pl.program_id(1)

    @pl.when(j == 0)
    def _():
        acc_ref[...] = jnp.zeros_like(acc_ref)

    x = x_ref[...]
    te = u_ref.shape[0]
    for s in range(te // min(te, DENSE_SUBTILE)):
        rows = slice(s * DENSE_SUBTILE, (s + 1) * DENSE_SUBTILE)
        hid = _dot_nt(x, u_ref[rows, :].astype(BF16))
        a = (_gelu(hid) * w_ref[:, rows].astype(F32)).astype(BF16)
        acc_ref[...] += _dot(a, v_ref[rows, :].astype(BF16))

    @pl.when(j == nj - 1)
    def _():
        hnew = h_ref[...] + acc_ref[...]
        ho_ref[...] = hnew
        hn_ref[...] = _rms(hnew, gn_ref[...]).astype(hn_ref.dtype)


def peer_dense(x, u, v, layer, w, h, gain_next, tm=1024, te=1024):
    n, d = x.shape
    ne = u.shape[1]
    tm = min(tm, n)
    te = min(te, ne)
    nj = ne // te
    rowspec_in = pl.BlockSpec((tm, d), lambda i, j: (i, 0), pipeline_mode=pl.Buffered(1))
    rowspec_out = pl.BlockSpec((tm, d), lambda i, j: (i, 0), pipeline_mode=pl.Buffered(1))
    return pl.pallas_call(
        functools.partial(_peer_dense_body, nj=nj),
        out_shape=[jax.ShapeDtypeStruct((n, d), F32), jax.ShapeDtypeStruct((n, d), BF16)],
        grid=(n // tm, nj),
        in_specs=[rowspec_in,
                  pl.BlockSpec((None, te, d), lambda i, j: (layer, j, 0)),
                  pl.BlockSpec((None, te, d), lambda i, j: (layer, j, 0)),
                  pl.BlockSpec((tm, te), lambda i, j: (i, j)),
                  rowspec_in,
                  pl.BlockSpec((1, d), lambda i, j: (0, 0))],
        out_specs=[rowspec_out, rowspec_out],
        scratch_shapes=[pltpu.VMEM((tm, d), F32)],
        compiler_params=_cparams(("parallel", "arbitrary")),
        name="peer_dense",
    )(x, u, v, w, h, gain_next.astype(F32).reshape(1, d))


def _fox_layer(hn, b, t, w_in, b_f, q_gain, k_gain):
    d = hn.shape[1]
    w = w_in.astype(BF16)
    qkvg = fox_proj(hn, w[:, 0:4 * d], q_gain, k_gain, FOX_HEAD_DIM ** -0.5, FOX_HEAD_DIM)
    lf = mm_logsig(hn, w[:, 4 * d:], b_f, 1.0)
    c = time_cumsum(lf.reshape(b, t, FOX_HEADS))
    c_rows = c.transpose(0, 2, 1).reshape(b, FOX_HEADS // 2, 2, t)
    o = fox_attention(qkvg.reshape(b, t, 4 * d), c_rows)
    return o.reshape(b * t, d)


def _gla_layer(hn, b, t, w_in, w_up, b_alpha, out_gain):
    d = hn.shape[1]
    gk = GLA_HEADS * GLA_DK
    w = w_in.astype(BF16)
    q = mm_plain(hn, w[:, 0:gk], F32, scale=GLA_DK ** -0.5)
    k = mm_plain(hn, w[:, gk:2 * gk], F32)
    v = mm_plain(hn, w[:, 2 * gk:2 * gk + d], BF16)
    r = mm_plain(hn, w[:, 2 * gk + d:2 * gk + 2 * d], BF16)
    low = mm_plain(hn, w[:, 2 * gk + 2 * d:], BF16)
    log_a = mm_logsig(low, w_up.astype(BF16), b_alpha, 1.0 / GLA_TAU)
    o = gated_linear_attention(q.reshape(b, t, gk), k.reshape(b, t, gk), log_a.reshape(b, t, gk),
                               v.reshape(b, t, d), r.reshape(b, t, d), out_gain,
                               GLA_HEADS, GLA_DK, GLA_DV)
    return o.reshape(b * t, d)


def _hgrn_layer(hn, b, t, layer, w_in, lb_logits, out_gain):
    d = hn.shape[1]
    w = w_in.astype(BF16)
    q = mm_plain(hn, w[:, 0:d], F32, scale=HGRN_DK ** -0.5)
    k, log_g = mm_hgrn_gate(hn, w[:, d:2 * d], lb_logits, layer)
    v = mm_plain(hn, w[:, 2 * d:3 * d], BF16)
    r = mm_plain(hn, w[:, 3 * d:4 * d], BF16)
    shp = (b, t, d)
    o = gated_linear_attention(q.reshape(shp), k.reshape(shp), log_g.reshape(shp),
                               v.reshape(shp), r.reshape(shp), out_gain,
                               HGRN_HEADS, HGRN_DK, HGRN_DV)
    return o.reshape(b * t, d)


def _peer_layer(h, hn, w_q, sub_keys, u, v, layer, gain_next):
    sk = sub_keys.astype(BF16).reshape(2 * PEER_HEADS, PEER_NKEYS, -1)
    w = peer_route_weights(hn, w_q.astype(BF16), sk)
    return peer_dense(hn, u, v, layer, w, h, gain_next)


def kernel(x, norm_mix, norm_ffn, fox_w_in, fox_b_f, fox_q_gain, fox_k_gain, fox_w_out,
           gla_w_in, gla_w_up, gla_b_alpha, gla_out_gain, gla_w_out,
           hgrn_w_in, hgrn_lb_logits, hgrn_out_gain, hgrn_w_out,
           peer_w_q, peer_sub_keys, peer_u, peer_v):
    b, t, d = x.shape
    depth = norm_mix.shape[0]
    h = x.reshape(b * t, d)
    hn = rmsnorm(h, norm_mix[0])
    u_all, v_all = peer_u, peer_v
    for i in range(depth):
        m, j = i % N_MIXERS, i // N_MIXERS
        if m == 0:
            o = _fox_layer(hn, b, t, fox_w_in[j], fox_b_f[j], fox_q_gain[j], fox_k_gain[j])
            w_out = fox_w_out[j]
        elif m == 1:
            o = _gla_layer(hn, b, t, gla_w_in[j], gla_w_up[j], gla_b_alpha[j], gla_out_gain[j])
            w_out = gla_w_out[j]
        else:
            o = _hgrn_layer(hn, b, t, i, hgrn_w_in[j], hgrn_lb_logits, hgrn_out_gain[j])
            w_out = hgrn_w_out[j]
        h, hn = mm_res_norm(o, w_out.astype(BF16), h, norm_ffn[i])
        gain_next = norm_mix[(i + 1) % depth]
        h, hn = _peer_layer(h, hn, peer_w_q[i], peer_sub_keys[i], u_all, v_all, i, gain_next)
    return h.reshape(b, t, d)
```
